```python
import numpy as np
import jax
import jax.numpy as jnp
from jax import lax

D_MODEL = 2048
BATCH = 4
SEQ = 2048
DEPTH = 4
DEC_BATCH = 8
DEC_SEQ = 1
PAST_LEN = 16384
PAGE_SIZE = 128

N_A_LAYERS = DEPTH // 2
N_B_LAYERS = DEPTH - N_A_LAYERS
N_DENSE = (DEPTH + 1) // 2
N_MOE = DEPTH // 2

GM_CHUNK = 128
GM_DIM = D_MODEL
GM_GROUPS = 16
GM_GW = GM_DIM // GM_GROUPS

N_HEADS = 16
HEAD_DIM = 128
KV_HEADS = 4
Q_PER_KV = N_HEADS // KV_HEADS
ROT_DIM = HEAD_DIM // 4
ROPE_THETA = 500000.0
CMP_LEN = 32
CMP_STRIDE = 16
CMP_R = CMP_LEN // CMP_STRIDE
CMP_HID = HEAD_DIM
SEL_BLOCK = 64
SEL_TOP = 16
WINDOW = 512
Q_BLOCK = 128
N_BRANCH = 3
N_KV_COLS = 6 * KV_HEADS * HEAD_DIM
ATTN_SCALE = HEAD_DIM ** -0.5

FF_DENSE = 5632
N_EXPERTS = 8
TOP_K = 2
FF_EXPERT = 7168
MOE_BLOCK = 128

EPS = 1e-6
NEG = -1e30
FORCE_SCORE = 1e6

kernel_name = 'yoco_gmlp_nsa_decoder_step'


def _rmsnorm(x, g):
    x32 = x.astype(jnp.float32)
    y = x32 * lax.rsqrt(jnp.mean(x32 * x32, -1, keepdims=True) + EPS)
    return y.astype(x.dtype) * g


def _layernorm(x, g, b):
    x32 = x.astype(jnp.float32)
    mu = jnp.mean(x32, -1, keepdims=True)
    xc = x32 - mu
    y = xc * lax.rsqrt(jnp.mean(xc * xc, -1, keepdims=True) + EPS)
    return y.astype(x.dtype) * g + b


def _rope(x, pos):
    half = ROT_DIM // 2
    inv = ROPE_THETA ** (-jnp.arange(half, dtype=jnp.float32) / half)
    ang = pos.astype(jnp.float32)[:, None] * inv[None, :]
    cos = jnp.cos(ang)[:, None, :]
    sin = jnp.sin(ang)[:, None, :]
    xr = x[..., :ROT_DIM].astype(jnp.float32)
    x1, x2 = xr[..., :half], xr[..., half:]
    rot = jnp.concatenate([x1 * cos - x2 * sin, x2 * cos + x1 * sin], -1).astype(x.dtype)
    return jnp.concatenate([rot, x[..., ROT_DIM:]], -1)


def _chunk_gmlp(h, w_in, ln_g, ln_b, w_s, b_s, w_out):
    B, T, _ = h.shape
    L = min(GM_CHUNK, T)
    z = jax.nn.gelu(h @ w_in)
    u, v = z[..., :GM_DIM], z[..., GM_DIM:]
    v = _layernorm(v, ln_g, ln_b)
    tri = jnp.tril(jnp.ones((L, L), dtype=bool))
    ws = jnp.where(tri[None], w_s[:, :L, :L], 0.0)
    vg = v.reshape(B, T // L, L, GM_GROUPS, GM_GW)
    s = jnp.einsum('gts,bcsgd->bctgd', ws, vg) + b_s[:, :L].T[None, None, :, :, None]
    out = u * s.reshape(B, T, GM_DIM)
    return out @ w_out, v


def _swiglu(h, wg, wu, wd):
    return (jax.nn.silu(h @ wg) * (h @ wu)) @ wd


def _moe(h, router, wg, wu, wd):
    B, T, D = h.shape
    n = B * T
    x = h.reshape(n, D)
    logits = (x @ router).astype(jnp.float32)
    top_v, top_e = lax.top_k(logits, TOP_K)
    gate = jax.nn.softmax(top_v, -1)
    a = n * TOP_K
    e_flat = top_e.reshape(a)
    tok = jnp.arange(a, dtype=jnp.int32) // TOP_K
    order = jnp.argsort(e_flat)
    e_s, tok_s, w_s = e_flat[order], tok[order], gate.reshape(a)[order]
    blk = min(MOE_BLOCK, a)
    n_blk = -(-a // blk) + N_EXPERTS
    counts = jnp.bincount(e_flat, length=N_EXPERTS)
    padded = (counts + blk - 1) // blk * blk
    end_pad = jnp.cumsum(padded)
    start_pad = end_pad - padded
    start = jnp.cumsum(counts) - counts
    dest = start_pad[e_s] + jnp.arange(a, dtype=jnp.int32) - start[e_s]
    row_tok = jnp.full((n_blk * blk,), n, dtype=jnp.int32).at[dest].set(tok_s)
    xp = jnp.concatenate([x, jnp.zeros((1, D), x.dtype)], 0)
    xb = xp[row_tok].reshape(n_blk, blk, D)
    blk_e = jnp.minimum(jnp.searchsorted(end_pad, jnp.arange(n_blk) * blk, side='right'), N_EXPERTS - 1)

    def expert(args):
        xi, e = args
        return (jax.nn.silu(xi @ wg[e]) * (xi @ wu[e])) @ wd[e]

    yb = lax.map(expert, (xb, blk_e)).reshape(n_blk * blk, D)
    y = jnp.zeros((n, D), h.dtype).at[tok_s].add(w_s.astype(h.dtype)[:, None] * yb[dest])
    return y.reshape(B, T, D)


def _kv_rows(hkv, pos, kv_w):
    B, T, _ = hkv.shape
    r = (hkv @ kv_w).reshape(B, T, 6, KV_HEADS, HEAD_DIM)
    return (r[:, :, 0], r[:, :, 1], _rope(r[:, :, 2], pos), r[:, :, 3], _rope(r[:, :, 4], pos), r[:, :, 5])


def _compress(rows, pe, w1, w2):
    B, T = rows.shape[:2]
    n_chunk = T // CMP_STRIDE
    nc = n_chunk - CMP_R + 1
    ch = rows[:, :n_chunk * CMP_STRIDE].reshape(B, n_chunk, CMP_STRIDE, KV_HEADS, HEAD_DIM)
    z = jnp.einsum('bnskd,rsdh->bnrkh', ch, w1.reshape(CMP_R, CMP_STRIDE, HEAD_DIM, CMP_HID))
    pre = pe.reshape(-1) @ w1
    for r in range(CMP_R):
        pre = pre + z[:, r:r + nc, r]
    return jax.nn.gelu(pre) @ w2


def _cmp_core(q, kc, vc, qpos):
    nc = kc.shape[1]
    end = jnp.arange(nc) * CMP_STRIDE + CMP_LEN - 1
    mask = end[None, :] <= qpos[:, None]
    s = jnp.einsum('bqgpd,bngd->bgpqn', q, kc).astype(jnp.float32) * ATTN_SCALE
    p = jax.nn.softmax(jnp.where(mask, s, NEG), -1)
    p = jnp.where(mask, p, 0.0)
    o = jnp.einsum('bgpqn,bngd->bqgpd', p.astype(vc.dtype), vc)
    return o, p


def _overlap_matrix(nc, ns):
    cs = np.arange(nc) * CMP_STRIDE
    ss = np.arange(ns) * SEL_BLOCK
    ov = np.minimum(cs[:, None] + CMP_LEN, ss[None, :] + SEL_BLOCK) - np.maximum(cs[:, None], ss[None, :])
    return jnp.asarray(np.clip(ov, 0, None).astype(np.float32) / np.float32(CMP_LEN))


def _select(p, qpos, n_sel):
    imp = jnp.einsum('bgpqn,nj->bgqj', p, _overlap_matrix(p.shape[-1], n_sel))
    blk = jnp.arange(n_sel)[None, :]
    qblk = (qpos // SEL_BLOCK)[:, None]
    valid = blk * SEL_BLOCK <= qpos[:, None]
    forced = (blk == 0) | (blk == qblk) | (blk == qblk - 1)
    score = jnp.where(forced & valid, FORCE_SCORE, jnp.where(valid, imp, -1.0))
    vals, idx = lax.top_k(score, min(SEL_TOP, n_sel))
    return idx, vals >= 0.0


def _sel_core(q, qpos, kg, vg, idx, ok):
    kpos = idx[..., None] * SEL_BLOCK + jnp.arange(SEL_BLOCK)
    mask = ok[..., None] & (kpos <= qpos[None, None, :, None, None])
    s = jnp.einsum('bqgpd,bgqnsd->bgqpns', q, kg).astype(jnp.float32) * ATTN_SCALE
    s = jnp.where(mask[:, :, :, None], s, NEG)
    B, G, Tq, P, K, SB = s.shape
    pr = jax.nn.softmax(s.reshape(B, G, Tq, P, K * SB), -1).reshape(s.shape).astype(vg.dtype)
    return jnp.einsum('bgqpns,bgqnsd->bqgpd', pr, vg)


def _win_core(q, k, v, mask):
    s = jnp.einsum('bcqgpd,bckgd->bcgpqk', q, k).astype(jnp.float32) * ATTN_SCALE
    pr = jax.nn.softmax(jnp.where(mask[None, :, None, None], s, NEG), -1).astype(v.dtype)
    return jnp.einsum('bcgpqk,bckgd->bcqgpd', pr, v)


def _sel_prompt(qr, qpos, idx, ok, k_sel, v_sel):
    B, S = qr.shape[:2]
    n_sel = S // SEL_BLOCK
    nq = S // Q_BLOCK
    K = idx.shape[-1]
    kb = k_sel.reshape(B, n_sel, SEL_BLOCK, KV_HEADS, HEAD_DIM)
    vb = v_sel.reshape(B, n_sel, SEL_BLOCK, KV_HEADS, HEAD_DIM)
    bi = jnp.arange(B)[:, None, None, None]
    gi = jnp.arange(KV_HEADS)[None, :, None, None]
    qs = jnp.moveaxis(qr.reshape(B, nq, Q_BLOCK, KV_HEADS, Q_PER_KV, HEAD_DIM), 1, 0)
    idxs = jnp.moveaxis(idx.reshape(B, KV_HEADS, nq, Q_BLOCK, K), 2, 0)
    oks = jnp.moveaxis(ok.reshape(B, KV_HEADS, nq, Q_BLOCK, K), 2, 0)
    poss = qpos.reshape(nq, Q_BLOCK)

    def step(a):
        q_c, i_c, ok_c, p_c = a
        return _sel_core(q_c, p_c, kb[bi, i_c, :, gi], vb[bi, i_c, :, gi], i_c, ok_c)

    o = lax.map(step, (qs, idxs, oks, poss))
    return jnp.moveaxis(o, 0, 1).reshape(B, S, KV_HEADS, Q_PER_KV, HEAD_DIM)


def _sel_sample(qr, qpos, idx, ok, pool_k, pool_v, page_table, new_k, new_v):
    DB, T = new_k.shape[:2]
    ppb = PAGE_SIZE // SEL_BLOCK
    n_pool = pool_k.shape[0]
    n_past = PAST_LEN // SEL_BLOCK
    n_new = -(-T // SEL_BLOCK)
    pad = n_new * SEL_BLOCK - T
    bi = jnp.arange(DB)[:, None, None, None]
    gi = jnp.arange(KV_HEADS)[None, :, None, None]
    is_past = idx < n_past
    ip = jnp.minimum(idx, n_past - 1)
    phys = page_table[bi, ip // ppb] * ppb + ip % ppb
    inew = jnp.clip(idx - n_past, 0, n_new - 1)

    def gather(pool, new):
        pb = pool.reshape(n_pool * ppb, SEL_BLOCK, KV_HEADS, HEAD_DIM)
        nb = jnp.pad(new, ((0, 0), (0, pad), (0, 0), (0, 0))).reshape(DB, n_new, SEL_BLOCK, KV_HEADS, HEAD_DIM)
        return jnp.where(is_past[..., None, None], pb[phys, :, gi], nb[bi, inew, :, gi])

    return _sel_core(qr, qpos, gather(pool_k, new_k), gather(pool_v, new_v), idx, ok)


def _win_prompt(qr, k_win, v_win):
    B, S = qr.shape[:2]
    nq = S // Q_BLOCK
    span = Q_BLOCK + WINDOW
    ridx = jnp.arange(nq)[:, None] * Q_BLOCK + jnp.arange(span)[None, :]
    padw = ((0, 0), (WINDOW, 0), (0, 0), (0, 0))
    kb = jnp.pad(k_win, padw)[:, ridx]
    vb = jnp.pad(v_win, padw)[:, ridx]
    kpos = (ridx - WINDOW)[:, None, :]
    qpos = (jnp.arange(nq)[:, None] * Q_BLOCK + jnp.arange(Q_BLOCK)[None, :])[:, :, None]
    mask = (kpos >= 0) & (kpos <= qpos) & (kpos > qpos - WINDOW)
    q = qr.reshape(B, nq, Q_BLOCK, KV_HEADS, Q_PER_KV, HEAD_DIM)
    return _win_core(q, kb, vb, mask).reshape(B, S, KV_HEADS, Q_PER_KV, HEAD_DIM)


def _win_sample(qr, qpos, k_all, v_all, kpos):
    mask = (kpos[None, :] <= qpos[:, None]) & (kpos[None, :] > qpos[:, None] - WINDOW)
    return _win_core(qr[:, None], k_all[:, None], v_all[:, None], mask[None])[:, 0]


def _nsa(h, qpos, kv, w_in, w_out):
    B, T, _ = h.shape
    proj = h @ w_in
    q = proj[..., :N_HEADS * HEAD_DIM].reshape(B, T, N_HEADS, HEAD_DIM)
    gates = jax.nn.sigmoid(proj[..., N_HEADS * HEAD_DIM:].astype(jnp.float32))
    gates = gates.reshape(B, T, N_BRANCH, KV_HEADS, Q_PER_KV, 1).astype(h.dtype)
    qc = q.reshape(B, T, KV_HEADS, Q_PER_KV, HEAD_DIM)
    qr = _rope(q, qpos).reshape(B, T, KV_HEADS, Q_PER_KV, HEAD_DIM)
    o_cmp, p_cmp = _cmp_core(qc, kv['kc'], kv['vc'], qpos)
    idx, ok = _select(p_cmp, qpos, kv['n_sel'])
    o_sel = kv['sel'](qr, qpos, idx, ok)
    o_win = kv['win'](qr, qpos)
    o = gates[:, :, 0] * o_cmp + gates[:, :, 1] * o_sel + gates[:, :, 2] * o_win
    return o.reshape(B, T, N_HEADS * HEAD_DIM) @ w_out


def _trunk(x, qpos, prm, build_kv):
    gm_v = []
    kv = None
    for l in range(DEPTH):
        h = _rmsnorm(x, prm['norm_mix'][l])
        if l < N_A_LAYERS:
            o, v = _chunk_gmlp(h, prm['gm_w_in'][l], prm['gm_ln_g'][l], prm['gm_ln_b'][l],
                               prm['gm_w_s'][l], prm['gm_b_s'][l], prm['gm_w_out'][l])
            gm_v.append(v)
        else:
            b = l - N_A_LAYERS
            o = _nsa(h, qpos, kv, prm['nsa_w_in'][b], prm['nsa_w_out'][b])
        x = x + o
        h = _rmsnorm(x, prm['norm_ffn'][l])
        if l % 2 == 0:
            x = x + _swiglu(h, prm['ff_w_gate'][l // 2], prm['ff_w_up'][l // 2], prm['ff_w_down'][l // 2])
        else:
            x = x + _moe(h, prm['moe_router'][l // 2], prm['moe_w_gate'][l // 2],
                         prm['moe_w_up'][l // 2], prm['moe_w_down'][l // 2])
        if l == N_A_LAYERS - 1:
            kv = build_kv(_rmsnorm(x, prm['norm_kv']))
    return _rmsnorm(x, prm['norm_final']), gm_v, kv


def setup_inputs(seed: int = 0) -> dict:
    key = jax.random.key(seed)
    ks = iter(jax.random.split(key, 48))

    def nrm(shape, scale):
        return jax.random.normal(next(ks), shape, jnp.float32) * scale

    n_pages = PAST_LEN // PAGE_SIZE
    n_used = DEC_BATCH * n_pages
    n_pool = n_used + max(1, n_used // 4)
    wk = min(WINDOW, PAST_LEN)
    perm = jax.random.permutation(next(ks), n_pool)
    page_table = perm[:n_used].reshape(DEC_BATCH, n_pages).astype(jnp.int32)
    cshape = (n_pool, PAGE_SIZE, KV_HEADS, HEAD_DIM)
    wshape = (DEC_BATCH, wk, KV_HEADS, HEAD_DIM)
    dm = D_MODEL ** -0.5
    return {
        'x_prompt': nrm((BATCH, SEQ, D_MODEL), 1.0),
        'x_sample': nrm((DEC_BATCH, DEC_SEQ, D_MODEL), 1.0),
        'cache_cmp_k': nrm(cshape, 1.0),
        'cache_cmp_v': nrm(cshape, 1.0),
        'cache_sel_k': nrm(cshape, 1.0),
        'cache_sel_v': nrm(cshape, 1.0),
        'state_win_k': nrm(wshape, 1.0),
        'state_win_v': nrm(wshape, 1.0),
        'page_table': page_table,
        'norm_mix': 1.0 + nrm((DEPTH, D_MODEL), 0.02),
        'norm_ffn': 1.0 + nrm((DEPTH, D_MODEL), 0.02),
        'norm_kv': 1.0 + nrm((D_MODEL,), 0.02),
        'norm_final': 1.0 + nrm((D_MODEL,), 0.02),
        'gm_w_in': nrm((N_A_LAYERS, D_MODEL, 2 * GM_DIM), dm),
        'gm_ln_g': 1.0 + nrm((N_A_LAYERS, GM_DIM), 0.02),
        'gm_ln_b': nrm((N_A_LAYERS, GM_DIM), 0.02),
        'gm_w_s': nrm((N_A_LAYERS, GM_GROUPS, GM_CHUNK, GM_CHUNK), 0.5 * GM_CHUNK ** -0.5),
        'gm_b_s': 1.0 + nrm((N_A_LAYERS, GM_GROUPS, GM_CHUNK), 0.1),
        'gm_w_out': nrm((N_A_LAYERS, GM_DIM, D_MODEL), GM_DIM ** -0.5),
        'nsa_w_in': nrm((N_B_LAYERS, D_MODEL, N_HEADS * HEAD_DIM + N_BRANCH * N_HEADS), dm),
        'nsa_w_out': nrm((N_B_LAYERS, N_HEADS * HEAD_DIM, D_MODEL), (N_HEADS * HEAD_DIM) ** -0.5),
        'kv_w': nrm((D_MODEL, N_KV_COLS), dm),
        'cmp_pe_k': nrm((CMP_LEN, HEAD_DIM), 0.1),
        'cmp_w1_k': nrm((CMP_LEN * HEAD_DIM, CMP_HID), (CMP_LEN * HEAD_DIM) ** -0.5),
        'cmp_w2_k': nrm((CMP_HID, HEAD_DIM), CMP_HID ** -0.5),
        'cmp_pe_v': nrm((CMP_LEN, HEAD_DIM), 0.1),
        'cmp_w1_v': nrm((CMP_LEN * HEAD_DIM, CMP_HID), (CMP_LEN * HEAD_DIM) ** -0.5),
        'cmp_w2_v': nrm((CMP_HID, HEAD_DIM), CMP_HID ** -0.5),
        'ff_w_gate': nrm((N_DENSE, D_MODEL, FF_DENSE), dm),
        'ff_w_up': nrm((N_DENSE, D_MODEL, FF_DENSE), dm),
        'ff_w_down': nrm((N_DENSE, FF_DENSE, D_MODEL), FF_DENSE ** -0.5),
        'moe_router': nrm((N_MOE, D_MODEL, N_EXPERTS), dm),
        'moe_w_gate': nrm((N_MOE, N_EXPERTS, D_MODEL, FF_EXPERT), dm),
        'moe_w_up': nrm((N_MOE, N_EXPERTS, D_MODEL, FF_EXPERT), dm),
        'moe_w_down': nrm((N_MOE, N_EXPERTS, FF_EXPERT, D_MODEL), FF_EXPERT ** -0.5),
    }


def reference(x_prompt, x_sample, cache_cmp_k, cache_cmp_v, cache_sel_k, cache_sel_v, state_win_k, state_win_v,
              page_table, norm_mix, norm_ffn, norm_kv, norm_final, gm_w_in, gm_ln_g, gm_ln_b, gm_w_s, gm_b_s,
              gm_w_out, nsa_w_in, nsa_w_out, kv_w, cmp_pe_k, cmp_w1_k, cmp_w2_k, cmp_pe_v, cmp_w1_v, cmp_w2_v,
              ff_w_gate, ff_w_up, ff_w_down, moe_router, moe_w_gate, moe_w_up, moe_w_down):
    prm = dict(norm_mix=norm_mix, norm_ffn=norm_ffn, norm_kv=norm_kv, norm_final=norm_final,
               gm_w_in=gm_w_in, gm_ln_g=gm_ln_g, gm_ln_b=gm_ln_b, gm_w_s=gm_w_s, gm_b_s=gm_b_s,
               gm_w_out=gm_w_out, nsa_w_in=nsa_w_in, nsa_w_out=nsa_w_out, ff_w_gate=ff_w_gate,
               ff_w_up=ff_w_up, ff_w_down=ff_w_down, moe_router=moe_router, moe_w_gate=moe_w_gate,
               moe_w_up=moe_w_up, moe_w_down=moe_w_down)
    B, S, _ = x_prompt.shape
    DB, DS, _ = x_sample.shape
    ppos = jnp.arange(S)
    spos = PAST_LEN + jnp.arange(DS)

    def kv_prompt(hkv):
        kcr, vcr, ks, vs, kw, vw = _kv_rows(hkv, ppos, kv_w)
        wk = min(WINDOW, S)

        def pg(r):
            return r.reshape(B, S // PAGE_SIZE, PAGE_SIZE, KV_HEADS, HEAD_DIM)

        return dict(kc=_compress(kcr, cmp_pe_k, cmp_w1_k, cmp_w2_k),
                    vc=_compress(vcr, cmp_pe_v, cmp_w1_v, cmp_w2_v),
                    n_sel=S // SEL_BLOCK,
                    sel=lambda qr, qp, i, o: _sel_prompt(qr, qp, i, o, ks, vs),
                    win=lambda qr, qp: _win_prompt(qr, kw, vw),
                    state=(pg(kcr), pg(vcr), pg(ks), pg(vs), kw[:, S - wk:], vw[:, S - wk:]))

    def kv_sample(hkv):
        kcr, vcr, ks, vs, kw, vw = _kv_rows(hkv, spos, kv_w)

        def past(c):
            return c[page_table].reshape(DB, PAST_LEN, KV_HEADS, HEAD_DIM)

        kc = _compress(jnp.concatenate([past(cache_cmp_k), kcr], 1), cmp_pe_k, cmp_w1_k, cmp_w2_k)
        vc = _compress(jnp.concatenate([past(cache_cmp_v), vcr], 1), cmp_pe_v, cmp_w1_v, cmp_w2_v)
        wk = state_win_k.shape[1]
        k_all = jnp.concatenate([state_win_k, kw], 1)
        v_all = jnp.concatenate([state_win_v, vw], 1)
        kpos = PAST_LEN - wk + jnp.arange(wk + DS)
        return dict(kc=kc, vc=vc,
                    n_sel=-(-(PAST_LEN + DS) // SEL_BLOCK),
                    sel=lambda qr, qp, i, o: _sel_sample(qr, qp, i, o, cache_sel_k, cache_sel_v, page_table, ks, vs),
                    win=lambda qr, qp: _win_sample(qr, qp, k_all, v_all, kpos),
                    state=(kcr, vcr, ks, vs, k_all[:, DS:], v_all[:, DS:]))

    y_prompt, _, kvp = _trunk(x_prompt, ppos, prm, kv_prompt)
    y_sample, gm_s, kvs = _trunk(x_sample, spos, prm, kv_sample)
    p_cmp_k, p_cmp_v, p_sel_k, p_sel_v, p_win_k, p_win_v = kvp['state']
    s_cmp_k, s_cmp_v, s_sel_k, s_sel_v, s_win_k, s_win_v = kvs['state']
    s_gm_v = jnp.stack(gm_s, 0)
    return (y_prompt, y_sample, p_cmp_k, p_cmp_v, p_sel_k, p_sel_v, p_win_k, p_win_v,
            s_cmp_k, s_cmp_v, s_sel_k, s_sel_v, s_win_k, s_win_v, s_gm_v)
```

```python
import functools

import numpy as np
import jax
import jax.numpy as jnp
from jax import lax
from jax.experimental import pallas as pl
from jax.experimental.pallas import tpu as pltpu

F32 = jnp.float32
BF16 = jnp.bfloat16
I32 = jnp.int32

D_MODEL = 2048
DEPTH = 4
PAGE_SIZE = 128
N_A_LAYERS = DEPTH // 2
GM_CHUNK = 128
GM_DIM = D_MODEL
GM_GROUPS = 16
GM_GW = GM_DIM // GM_GROUPS
N_HEADS = 16
HEAD_DIM = 128
KV_HEADS = 4
Q_PER_KV = N_HEADS // KV_HEADS
ROT_DIM = HEAD_DIM // 4
ROPE_THETA = 500000.0
CMP_LEN = 32
CMP_STRIDE = 16
CMP_R = CMP_LEN // CMP_STRIDE
SEL_BLOCK = 64
SEL_TOP = 16
WINDOW = 512
N_BRANCH = 3
KV_SEC = KV_HEADS * HEAD_DIM
ATTN_SCALE = HEAD_DIM ** -0.5
N_EXPERTS = 8
TOP_K = 2
EPS = 1e-6
NEG = -1e30
FORCE_SCORE = 1e6

LANES = 128
VMEM_LIMIT_BYTES = 56 * 1024 * 1024
ROW_TILE = 832
ROW_TILE_DOWN = 416
COL_TILE = 512
MOE_BLOCK = 256
KEY_TILE = 256
Q_TILE = 128
PAGES_PER_STEP = 8


def _cparams(*sem):
    return pltpu.CompilerParams(dimension_semantics=sem, vmem_limit_bytes=VMEM_LIMIT_BYTES)


def _dot(a, b):
    return jnp.dot(a, b, preferred_element_type=F32)


def _dot_nt(a, b):
    return lax.dot_general(a, b, (((1,), (1,)), ((), ())), preferred_element_type=F32)


def _dot3(a, b):
    ah = a.astype(BF16)
    al = (a - ah.astype(F32)).astype(BF16)
    bh = b.astype(BF16)
    bl = (b - bh.astype(F32)).astype(BF16)
    return _dot(ah, bh) + _dot(al, bh) + _dot(ah, bl)


def _ws_body(*refs, nw, n_extra, n_out, epilogue):
    x_ref = refs[0]
    w_refs = refs[1:1 + nw]
    e_refs = refs[1 + nw:1 + nw + n_extra]
    o_refs = refs[1 + nw + n_extra:1 + nw + n_extra + n_out]
    wb_refs = refs[1 + nw + n_extra + n_out:]
    j = pl.program_id(0)

    @pl.when(pl.program_id(1) == 0)
    def _():
        for w_ref, wb in zip(w_refs, wb_refs):
            wb[...] = w_ref[...].astype(BF16)

    accs = [_dot(x_ref[...], wb[...]) for wb in wb_refs]
    outs = epilogue(j, accs, e_refs)
    for o_ref, o in zip(o_refs, outs):
        o_ref[...] = o.astype(o_ref.dtype)


def _ws_matmul(x, ws, epilogue, outs, *, tm, tn, n_cols, extras=(), name):
    M, K = x.shape
    nj = n_cols // tn
    ni = M // tm
    assert nj * tn == n_cols and ni * tm == M
    in_specs = [pl.BlockSpec((tm, K), lambda j, i: (i, 0))]
    in_specs += [pl.BlockSpec((K, tn), lambda j, i: (0, j)) for _ in ws]
    in_specs += [pl.BlockSpec(bs, im) for _, bs, im in extras]
    out_shape = [jax.ShapeDtypeStruct((M, c), dt) for c, dt in outs]
    out_specs = [pl.BlockSpec((tm, c // nj), lambda j, i: (i, j)) for c, _ in outs]
    body = functools.partial(_ws_body, nw=len(ws), n_extra=len(extras), n_out=len(outs), epilogue=epilogue)
    res = pl.pallas_call(
        body, grid=(nj, ni), in_specs=in_specs, out_specs=out_specs, out_shape=out_shape,
        scratch_shapes=[pltpu.VMEM((K, tn), BF16) for _ in ws],
        compiler_params=_cparams("arbitrary", "arbitrary"), name=name,
    )(x, *ws, *[a for a, _, _ in extras])
    return res


def _rope_tile(acc, c_ref, s1_ref, s2_ref):
    c, s1, s2 = c_ref[...], s1_ref[...], s2_ref[...]
    parts = []
    for h in range(acc.shape[1] // HEAD_DIM):
        xh = acc[:, h * HEAD_DIM:(h + 1) * HEAD_DIM]
        parts.append(xh * c + pltpu.roll(xh, HEAD_DIM - ROT_DIM // 2, 1) * s1 + pltpu.roll(xh, ROT_DIM // 2, 1) * s2)
    return jnp.concatenate(parts, 1)


def _ep_gelu(j, accs, e):
    return [jax.nn.gelu(accs[0])]


def _ep_residual(j, accs, e):
    return [e[0][...] + accs[0]]


def _ep_swiglu(j, accs, e):
    return [jax.nn.silu(accs[0]) * accs[1]]


def _ep_sigmoid(j, accs, e):
    return [jax.nn.sigmoid(accs[0])]


def _ep_kv(j, accs, e):
    acc = accs[0]
    roped = _rope_tile(acc, *e)
    r = jnp.where((j == 2) | (j == 4), roped, acc)
    return [r, r]


def _ep_q(j, accs, e):
    acc = accs[0]
    return [acc, _rope_tile(acc, *e)]


def _rms(x, g):
    return (x * lax.rsqrt(jnp.mean(x * x, -1, keepdims=True) + EPS)) * g


def _rms_body(x_ref, g_ref, o_ref):
    o_ref[...] = _rms(x_ref[...], g_ref[...]).astype(o_ref.dtype)


def _rmsnorm(x, g, out_dtype, *, tm):
    M, D = x.shape
    return pl.pallas_call(
        _rms_body, grid=(M // tm,),
        in_specs=[pl.BlockSpec((tm, D), lambda i: (i, 0)), pl.BlockSpec((1, D), lambda i: (0, 0))],
        out_specs=pl.BlockSpec((tm, D), lambda i: (i, 0)),
        out_shape=jax.ShapeDtypeStruct((M, D), out_dtype),
        compiler_params=_cparams("arbitrary"), name="rmsnorm",
    )(x, g.reshape(1, D))


def _rms_router_body(x_ref, g_ref, r_ref, h_ref, te_ref, tg_ref):
    h = _rms(x_ref[...], g_ref[...])
    h_ref[...] = h
    lane = lax.broadcasted_iota(I32, (1, LANES), 1).astype(F32)
    logits = jnp.where(lane < N_EXPERTS, _dot3(h, r_ref[...]), -jnp.inf)
    m1 = jnp.max(logits, -1, keepdims=True)
    i1 = jnp.min(jnp.where(logits == m1, lane, float(LANES)), -1, keepdims=True)
    rest = jnp.where(lane == i1, -jnp.inf, logits)
    m2 = jnp.max(rest, -1, keepdims=True)
    i2 = jnp.min(jnp.where(rest == m2, lane, float(LANES)), -1, keepdims=True)
    e2 = jnp.exp(m2 - m1)
    den = 1.0 + e2
    te_ref[...] = jnp.where(lane == 0, i1, jnp.where(lane == 1, i2, 0.0)).astype(I32)
    tg_ref[...] = jnp.where(lane == 0, 1.0 / den, jnp.where(lane == 1, e2 / den, 0.0))


def _rmsnorm_router(x, g, router, *, tm):
    M, D = x.shape
    rp = jnp.pad(router, ((0, 0), (0, LANES - N_EXPERTS)))
    return pl.pallas_call(
        _rms_router_body, grid=(M // tm,),
        in_specs=[pl.BlockSpec((tm, D), lambda i: (i, 0)), pl.BlockSpec((1, D), lambda i: (0, 0)),
                  pl.BlockSpec((D, LANES), lambda i: (0, 0))],
        out_specs=[pl.BlockSpec((tm, D), lambda i: (i, 0)), pl.BlockSpec((tm, LANES), lambda i: (i, 0)),
                   pl.BlockSpec((tm, LANES), lambda i: (i, 0))],
        out_shape=[jax.ShapeDtypeStruct((M, D), F32), jax.ShapeDtypeStruct((M, LANES), I32),
                   jax.ShapeDtypeStruct((M, LANES), F32)],
        compiler_params=_cparams("arbitrary"), name="rmsnorm_router",
    )(x, g.reshape(1, D), rp)


def _gm_gate_body(z_ref, lng_ref, lnb_ref, ws_ref, bst_ref, w00_ref, b00_ref, o_ref, vs_ref, *,
                  n_prompt_chunks, n_sample):
    c = pl.program_id(0)
    z = z_ref[...]
    u = z[:, :GM_DIM].astype(F32)
    v = z[:, GM_DIM:].astype(F32)
    mu = jnp.mean(v, -1, keepdims=True)
    vc = v - mu
    vn = (vc * lax.rsqrt(jnp.mean(vc * vc, -1, keepdims=True) + EPS)) * lng_ref[...] + lnb_ref[...]

    @pl.when(c < n_prompt_chunks)
    def _():
        vb = vn.astype(BF16)
        tri = lax.broadcasted_iota(I32, (GM_CHUNK, GM_CHUNK), 0) >= lax.broadcasted_iota(I32, (GM_CHUNK, GM_CHUNK), 1)
        for g in range(GM_GROUPS):
            sl = slice(g * GM_GW, (g + 1) * GM_GW)
            wg = jnp.where(tri, ws_ref[g], 0.0).astype(BF16)
            s = _dot(wg, vb[:, sl]) + bst_ref[:, g:g + 1]
            o_ref[:, sl] = (u[:, sl] * s).astype(o_ref.dtype)

    @pl.when(c >= n_prompt_chunks)
    def _():
        row = lax.broadcasted_iota(I32, (GM_CHUNK, 1), 0)
        s = w00_ref[...] * vn + b00_ref[...]
        o_ref[...] = jnp.where(row < n_sample, u * s, 0.0).astype(o_ref.dtype)
        vs_ref[...] = vn[:vs_ref.shape[0]]


def _gm_gate(z, ln_g, ln_b, w_s, b_s, *, n_prompt, n_sample):
    M = z.shape[0]
    n_pc = n_prompt // GM_CHUNK
    assert n_prompt % GM_CHUNK == 0 and M == (n_pc + 1) * GM_CHUNK and n_sample <= GM_CHUNK
    n_vs = -(-n_sample // 8) * 8
    w00 = jnp.repeat(w_s[:, 0, 0], GM_GW).reshape(1, GM_DIM)
    b00 = jnp.repeat(b_s[:, 0], GM_GW).reshape(1, GM_DIM)
    body = functools.partial(_gm_gate_body, n_prompt_chunks=n_pc, n_sample=n_sample)
    full = lambda *shape: pl.BlockSpec(shape, lambda c: (0,) * len(shape))
    return pl.pallas_call(
        body, grid=(n_pc + 1,),
        in_specs=[pl.BlockSpec((GM_CHUNK, 2 * GM_DIM), lambda c: (c, 0)), full(1, GM_DIM), full(1, GM_DIM),
                  full(GM_GROUPS, GM_CHUNK, GM_CHUNK), full(GM_CHUNK, GM_GROUPS), full(1, GM_DIM), full(1, GM_DIM)],
        out_specs=[pl.BlockSpec((GM_CHUNK, GM_DIM), lambda c: (c, 0)), full(n_vs, GM_DIM)],
        out_shape=[jax.ShapeDtypeStruct((M, GM_DIM), BF16), jax.ShapeDtypeStruct((n_vs, GM_DIM), F32)],
        compiler_params=_cparams("arbitrary"), name="gmlp_gate",
    )(z, ln_g.reshape(1, GM_DIM), ln_b.reshape(1, GM_DIM), w_s, b_s.T, w00, b00)


def _moe_gather_body(rt_ref, nu_ref, h_hbm, o_ref, buf, sem):
    i = pl.program_id(0)

    def row_copy(tok, r):
        return pltpu.make_async_copy(h_hbm.at[pl.ds(tok, 1), :], buf.at[pl.ds(r, 1), :], sem)

    @pl.when(i < nu_ref[0])
    def _():
        def issue(r, c):
            row_copy(rt_ref[i * MOE_BLOCK + r], r).start()
            return c

        def wait(r, c):
            row_copy(0, r).wait()
            return c

        lax.fori_loop(0, MOE_BLOCK, issue, 0, unroll=8)
        lax.fori_loop(0, MOE_BLOCK, wait, 0, unroll=8)
        o_ref[...] = buf[...].astype(o_ref.dtype)

    @pl.when(i >= nu_ref[0])
    def _():
        o_ref[...] = jnp.zeros(o_ref.shape, o_ref.dtype)


def _moe_gather(h, row_tok, n_used, n_blk):
    D = h.shape[1]
    return pl.pallas_call(
        _moe_gather_body,
        grid_spec=pltpu.PrefetchScalarGridSpec(
            num_scalar_prefetch=2, grid=(n_blk,),
            in_specs=[pl.BlockSpec(memory_space=pl.ANY)],
            out_specs=pl.BlockSpec((MOE_BLOCK, D), lambda i, rt, nu: (i, 0)),
            scratch_shapes=[pltpu.VMEM((MOE_BLOCK, D), F32), pltpu.SemaphoreType.DMA(())]),
        out_shape=jax.ShapeDtypeStruct((n_blk * MOE_BLOCK, D), BF16),
        compiler_params=_cparams("arbitrary"), name="moe_gather",
    )(row_tok, n_used, h)


def _moe_mm_body(be_ref, bf_ref, nu_ref, *refs, nw, epilogue):
    x_ref = refs[0]
    w_refs = refs[1:1 + nw]
    o_ref = refs[1 + nw]
    wb_refs = refs[2 + nw:]
    i = pl.program_id(1)
    live = i < nu_ref[0]

    @pl.when(live & (bf_ref[i] == 1))
    def _():
        for w_ref, wb in zip(w_refs, wb_refs):
            wb[...] = w_ref[...].astype(BF16)

    @pl.when(live)
    def _():
        accs = [_dot(x_ref[...], wb[...]) for wb in wb_refs]
        o_ref[...] = epilogue(accs).astype(o_ref.dtype)

    @pl.when(jnp.logical_not(live))
    def _():
        o_ref[...] = jnp.zeros(o_ref.shape, o_ref.dtype)


def _moe_matmul(x, ws, blk_e, blk_first, n_used, epilogue, out_dtype, *, tn, name):
    M, K = x.shape
    N = ws[0].shape[2]
    nj, ni = N // tn, M // MOE_BLOCK
    assert nj * tn == N
    clamp = lambda i, nu: jnp.minimum(i, nu[0] - 1)
    in_specs = [pl.BlockSpec((MOE_BLOCK, K), lambda j, i, be, bf, nu: (clamp(i, nu), 0))]
    in_specs += [pl.BlockSpec((None, K, tn), lambda j, i, be, bf, nu: (be[clamp(i, nu)], 0, j)) for _ in ws]
    body = functools.partial(_moe_mm_body, nw=len(ws), epilogue=epilogue)
    return pl.pallas_call(
        body,
        grid_spec=pltpu.PrefetchScalarGridSpec(
            num_scalar_prefetch=3, grid=(nj, ni), in_specs=in_specs,
            out_specs=pl.BlockSpec((MOE_BLOCK, tn), lambda j, i, be, bf, nu: (i, j)),
            scratch_shapes=[pltpu.VMEM((K, tn), BF16) for _ in ws]),
        out_shape=jax.ShapeDtypeStruct((M, N), out_dtype),
        compiler_params=_cparams("arbitrary", "arbitrary"), name=name,
    )(blk_e, blk_first, n_used, x, *ws)


def _moe_combine_body(d0_ref, d1_ref, x_ref, g_ref, yb_hbm, o_ref, b0, b1, sem, *, tm):
    i = pl.program_id(0)

    def row_copy(src_row, buf, r):
        return pltpu.make_async_copy(yb_hbm.at[pl.ds(src_row, 1), :], buf.at[pl.ds(r, 1), :], sem)

    def issue(r, c):
        row_copy(d0_ref[i * tm + r], b0, r).start()
        row_copy(d1_ref[i * tm + r], b1, r).start()
        return c

    def wait(r, c):
        row_copy(0, b0, r).wait()
        row_copy(0, b1, r).wait()
        return c

    lax.fori_loop(0, tm, issue, 0, unroll=8)
    lax.fori_loop(0, tm, wait, 0, unroll=8)
    g = g_ref[...]
    o_ref[...] = x_ref[...] + (g[:, 0:1] * b0[...] + g[:, 1:2] * b1[...])


def _moe_combine(x, gates, yb, d0, d1, *, tm):
    M, D = x.shape
    body = functools.partial(_moe_combine_body, tm=tm)
    return pl.pallas_call(
        body,
        grid_spec=pltpu.PrefetchScalarGridSpec(
            num_scalar_prefetch=2, grid=(M // tm,),
            in_specs=[pl.BlockSpec((tm, D), lambda i, a, b: (i, 0)), pl.BlockSpec((tm, LANES), lambda i, a, b: (i, 0)),
                      pl.BlockSpec(memory_space=pl.ANY)],
            out_specs=pl.BlockSpec((tm, D), lambda i, a, b: (i, 0)),
            scratch_shapes=[pltpu.VMEM((tm, D), F32), pltpu.VMEM((tm, D), F32), pltpu.SemaphoreType.DMA(())]),
        out_shape=jax.ShapeDtypeStruct((M, D), F32),
        compiler_params=_cparams("arbitrary"), name="moe_combine",
    )(d0, d1, x, gates, yb)


def _moe(x, g_norm, router, wg, wu, wd, *, n_tok):
    M, D = x.shape
    h, te, tg = _rmsnorm_router(x, g_norm, router, tm=ROW_TILE)
    a = n_tok * TOP_K
    n_blk = -(-a // MOE_BLOCK) + N_EXPERTS
    e_flat = te[:n_tok, :TOP_K].reshape(a)
    onehot = (e_flat[:, None] == jnp.arange(N_EXPERTS, dtype=I32)[None, :]).astype(I32)
    csum = jnp.cumsum(onehot, 0)
    pos = jnp.take_along_axis(csum, e_flat[:, None], 1)[:, 0] - 1
    counts = csum[-1]
    padded = (counts + MOE_BLOCK - 1) // MOE_BLOCK * MOE_BLOCK
    end_pad = jnp.cumsum(padded)
    start_pad = end_pad - padded
    dest = (start_pad[e_flat] + pos).astype(I32)
    tok = jnp.arange(a, dtype=I32) // TOP_K
    row_tok = jnp.full((n_blk * MOE_BLOCK,), n_tok, I32).at[dest].set(tok)
    n_used = (end_pad[-1] // MOE_BLOCK).astype(I32).reshape(1)
    blk_e = jnp.minimum(jnp.searchsorted(end_pad, jnp.arange(n_blk, dtype=I32) * MOE_BLOCK, side='right'),
                        N_EXPERTS - 1).astype(I32)
    blk_first = jnp.concatenate([jnp.ones((1,), I32), (blk_e[1:] != blk_e[:-1]).astype(I32)])
    d = jnp.pad(dest.reshape(n_tok, TOP_K), ((0, M - n_tok), (0, 0)))
    row = jnp.arange(M, dtype=I32)[:, None]
    gates = jnp.where(row < n_tok, tg, 0.0)

    xb = _moe_gather(h, row_tok, n_used, n_blk)
    hb = _moe_matmul(xb, [wg, wu], blk_e, blk_first, n_used, lambda accs: jax.nn.silu(accs[0]) * accs[1], BF16,
                     tn=COL_TILE, name="moe_up")
    yb = _moe_matmul(hb, [wd], blk_e, blk_first, n_used, lambda accs: accs[0], F32, tn=COL_TILE, name="moe_down")
    return _moe_combine(x, gates, yb, d[:, 0], d[:, 1], tm=ROW_TILE_DOWN)


def _cmp_z_body(pt_ref, *refs):
    n_in = KV_HEADS * PAGES_PER_STEP
    page_refs = refs[:n_in]
    w_ref = refs[n_in]
    o_ref = refs[n_in + 1]
    n_ch = PAGE_SIZE // CMP_STRIDE
    acc = jnp.zeros((n_in * n_ch, CMP_R * HEAD_DIM), F32)
    for s in range(CMP_STRIDE):
        rows = [pg[pl.ds(s, n_ch, stride=CMP_STRIDE), :] for pg in page_refs]
        acc = acc + _dot(jnp.concatenate(rows, 0).astype(BF16), w_ref[s])
    m = PAGES_PER_STEP * n_ch
    for k in range(KV_HEADS):
        o_ref[k] = acc[k * m:(k + 1) * m]


def _cmp_z(pages, table, sec, w1cat):
    nb, npg = table.shape
    assert npg % PAGES_PER_STEP == 0
    n_ch = PAGE_SIZE // CMP_STRIDE
    m = PAGES_PER_STEP * n_ch
    page_spec = lambda k, q: pl.BlockSpec((None, PAGE_SIZE, HEAD_DIM),
                                          lambda b, g, pt: (pt[b, g * PAGES_PER_STEP + q], 0, sec * KV_HEADS + k))
    return pl.pallas_call(
        _cmp_z_body,
        grid_spec=pltpu.PrefetchScalarGridSpec(
            num_scalar_prefetch=1, grid=(nb, npg // PAGES_PER_STEP),
            in_specs=[page_spec(k, q) for k in range(KV_HEADS) for q in range(PAGES_PER_STEP)]
            + [pl.BlockSpec((CMP_STRIDE, HEAD_DIM, CMP_R * HEAD_DIM), lambda b, g, pt: (0, 0, 0))],
            out_specs=pl.BlockSpec((None, KV_HEADS, m, CMP_R * HEAD_DIM), lambda b, g, pt: (b, 0, g, 0))),
        out_shape=jax.ShapeDtypeStruct((nb, KV_HEADS, npg * n_ch, CMP_R * HEAD_DIM), F32),
        compiler_params=_cparams("arbitrary", "arbitrary"), name="cmp_z",
    )(table, *([pages] * (KV_HEADS * PAGES_PER_STEP)), w1cat)


def _cmp_fin_body(z_ref, pe_ref, w1_ref, w2_ref, o_ref):
    z = z_ref[...]
    n = z.shape[0]
    pe = jnp.broadcast_to(pe_ref[...], (8, pe_ref.shape[1])).astype(BF16)
    pre = _dot(pe, w1_ref[...].astype(BF16))[0:1]
    pre = pre + z[:, :HEAD_DIM]
    pre = pre + pltpu.roll(z[:, HEAD_DIM:], n - 1, 0)
    o_ref[...] = _dot(jax.nn.gelu(pre).astype(BF16), w2_ref[...].astype(BF16)).astype(o_ref.dtype)


def _cmp_fin(z, pe, w1, w2):
    nb, _, n, _ = z.shape
    return pl.pallas_call(
        _cmp_fin_body, grid=(nb, KV_HEADS),
        in_specs=[pl.BlockSpec((None, None, n, CMP_R * HEAD_DIM), lambda b, k: (b, k, 0, 0)),
                  pl.BlockSpec((1, CMP_LEN * HEAD_DIM), lambda b, k: (0, 0)),
                  pl.BlockSpec((CMP_LEN * HEAD_DIM, HEAD_DIM), lambda b, k: (0, 0)),
                  pl.BlockSpec((HEAD_DIM, HEAD_DIM), lambda b, k: (0, 0))],
        out_specs=pl.BlockSpec((None, None, n, HEAD_DIM), lambda b, k: (b, k, 0, 0)),
        out_shape=jax.ShapeDtypeStruct((nb, KV_HEADS, n, HEAD_DIM), BF16),
        compiler_params=_cparams("arbitrary", "arbitrary"), name="cmp_fin",
    )(z, pe.reshape(1, -1), w1, w2)


def _w1cat(w1):
    w = w1.reshape(CMP_R, CMP_STRIDE, HEAD_DIM, HEAD_DIM)
    return jnp.transpose(w, (1, 2, 0, 3)).reshape(CMP_STRIDE, HEAD_DIM, CMP_R * HEAD_DIM).astype(BF16)


def _compress(pages, table, sec, pe, w1, w2):
    return _cmp_fin(_cmp_z(pages, table, sec, _w1cat(w1)), pe, w1, w2)


def _overlap_matrix(nc, ns, rows, cols):
    cs = np.arange(nc) * CMP_STRIDE
    ss = np.arange(ns) * SEL_BLOCK
    ov = np.minimum(cs[:, None] + CMP_LEN, ss[None, :] + SEL_BLOCK) - np.maximum(cs[:, None], ss[None, :])
    ov = np.clip(ov, 0, None).astype(np.float32) / np.float32(CMP_LEN)
    out = np.zeros((rows, cols), np.float32)
    out[:nc, :ns] = ov
    return jnp.asarray(out)


def _attn_prompt_body(qc_ref, qr_ref, kc_ref, vc_ref, ks_ref, vs_ref, kw_ref, vw_ref, g_ref, ov_ref, ex_ref, o_ref,
                      allow_ref, ocmp_ref, osel_ref, m_ref, l_ref, acc_ref, *, seq, n_cmp, n_sel):
    qi = pl.program_id(2)
    qpos = qi * Q_TILE + lax.broadcasted_iota(I32, (Q_TILE, 1), 0)
    lane = lax.broadcasted_iota(I32, (1, LANES), 1)

    cmask = (lane * CMP_STRIDE + (CMP_LEN - 1) <= qpos) & (lane < n_cmp)
    kc, vc = kc_ref[...], vc_ref[...]
    psum = jnp.zeros((Q_TILE, LANES), F32)
    for p in range(Q_PER_KV):
        s = _dot_nt(qc_ref[:, p * HEAD_DIM:(p + 1) * HEAD_DIM], kc) * ATTN_SCALE
        sm = jnp.where(cmask, s, NEG)
        e = jnp.exp(sm - jnp.max(sm, -1, keepdims=True))
        pr = jnp.where(cmask, e / jnp.sum(e, -1, keepdims=True), 0.0)
        psum = psum + pr
        ocmp_ref[p] = _dot(pr.astype(BF16), vc)

    imp = _dot3(psum, ov_ref[...])
    qblk = qpos >> 6
    valid = (lane * SEL_BLOCK <= qpos) & (lane < n_sel)
    forced = (lane == 0) | (lane == qblk) | (lane == qblk - 1)
    score = jnp.where(forced & valid, FORCE_SCORE, jnp.where(valid, imp, -1.0))
    rank = jnp.zeros((Q_TILE, LANES), F32)
    for j in range(n_sel):
        cj = score[:, j:j + 1]
        beats = (cj > score) | ((cj == score) & (lane > j))
        rank = rank + jnp.where(beats, 1.0, 0.0)
    sel = (rank < SEL_TOP) & (score >= 0.0)
    selx = _dot(jnp.where(sel, 1.0, 0.0).astype(BF16), ex_ref[...])
    kpos_all = lax.broadcasted_iota(I32, (1, seq), 1)
    allow_ref[...] = jnp.where((selx > 0.5) & (kpos_all <= qpos), 1.0, 0.0)

    def flash(k_ref, v_ref, lo, hi, mask_fn):
        m_ref[...] = jnp.full(m_ref.shape, NEG, F32)
        l_ref[...] = jnp.zeros(l_ref.shape, F32)
        acc_ref[...] = jnp.zeros(acc_ref.shape, F32)

        def body(kt, c):
            start = pl.multiple_of(kt * KEY_TILE, KEY_TILE)
            k = k_ref[pl.ds(start, KEY_TILE), :]
            v = v_ref[pl.ds(start, KEY_TILE), :]
            mask = mask_fn(start)
            for p in range(Q_PER_KV):
                s = _dot_nt(qr_ref[:, p * HEAD_DIM:(p + 1) * HEAD_DIM], k) * ATTN_SCALE
                s = jnp.where(mask, s, NEG)
                m_prev = m_ref[p]
                m_new = jnp.maximum(m_prev, jnp.max(s, -1, keepdims=True))
                alpha = jnp.exp(m_prev - m_new)
                e = jnp.exp(s - m_new)
                l_ref[p] = alpha * l_ref[p] + jnp.sum(e, -1, keepdims=True)
                acc_ref[p] = alpha * acc_ref[p] + _dot(e.astype(BF16), v)
                m_ref[p] = m_new
            return c

        lax.fori_loop(lo, hi, body, 0)

    hi = (qi * Q_TILE + Q_TILE - 1) // KEY_TILE + 1
    flash(ks_ref, vs_ref, 0, hi, lambda start: allow_ref[:, pl.ds(start, KEY_TILE)] > 0.5)
    for p in range(Q_PER_KV):
        osel_ref[p] = acc_ref[p] / l_ref[p]

    def win_mask(start):
        kp = start + lax.broadcasted_iota(I32, (1, KEY_TILE), 1)
        return (kp <= qpos) & (kp > qpos - WINDOW)

    flash(kw_ref, vw_ref, jnp.maximum(qi * Q_TILE - (WINDOW - 1), 0) // KEY_TILE, hi, win_mask)

    g = g_ref[...]
    for p in range(Q_PER_KV):
        o = (g[:, p:p + 1] * ocmp_ref[p] + g[:, Q_PER_KV + p:Q_PER_KV + p + 1] * osel_ref[p]
             + g[:, 2 * Q_PER_KV + p:2 * Q_PER_KV + p + 1] * (acc_ref[p] / l_ref[p]))
        o_ref[:, p * HEAD_DIM:(p + 1) * HEAD_DIM] = o.astype(o_ref.dtype)


def _attn_prompt(qc, qr, kc, vc, rb, gates, *, batch, seq):
    nq = seq // Q_TILE
    n_chunk = seq // CMP_STRIDE
    n_cmp = n_chunk - CMP_R + 1
    n_sel = seq // SEL_BLOCK
    assert n_chunk == LANES and n_sel <= LANES and seq % KEY_TILE == 0
    ov = _overlap_matrix(n_cmp, n_sel, LANES, LANES)
    ex = np.zeros((LANES, seq), np.float32)
    ex[np.arange(seq) // SEL_BLOCK, np.arange(seq)] = 1.0
    ex = jnp.asarray(ex, BF16)
    qspec = pl.BlockSpec((Q_TILE, KV_SEC), lambda b, g, q: (b * nq + q, g))
    cspec = pl.BlockSpec((None, None, n_chunk, HEAD_DIM), lambda b, g, q: (b, g, 0, 0))
    kvspec = lambda sec: pl.BlockSpec((seq, HEAD_DIM), lambda b, g, q: (b, sec * KV_HEADS + g))
    body = functools.partial(_attn_prompt_body, seq=seq, n_cmp=n_cmp, n_sel=n_sel)
    return pl.pallas_call(
        body, grid=(batch, KV_HEADS, nq),
        in_specs=[qspec, qspec, cspec, cspec, kvspec(2), kvspec(3), kvspec(4), kvspec(5),
                  pl.BlockSpec((Q_TILE, LANES), lambda b, g, q: (b * nq + q, g)),
                  pl.BlockSpec((LANES, LANES), lambda b, g, q: (0, 0)),
                  pl.BlockSpec((LANES, seq), lambda b, g, q: (0, 0))],
        out_specs=qspec,
        out_shape=jax.ShapeDtypeStruct((batch * seq, N_HEADS * HEAD_DIM), BF16),
        scratch_shapes=[pltpu.VMEM((Q_TILE, seq), F32), pltpu.VMEM((Q_PER_KV, Q_TILE, HEAD_DIM), F32),
                        pltpu.VMEM((Q_PER_KV, Q_TILE, HEAD_DIM), F32), pltpu.VMEM((Q_PER_KV, Q_TILE, 1), F32),
                        pltpu.VMEM((Q_PER_KV, Q_TILE, 1), F32), pltpu.VMEM((Q_PER_KV, Q_TILE, HEAD_DIM), F32)],
        compiler_params=_cparams("arbitrary", "arbitrary", "arbitrary"), name="attn_prompt",
    )(qc, qr, kc, vc, rb, rb, rb, rb, gates, ov, ex)


def _attn_s1_body(q_ref, kc_ref, vc_ref, ov_ref, o_ref, idx_ref, val_ref, *, qpos, n_cmp, n_sel):
    rows = q_ref.shape[0]
    n = kc_ref.shape[0]
    col = lax.broadcasted_iota(I32, (1, n), 1)
    cmask = (col * CMP_STRIDE + (CMP_LEN - 1) <= qpos) & (col < n_cmp)
    s = _dot_nt(q_ref[...], kc_ref[...]) * ATTN_SCALE
    sm = jnp.where(cmask, s, NEG)
    e = jnp.exp(sm - jnp.max(sm, -1, keepdims=True))
    pr = jnp.where(cmask, e / jnp.sum(e, -1, keepdims=True), 0.0)
    o_ref[...] = _dot(pr.astype(BF16), vc_ref[...])
    head = lax.broadcasted_iota(I32, (rows, 1), 0)
    psum = jnp.sum(jnp.where(head < Q_PER_KV, pr, 0.0), 0, keepdims=True)
    imp = _dot3(jnp.broadcast_to(psum, (rows, n)), ov_ref[...])[0:1]
    w = ov_ref.shape[1]
    blk = lax.broadcasted_iota(I32, (1, w), 1)
    qblk = qpos // SEL_BLOCK
    valid = (blk * SEL_BLOCK <= qpos) & (blk < n_sel)
    forced = (blk == 0) | (blk == qblk) | (blk == qblk - 1)
    low = -3e38
    score = jnp.where(blk < n_sel, jnp.where(forced & valid, FORCE_SCORE, jnp.where(valid, imp, -1.0)), low)
    blkf = blk.astype(F32)
    lane = lax.broadcasted_iota(I32, (1, LANES), 1)
    idx_row = jnp.zeros((1, LANES), F32)
    val_row = jnp.full((1, LANES), -1.0, F32)
    for t in range(SEL_TOP):
        m = jnp.max(score, -1, keepdims=True)
        it = jnp.min(jnp.where(score == m, blkf, float(w)), -1, keepdims=True)
        idx_row = jnp.where(lane == t, it, idx_row)
        val_row = jnp.where(lane == t, m, val_row)
        score = jnp.where(blkf == it, low, score)
    idx_ref[...] = jnp.broadcast_to(idx_row, idx_ref.shape).astype(I32)
    val_ref[...] = jnp.broadcast_to(val_row, val_ref.shape)


def _attn_s1(qc, kc, vc, *, qpos):
    nb, _, rows, _ = qc.shape
    n_chunk = kc.shape[2]
    t_all = qpos + 1
    n_cmp = t_all // CMP_STRIDE - CMP_R + 1
    n_sel = -(-t_all // SEL_BLOCK)
    assert n_cmp <= n_chunk and n_sel >= SEL_TOP
    w = -(-n_sel // LANES) * LANES
    ov = _overlap_matrix(n_cmp, n_sel, n_chunk, w)
    body = functools.partial(_attn_s1_body, qpos=qpos, n_cmp=n_cmp, n_sel=n_sel)
    hspec = pl.BlockSpec((None, None, rows, HEAD_DIM), lambda b, g: (b, g, 0, 0))
    cspec = pl.BlockSpec((None, None, n_chunk, HEAD_DIM), lambda b, g: (b, g, 0, 0))
    return pl.pallas_call(
        body, grid=(nb, KV_HEADS),
        in_specs=[hspec, cspec, cspec, pl.BlockSpec((n_chunk, w), lambda b, g: (0, 0))],
        out_specs=[hspec, hspec, hspec],
        out_shape=[jax.ShapeDtypeStruct((nb, KV_HEADS, rows, HEAD_DIM), F32),
                   jax.ShapeDtypeStruct((nb, KV_HEADS, rows, LANES), I32),
                   jax.ShapeDtypeStruct((nb, KV_HEADS, rows, LANES), F32)],
        compiler_params=_cparams("arbitrary", "arbitrary"), name="attn_sample_cmp",
    )(qc, kc, vc, ov)


def _attn_s2_body(phys_ref, idx_ref, ok_ref, new_ref, q_ref, kb_ref, vb_ref, kn_ref, vn_ref, wk_ref, wv_ref,
                  kwn_ref, vwn_ref, oc_ref, g_ref, o_ref, m_ref, l_ref, acc_ref, ow_ref, *, qpos, n_past_blk, past_len):
    b, g, kk = pl.program_id(0), pl.program_id(1), pl.program_id(2)
    q = q_ref[...]
    rows = q.shape[0]

    def update(s, v):
        m_prev = m_ref[...]
        m_new = jnp.maximum(m_prev, jnp.max(s, -1, keepdims=True))
        alpha = jnp.exp(m_prev - m_new)
        e = jnp.exp(s - m_new)
        l_ref[...] = alpha * l_ref[...] + jnp.sum(e, -1, keepdims=True)
        acc_ref[...] = alpha * acc_ref[...] + _dot(e.astype(BF16), v)
        m_ref[...] = m_new

    @pl.when(kk == 0)
    def _():
        wk = wk_ref.shape[0]
        kp = (past_len - wk) + lax.broadcasted_iota(I32, (1, wk), 1)
        mask = (kp <= qpos) & (kp > qpos - WINDOW)
        s = jnp.where(mask, _dot_nt(q, wk_ref[...].astype(BF16)) * ATTN_SCALE, NEG)
        sn = _dot_nt(q, kwn_ref[...].astype(BF16)) * ATTN_SCALE
        coln = lax.broadcasted_iota(I32, (1, sn.shape[1]), 1)
        sn = jnp.where(coln == 0, sn, NEG)
        m = jnp.maximum(jnp.max(s, -1, keepdims=True), jnp.max(sn, -1, keepdims=True))
        e, en = jnp.exp(s - m), jnp.exp(sn - m)
        den = jnp.sum(e, -1, keepdims=True) + jnp.sum(en, -1, keepdims=True)
        ow_ref[...] = (_dot(e.astype(BF16), wv_ref[...].astype(BF16))
                       + _dot(en.astype(BF16), vwn_ref[...].astype(BF16))) / den
        m_ref[...] = jnp.full(m_ref.shape, NEG, F32)
        l_ref[...] = jnp.zeros(l_ref.shape, F32)
        acc_ref[...] = jnp.zeros(acc_ref.shape, F32)

    t = (b * KV_HEADS + g) * SEL_TOP + kk
    kp = idx_ref[t] * SEL_BLOCK + lax.broadcasted_iota(I32, (1, SEL_BLOCK), 1)
    limit = jnp.where((ok_ref[t] == 1) & (idx_ref[t] < n_past_blk), qpos, -1)
    s = jnp.where(kp <= limit, _dot_nt(q, kb_ref[...].astype(BF16)) * ATTN_SCALE, NEG)
    update(s, vb_ref[...].astype(BF16))

    @pl.when(kk == SEL_TOP - 1)
    def _():
        sn = _dot_nt(q, kn_ref[...].astype(BF16)) * ATTN_SCALE
        coln = lax.broadcasted_iota(I32, (1, sn.shape[1]), 1)
        sn = jnp.where(coln < new_ref[b * KV_HEADS + g], sn, NEG)
        update(sn, vn_ref[...].astype(BF16))
        gt = g_ref[...]
        o_ref[...] = gt[:, 0:1] * oc_ref[...] + gt[:, 1:2] * (acc_ref[...] / l_ref[...]) + gt[:, 2:3] * ow_ref[...]


def _attn_s2(qr, pool_k, pool_v, kn, vn, win_k, win_v, kwn, vwn, o_cmp, gates, phys, idx, ok, has_new, *,
             qpos, past_len):
    nb, _, rows, _ = qr.shape
    wk = win_k.shape[1]
    hspec = pl.BlockSpec((None, None, rows, HEAD_DIM), lambda b, g, k, *_: (b, g, 0, 0))
    bspec = pl.BlockSpec((None, SEL_BLOCK, HEAD_DIM),
                         lambda b, g, k, ph, *_: (ph[(b * KV_HEADS + g) * SEL_TOP + k], 0, g))
    wspec = pl.BlockSpec((None, wk, HEAD_DIM), lambda b, g, k, *_: (b, 0, g))
    body = functools.partial(_attn_s2_body, qpos=qpos, n_past_blk=past_len // SEL_BLOCK, past_len=past_len)
    return pl.pallas_call(
        body,
        grid_spec=pltpu.PrefetchScalarGridSpec(
            num_scalar_prefetch=4, grid=(nb, KV_HEADS, SEL_TOP),
            in_specs=[hspec, bspec, bspec, hspec, hspec, wspec, wspec, hspec, hspec, hspec, hspec],
            out_specs=hspec,
            scratch_shapes=[pltpu.VMEM((rows, 1), F32), pltpu.VMEM((rows, 1), F32), pltpu.VMEM((rows, HEAD_DIM), F32),
                            pltpu.VMEM((rows, HEAD_DIM), F32)]),
        out_shape=jax.ShapeDtypeStruct((nb, KV_HEADS, rows, HEAD_DIM), F32),
        compiler_params=_cparams("arbitrary", "arbitrary", "arbitrary"), name="attn_sample_sel_win",
    )(phys, idx, ok, has_new, qr, pool_k, pool_v, kn, vn, win_k, win_v, kwn, vwn, o_cmp, gates)


def _head_rows(a, rows=8):
    nb = a.shape[0]
    p = a.shape[1] // (KV_HEADS * HEAD_DIM)
    a = a.reshape(nb, KV_HEADS, p, HEAD_DIM)
    return jnp.pad(a, ((0, 0), (0, 0), (0, rows - p), (0, 0)))


def kernel(x_prompt, x_sample, cache_cmp_k, cache_cmp_v, cache_sel_k, cache_sel_v, state_win_k, state_win_v, page_table, norm_mix, norm_ffn, norm_kv, norm_final, gm_w_in, gm_ln_g, gm_ln_b, gm_w_s, gm_b_s, gm_w_out, nsa_w_in, nsa_w_out, kv_w, cmp_pe_k, cmp_w1_k, cmp_w2_k, cmp_pe_v, cmp_w1_v, cmp_w2_v, ff_w_gate, ff_w_up, ff_w_down, moe_router, moe_w_gate, moe_w_up, moe_w_down):
    B, S, D = x_prompt.shape
    DB, DS, _ = x_sample.shape
    assert DS == 1 and D == D_MODEL and S % KEY_TILE == 0
    n_prompt, n_sample = B * S, DB * DS
    n_tok = n_prompt + n_sample
    M = n_prompt + GM_CHUNK
    assert M % ROW_TILE == 0 and M % ROW_TILE_DOWN == 0 and n_sample <= 8
    n_pool = cache_cmp_k.shape[0]
    past_len = page_table.shape[1] * PAGE_SIZE
    qpos_s = past_len
    wk = state_win_k.shape[1]
    assert wk == min(WINDOW, past_len)

    x = jnp.concatenate([x_prompt.reshape(n_prompt, D), x_sample.reshape(n_sample, D),
                         jnp.zeros((M - n_tok, D), F32)], 0)

    pos = jnp.concatenate([jnp.tile(jnp.arange(S), B), jnp.full((n_sample,), past_len), jnp.zeros((M - n_tok,), I32)])
    half = ROT_DIM // 2
    inv = ROPE_THETA ** (-jnp.arange(half, dtype=F32) / half)
    ang = pos.astype(F32)[:, None] * inv[None, :]
    cos, sin = jnp.cos(ang), jnp.sin(ang)
    zeros_h = jnp.zeros((M, half), F32)
    rest = HEAD_DIM - ROT_DIM
    rope_c = jnp.concatenate([cos, cos, jnp.ones((M, rest), F32)], 1)
    rope_s1 = jnp.concatenate([-sin, zeros_h, jnp.zeros((M, rest), F32)], 1)
    rope_s2 = jnp.concatenate([zeros_h, sin, jnp.zeros((M, rest), F32)], 1)
    rope_extras = [(t, (ROW_TILE, HEAD_DIM), lambda j, i: (i, 0)) for t in (rope_c, rope_s1, rope_s2)]

    def residual_mm(xin, w, res, tm, name):
        n = w.shape[1]
        return _ws_matmul(xin, [w], _ep_residual, [(n, F32)], tm=tm, tn=COL_TILE, n_cols=n,
                          extras=[(res, (tm, COL_TILE), lambda j, i: (i, j))], name=name)[0]

    gm_v = []
    kv = None
    for l in range(DEPTH):
        h = _rmsnorm(x, norm_mix[l], BF16, tm=ROW_TILE)
        if l < N_A_LAYERS:
            z = _ws_matmul(h, [gm_w_in[l]], _ep_gelu, [(2 * GM_DIM, BF16)], tm=ROW_TILE, tn=COL_TILE,
                           n_cols=2 * GM_DIM, name="gmlp_in")[0]
            o, v_s = _gm_gate(z, gm_ln_g[l], gm_ln_b[l], gm_w_s[l], gm_b_s[l], n_prompt=n_prompt, n_sample=n_sample)
            gm_v.append(v_s[:n_sample].reshape(DB, DS, GM_DIM))
            x = residual_mm(o, gm_w_out[l], x, ROW_TILE, "gmlp_out")
        else:
            bl = l - N_A_LAYERS
            w_in = nsa_w_in[bl]
            qc, qr = _ws_matmul(h, [w_in], _ep_q, [(N_HEADS * HEAD_DIM, BF16)] * 2, tm=ROW_TILE, tn=COL_TILE,
                                n_cols=N_HEADS * HEAD_DIM, extras=rope_extras, name="nsa_q")
            wg = w_in[:, N_HEADS * HEAD_DIM:].reshape(D, N_BRANCH, KV_HEADS, Q_PER_KV)
            wg = jnp.transpose(wg, (0, 2, 1, 3)).reshape(D, KV_HEADS, N_BRANCH * Q_PER_KV)
            wg = jnp.pad(wg, ((0, 0), (0, 0), (0, LANES - N_BRANCH * Q_PER_KV))).reshape(D, KV_HEADS * LANES)
            gates = _ws_matmul(h, [wg], _ep_sigmoid, [(KV_HEADS * LANES, F32)], tm=ROW_TILE, tn=KV_HEADS * LANES,
                               n_cols=KV_HEADS * LANES, name="nsa_gates")[0]
            o = _attn_prompt(qc, qr, kv['kc_p'], kv['vc_p'], kv['rb'], gates, batch=B, seq=S)
            qc_s = _head_rows(qc[n_prompt:n_tok])
            qr_s = _head_rows(qr[n_prompt:n_tok])
            oc_s, idx_s, val_s = _attn_s1(qc_s, kv['kc_s'], kv['vc_s'], qpos=qpos_s)
            idx = idx_s[:, :, 0, :SEL_TOP]
            ok = val_s[:, :, 0, :SEL_TOP] >= 0.0
            n_past_blk = past_len // SEL_BLOCK
            ppb = PAGE_SIZE // SEL_BLOCK
            ip = jnp.minimum(idx, n_past_blk - 1)
            phys = jnp.take_along_axis(page_table, (ip // ppb).reshape(DB, -1), 1).reshape(ip.shape) * ppb + ip % ppb
            has_new = jnp.any((idx >= n_past_blk) & ok, -1)
            g_s = gates[n_prompt:n_tok].reshape(DB, KV_HEADS, LANES)[:, :, :N_BRANCH * Q_PER_KV]
            g_s = jnp.transpose(g_s.reshape(DB, KV_HEADS, N_BRANCH, Q_PER_KV), (0, 1, 3, 2))
            g_s = jnp.pad(g_s, ((0, 0), (0, 0), (0, 8 - Q_PER_KV), (0, HEAD_DIM - N_BRANCH)))
            o_s = _attn_s2(qr_s, kv['pool_k'], kv['pool_v'], kv['ks_new'], kv['vs_new'], kv['win_k'], kv['win_v'],
                           kv['kw_new'], kv['vw_new'], oc_s, g_s, phys.reshape(-1).astype(I32),
                           idx.reshape(-1).astype(I32), ok.reshape(-1).astype(I32), has_new.reshape(-1).astype(I32),
                           qpos=qpos_s, past_len=past_len)
            o_s = o_s[:, :, :Q_PER_KV].reshape(n_sample, N_HEADS * HEAD_DIM).astype(BF16)
            o = jnp.concatenate([o, o_s, jnp.zeros((M - n_tok, N_HEADS * HEAD_DIM), BF16)], 0)
            x = residual_mm(o, nsa_w_out[bl], x, ROW_TILE, "nsa_out")

        if l % 2 == 0:
            h = _rmsnorm(x, norm_ffn[l], BF16, tm=ROW_TILE)
            hh = _ws_matmul(h, [ff_w_gate[l // 2], ff_w_up[l // 2]], _ep_swiglu, [(ff_w_gate.shape[2], BF16)],
                            tm=ROW_TILE, tn=COL_TILE, n_cols=ff_w_gate.shape[2], name="ffn_up")[0]
            x = residual_mm(hh, ff_w_down[l // 2], x, ROW_TILE_DOWN, "ffn_down")
        else:
            x = _moe(x, norm_ffn[l], moe_router[l // 2], moe_w_gate[l // 2], moe_w_up[l // 2], moe_w_down[l // 2],
                     n_tok=n_tok)

        if l == N_A_LAYERS - 1:
            hkv = _rmsnorm(x, norm_kv, BF16, tm=ROW_TILE)
            r, rb = _ws_matmul(hkv, [kv_w], _ep_kv, [(6 * KV_SEC, F32), (6 * KV_SEC, BF16)], tm=ROW_TILE, tn=KV_SEC,
                               n_cols=6 * KV_SEC, extras=rope_extras, name="kv_proj")
            pages_p = r.reshape(M // PAGE_SIZE, PAGE_SIZE, 6 * KV_SEC)
            table_p = jnp.arange(n_prompt // PAGE_SIZE, dtype=I32).reshape(B, S // PAGE_SIZE)
            r_s = r[n_prompt:n_tok]
            sec_s = lambda i: r_s[:, i * KV_SEC:(i + 1) * KV_SEC]
            pool = lambda c: c.reshape(n_pool * (PAGE_SIZE // SEL_BLOCK), SEL_BLOCK, KV_SEC)
            kv = dict(
                rb=rb,
                kc_p=_compress(pages_p, table_p, 0, cmp_pe_k, cmp_w1_k, cmp_w2_k),
                vc_p=_compress(pages_p, table_p, 1, cmp_pe_v, cmp_w1_v, cmp_w2_v),
                kc_s=_compress(cache_cmp_k.reshape(n_pool, PAGE_SIZE, KV_SEC), page_table, 0,
                               cmp_pe_k, cmp_w1_k, cmp_w2_k),
                vc_s=_compress(cache_cmp_v.reshape(n_pool, PAGE_SIZE, KV_SEC), page_table, 0,
                               cmp_pe_v, cmp_w1_v, cmp_w2_v),
                pool_k=pool(cache_sel_k), pool_v=pool(cache_sel_v),
                ks_new=_head_rows(sec_s(2)), vs_new=_head_rows(sec_s(3)),
                kw_new=_head_rows(sec_s(4)), vw_new=_head_rows(sec_s(5)),
                win_k=state_win_k.reshape(DB, wk, KV_SEC), win_v=state_win_v.reshape(DB, wk, KV_SEC),
            )
            r_p = r[:n_prompt]
            pg = lambda i: r_p[:, i * KV_SEC:(i + 1) * KV_SEC].reshape(B, S // PAGE_SIZE, PAGE_SIZE, KV_HEADS, HEAD_DIM)
            wkp = min(WINDOW, S)
            tail = lambda i: r_p[:, i * KV_SEC:(i + 1) * KV_SEC].reshape(B, S, KV_HEADS, HEAD_DIM)[:, S - wkp:]
            row_s = lambda i: sec_s(i).reshape(DB, DS, KV_HEADS, HEAD_DIM)
            state_p = (pg(0), pg(1), pg(2), pg(3), tail(4), tail(5))
            state_s = (row_s(0), row_s(1), row_s(2), row_s(3),
                       jnp.concatenate([state_win_k, row_s(4)], 1)[:, DS:],
                       jnp.concatenate([state_win_v, row_s(5)], 1)[:, DS:])

    y = _rmsnorm(x, norm_final, F32, tm=ROW_TILE)
    y_prompt = y[:n_prompt].reshape(B, S, D)
    y_sample = y[n_prompt:n_tok].reshape(DB, DS, D)
    return (y_prompt, y_sample) + state_p + state_s + (jnp.stack(gm_v, 0),)
```

```python
import functools

import numpy as np
import jax
import jax.numpy as jnp
from jax import lax
from jax.experimental import pallas as pl
from jax.experimental.pallas import tpu as pltpu

F32 = jnp.float32
BF16 = jnp.bfloat16
I32 = jnp.int32

D_MODEL = 2048
DEPTH = 4
PAGE_SIZE = 128
N_A_LAYERS = DEPTH // 2
GM_CHUNK = 128
GM_DIM = D_MODEL
GM_GROUPS = 16
GM_GW = GM_DIM // GM_GROUPS
N_HEADS = 16
HEAD_DIM = 128
KV_HEADS = 4
Q_PER_KV = N_HEADS // KV_HEADS
ROT_DIM = HEAD_DIM // 4
ROPE_THETA = 500000.0
CMP_LEN = 32
CMP_STRIDE = 16
CMP_R = CMP_LEN // CMP_STRIDE
SEL_BLOCK = 64
SEL_TOP = 16
WINDOW = 512
N_BRANCH = 3
KV_SEC = KV_HEADS * HEAD_DIM
ATTN_SCALE = HEAD_DIM ** -0.5
N_EXPERTS = 8
TOP_K = 2
EPS = 1e-6
NEG = -1e30
FORCE_SCORE = 1e6

LANES = 128
VMEM_LIMIT_BYTES = 56 * 1024 * 1024
ROW_TILE = 832
ROW_TILE_DOWN = 416
COL_TILE = 512
MOE_BLOCK = 256
KEY_TILE = 256
Q_TILE = 128
PAGES_PER_STEP = 8


def _cparams(*sem):
    return pltpu.CompilerParams(dimension_semantics=sem, vmem_limit_bytes=VMEM_LIMIT_BYTES)


def _dot(a, b):
    return jnp.dot(a, b, preferred_element_type=F32)


def _dot_nt(a, b):
    return lax.dot_general(a, b, (((1,), (1,)), ((), ())), preferred_element_type=F32)


def _dot3(a, b):
    ah = a.astype(BF16)
    al = (a - ah.astype(F32)).astype(BF16)
    bh = b.astype(BF16)
    bl = (b - bh.astype(F32)).astype(BF16)
    return _dot(ah, bh) + _dot(al, bh) + _dot(ah, bl)


def _ws_body(*refs, nw, n_extra, n_out, epilogue):
    x_ref = refs[0]
    w_refs = refs[1:1 + nw]
    e_refs = refs[1 + nw:1 + nw + n_extra]
    o_refs = refs[1 + nw + n_extra:1 + nw + n_extra + n_out]
    wb_refs = refs[1 + nw + n_extra + n_out:]
    j = pl.program_id(0)

    @pl.when(pl.program_id(1) == 0)
    def _():
        for w_ref, wb in zip(w_refs, wb_refs):
            wb[...] = w_ref[...].astype(BF16)

    accs = [_dot(x_ref[...], wb[...]) for wb in wb_refs]
    outs = epilogue(j, accs, e_refs)
    for o_ref, o in zip(o_refs, outs):
        o_ref[...] = o.astype(o_ref.dtype)


def _ws_matmul(x, ws, epilogue, outs, *, tm, tn, n_cols, extras=(), name):
    M, K = x.shape
    nj = n_cols // tn
    ni = M // tm
    assert nj * tn == n_cols and ni * tm == M
    in_specs = [pl.BlockSpec((tm, K), lambda j, i: (i, 0))]
    in_specs += [pl.BlockSpec((None, K, tn), lambda j, i, layer=layer: (layer, 0, j)) for _, layer in ws]
    ws = [w for w, _ in ws]
    in_specs += [pl.BlockSpec(bs, im) for _, bs, im in extras]
    out_shape = [jax.ShapeDtypeStruct((M, c), dt) for c, dt in outs]
    out_specs = [pl.BlockSpec((tm, c // nj), lambda j, i: (i, j)) for c, _ in outs]
    body = functools.partial(_ws_body, nw=len(ws), n_extra=len(extras), n_out=len(outs), epilogue=epilogue)
    res = pl.pallas_call(
        body, grid=(nj, ni), in_specs=in_specs, out_specs=out_specs, out_shape=out_shape,
        scratch_shapes=[pltpu.VMEM((K, tn), BF16) for _ in ws],
        compiler_params=_cparams("arbitrary", "arbitrary"), name=name,
    )(x, *ws, *[a for a, _, _ in extras])
    return res


def _rope_tile(acc, c_ref, s1_ref, s2_ref):
    c, s1, s2 = c_ref[...], s1_ref[...], s2_ref[...]
    parts = []
    for h in range(acc.shape[1] // HEAD_DIM):
        xh = acc[:, h * HEAD_DIM:(h + 1) * HEAD_DIM]
        parts.append(xh * c + pltpu.roll(xh, HEAD_DIM - ROT_DIM // 2, 1) * s1 + pltpu.roll(xh, ROT_DIM // 2, 1) * s2)
    return jnp.concatenate(parts, 1)


def _ep_gelu(j, accs, e):
    return [jax.nn.gelu(accs[0])]


def _ep_residual(j, accs, e):
    return [e[0][...] + accs[0]]


def _ep_swiglu(j, accs, e):
    return [jax.nn.silu(accs[0]) * accs[1]]


def _ep_sigmoid(j, accs, e):
    return [jax.nn.sigmoid(accs[0])]


def _ep_kv(j, accs, e):
    acc = accs[0]
    roped = _rope_tile(acc, *e)
    r = jnp.where((j == 2) | (j == 4), roped, acc)
    return [r, r]


def _ep_q(j, accs, e):
    acc = accs[0]
    return [acc, _rope_tile(acc, *e)]


def _rms(x, g):
    return (x * lax.rsqrt(jnp.mean(x * x, -1, keepdims=True) + EPS)) * g


def _rms_body(x_ref, g_ref, o_ref):
    o_ref[...] = _rms(x_ref[...], g_ref[...]).astype(o_ref.dtype)


def _rmsnorm(x, g, out_dtype, *, tm):
    M, D = x.shape
    return pl.pallas_call(
        _rms_body, grid=(M // tm,),
        in_specs=[pl.BlockSpec((tm, D), lambda i: (i, 0)), pl.BlockSpec((1, D), lambda i: (0, 0))],
        out_specs=pl.BlockSpec((tm, D), lambda i: (i, 0)),
        out_shape=jax.ShapeDtypeStruct((M, D), out_dtype),
        compiler_params=_cparams("arbitrary"), name="rmsnorm",
    )(x, g.reshape(1, D))


def _rms_router_body(x_ref, g_ref, r_ref, h_ref, te_ref, tg_ref):
    h = _rms(x_ref[...], g_ref[...])
    h_ref[...] = h
    lane = lax.broadcasted_iota(I32, (1, LANES), 1).astype(F32)
    logits = jnp.where(lane < N_EXPERTS, _dot3(h, r_ref[...]), -jnp.inf)
    m1 = jnp.max(logits, -1, keepdims=True)
    i1 = jnp.min(jnp.where(logits == m1, lane, float(LANES)), -1, keepdims=True)
    rest = jnp.where(lane == i1, -jnp.inf, logits)
    m2 = jnp.max(rest, -1, keepdims=True)
    i2 = jnp.min(jnp.where(rest == m2, lane, float(LANES)), -1, keepdims=True)
    e2 = jnp.exp(m2 - m1)
    den = 1.0 + e2
    te_ref[...] = jnp.where(lane == 0, i1, jnp.where(lane == 1, i2, 0.0)).astype(I32)
    tg_ref[...] = jnp.where(lane == 0, 1.0 / den, jnp.where(lane == 1, e2 / den, 0.0))


def _rmsnorm_router(x, g, router, *, tm):
    M, D = x.shape
    rp = jnp.pad(router, ((0, 0), (0, LANES - N_EXPERTS)))
    return pl.pallas_call(
        _rms_router_body, grid=(M // tm,),
        in_specs=[pl.BlockSpec((tm, D), lambda i: (i, 0)), pl.BlockSpec((1, D), lambda i: (0, 0)),
                  pl.BlockSpec((D, LANES), lambda i: (0, 0))],
        out_specs=[pl.BlockSpec((tm, D), lambda i: (i, 0)), pl.BlockSpec((tm, LANES), lambda i: (i, 0)),
                   pl.BlockSpec((tm, LANES), lambda i: (i, 0))],
        out_shape=[jax.ShapeDtypeStruct((M, D), F32), jax.ShapeDtypeStruct((M, LANES), I32),
                   jax.ShapeDtypeStruct((M, LANES), F32)],
        compiler_params=_cparams("arbitrary"), name="rmsnorm_router",
    )(x, g.reshape(1, D), rp)


def _gm_gate_body(z_ref, lng_ref, lnb_ref, ws_ref, bst_ref, w00_ref, b00_ref, o_ref, vs_ref, *,
                  n_prompt_chunks, n_sample):
    c = pl.program_id(0)
    z = z_ref[...]
    u = z[:, :GM_DIM].astype(F32)
    v = z[:, GM_DIM:].astype(F32)
    mu = jnp.mean(v, -1, keepdims=True)
    vc = v - mu
    vn = (vc * lax.rsqrt(jnp.mean(vc * vc, -1, keepdims=True) + EPS)) * lng_ref[...] + lnb_ref[...]

    @pl.when(c < n_prompt_chunks)
    def _():
        vb = vn.astype(BF16)
        tri = lax.broadcasted_iota(I32, (GM_CHUNK, GM_CHUNK), 0) >= lax.broadcasted_iota(I32, (GM_CHUNK, GM_CHUNK), 1)
        for g in range(GM_GROUPS):
            sl = slice(g * GM_GW, (g + 1) * GM_GW)
            wg = jnp.where(tri, ws_ref[g], 0.0).astype(BF16)
            s = _dot(wg, vb[:, sl]) + bst_ref[:, g:g + 1]
            o_ref[:, sl] = (u[:, sl] * s).astype(o_ref.dtype)

    @pl.when(c >= n_prompt_chunks)
    def _():
        row = lax.broadcasted_iota(I32, (GM_CHUNK, 1), 0)
        s = w00_ref[...] * vn + b00_ref[...]
        o_ref[...] = jnp.where(row < n_sample, u * s, 0.0).astype(o_ref.dtype)
        vs_ref[...] = vn[:vs_ref.shape[0]]


def _gm_gate(z, ln_g, ln_b, w_s, b_s, *, n_prompt, n_sample):
    M = z.shape[0]
    n_pc = n_prompt // GM_CHUNK
    assert n_prompt % GM_CHUNK == 0 and M == (n_pc + 1) * GM_CHUNK and n_sample <= GM_CHUNK
    n_vs = -(-n_sample // 8) * 8
    w00 = jnp.repeat(w_s[:, 0, 0], GM_GW).reshape(1, GM_DIM)
    b00 = jnp.repeat(b_s[:, 0], GM_GW).reshape(1, GM_DIM)
    body = functools.partial(_gm_gate_body, n_prompt_chunks=n_pc, n_sample=n_sample)
    full = lambda *shape: pl.BlockSpec(shape, lambda c: (0,) * len(shape))
    return pl.pallas_call(
        body, grid=(n_pc + 1,),
        in_specs=[pl.BlockSpec((GM_CHUNK, 2 * GM_DIM), lambda c: (c, 0)), full(1, GM_DIM), full(1, GM_DIM),
                  full(GM_GROUPS, GM_CHUNK, GM_CHUNK), full(GM_CHUNK, GM_GROUPS), full(1, GM_DIM), full(1, GM_DIM)],
        out_specs=[pl.BlockSpec((GM_CHUNK, GM_DIM), lambda c: (c, 0)), full(n_vs, GM_DIM)],
        out_shape=[jax.ShapeDtypeStruct((M, GM_DIM), BF16), jax.ShapeDtypeStruct((n_vs, GM_DIM), F32)],
        compiler_params=_cparams("arbitrary"), name="gmlp_gate",
    )(z, ln_g.reshape(1, GM_DIM), ln_b.reshape(1, GM_DIM), w_s, b_s.T, w00, b00)


def _moe_gather_body(rt_ref, nu_ref, h_hbm, o_ref, buf, sem):
    i = pl.program_id(0)

    def row_copy(tok, r):
        return pltpu.make_async_copy(h_hbm.at[pl.ds(tok, 1), :], buf.at[pl.ds(r, 1), :], sem)

    @pl.when(i < nu_ref[0])
    def _():
        def issue(r, c):
            row_copy(rt_ref[i * MOE_BLOCK + r], r).start()
            return c

        def wait(r, c):
            row_copy(0, r).wait()
            return c

        lax.fori_loop(0, MOE_BLOCK, issue, 0, unroll=8)
        lax.fori_loop(0, MOE_BLOCK, wait, 0, unroll=8)
        o_ref[...] = buf[...].astype(o_ref.dtype)

    @pl.when(i >= nu_ref[0])
    def _():
        o_ref[...] = jnp.zeros(o_ref.shape, o_ref.dtype)


def _moe_gather(h, row_tok, n_used, n_blk):
    D = h.shape[1]
    return pl.pallas_call(
        _moe_gather_body,
        grid_spec=pltpu.PrefetchScalarGridSpec(
            num_scalar_prefetch=2, grid=(n_blk,),
            in_specs=[pl.BlockSpec(memory_space=pl.ANY)],
            out_specs=pl.BlockSpec((MOE_BLOCK, D), lambda i, rt, nu: (i, 0)),
            scratch_shapes=[pltpu.VMEM((MOE_BLOCK, D), F32), pltpu.SemaphoreType.DMA(())]),
        out_shape=jax.ShapeDtypeStruct((n_blk * MOE_BLOCK, D), BF16),
        compiler_params=_cparams("arbitrary"), name="moe_gather",
    )(row_tok, n_used, h)


def _moe_mm_body(be_ref, bf_ref, nu_ref, *refs, nw, epilogue):
    x_ref = refs[0]
    w_refs = refs[1:1 + nw]
    o_ref = refs[1 + nw]
    wb_refs = refs[2 + nw:]
    i = pl.program_id(1)
    live = i < nu_ref[0]

    @pl.when(live & (bf_ref[i] == 1))
    def _():
        for w_ref, wb in zip(w_refs, wb_refs):
            wb[...] = w_ref[...].astype(BF16)

    @pl.when(live)
    def _():
        accs = [_dot(x_ref[...], wb[...]) for wb in wb_refs]
        o_ref[...] = epilogue(accs).astype(o_ref.dtype)

    @pl.when(jnp.logical_not(live))
    def _():
        o_ref[...] = jnp.zeros(o_ref.shape, o_ref.dtype)


def _moe_matmul(x, ws, layer, blk_e, blk_first, n_used, epilogue, out_dtype, *, tn, name):
    M, K = x.shape
    N = ws[0].shape[3]
    nj, ni = N // tn, M // MOE_BLOCK
    assert nj * tn == N
    clamp = lambda i, nu: jnp.minimum(i, nu[0] - 1)
    in_specs = [pl.BlockSpec((MOE_BLOCK, K), lambda j, i, be, bf, nu: (clamp(i, nu), 0))]
    in_specs += [pl.BlockSpec((None, None, K, tn), lambda j, i, be, bf, nu: (layer, be[clamp(i, nu)], 0, j))
                 for _ in ws]
    body = functools.partial(_moe_mm_body, nw=len(ws), epilogue=epilogue)
    return pl.pallas_call(
        body,
        grid_spec=pltpu.PrefetchScalarGridSpec(
            num_scalar_prefetch=3, grid=(nj, ni), in_specs=in_specs,
            out_specs=pl.BlockSpec((MOE_BLOCK, tn), lambda j, i, be, bf, nu: (i, j)),
            scratch_shapes=[pltpu.VMEM((K, tn), BF16) for _ in ws]),
        out_shape=jax.ShapeDtypeStruct((M, N), out_dtype),
        compiler_params=_cparams("arbitrary", "arbitrary"), name=name,
    )(blk_e, blk_first, n_used, x, *ws)


def _moe_combine_body(d0_ref, d1_ref, x_ref, g_ref, yb_hbm, o_ref, b0, b1, sem, *, tm):
    i = pl.program_id(0)

    def row_copy(src_row, buf, r):
        return pltpu.make_async_copy(yb_hbm.at[pl.ds(src_row, 1), :], buf.at[pl.ds(r, 1), :], sem)

    def issue(r, c):
        row_copy(d0_ref[i * tm + r], b0, r).start()
        row_copy(d1_ref[i * tm + r], b1, r).start()
        return c

    def wait(r, c):
        row_copy(0, b0, r).wait()
        row_copy(0, b1, r).wait()
        return c

    lax.fori_loop(0, tm, issue, 0, unroll=8)
    lax.fori_loop(0, tm, wait, 0, unroll=8)
    g = g_ref[...]
    o_ref[...] = x_ref[...] + (g[:, 0:1] * b0[...] + g[:, 1:2] * b1[...])


def _moe_combine(x, gates, yb, d0, d1, *, tm):
    M, D = x.shape
    body = functools.partial(_moe_combine_body, tm=tm)
    return pl.pallas_call(
        body,
        grid_spec=pltpu.PrefetchScalarGridSpec(
            num_scalar_prefetch=2, grid=(M // tm,),
            in_specs=[pl.BlockSpec((tm, D), lambda i, a, b: (i, 0)), pl.BlockSpec((tm, LANES), lambda i, a, b: (i, 0)),
                      pl.BlockSpec(memory_space=pl.ANY)],
            out_specs=pl.BlockSpec((tm, D), lambda i, a, b: (i, 0)),
            scratch_shapes=[pltpu.VMEM((tm, D), F32), pltpu.VMEM((tm, D), F32), pltpu.SemaphoreType.DMA(())]),
        out_shape=jax.ShapeDtypeStruct((M, D), F32),
        compiler_params=_cparams("arbitrary"), name="moe_combine",
    )(d0, d1, x, gates, yb)


def _moe(x, g_norm, router, wg, wu, wd, layer, *, n_tok):
    M, D = x.shape
    h, te, tg = _rmsnorm_router(x, g_norm, router, tm=ROW_TILE)
    a = n_tok * TOP_K
    n_blk = -(-a // MOE_BLOCK) + N_EXPERTS
    e_flat = te[:n_tok, :TOP_K].reshape(a)
    onehot = (e_flat[:, None] == jnp.arange(N_EXPERTS, dtype=I32)[None, :]).astype(I32)
    csum = jnp.cumsum(onehot, 0)
    pos = jnp.take_along_axis(csum, e_flat[:, None], 1)[:, 0] - 1
    counts = csum[-1]
    padded = (counts + MOE_BLOCK - 1) // MOE_BLOCK * MOE_BLOCK
    end_pad = jnp.cumsum(padded)
    start_pad = end_pad - padded
    dest = (start_pad[e_flat] + pos).astype(I32)
    tok = jnp.arange(a, dtype=I32) // TOP_K
    row_tok = jnp.full((n_blk * MOE_BLOCK,), n_tok, I32).at[dest].set(tok)
    n_used = (end_pad[-1] // MOE_BLOCK).astype(I32).reshape(1)
    blk_e = jnp.minimum(jnp.searchsorted(end_pad, jnp.arange(n_blk, dtype=I32) * MOE_BLOCK, side='right'),
                        N_EXPERTS - 1).astype(I32)
    blk_first = jnp.concatenate([jnp.ones((1,), I32), (blk_e[1:] != blk_e[:-1]).astype(I32)])
    d = jnp.pad(dest.reshape(n_tok, TOP_K), ((0, M - n_tok), (0, 0)))
    row = jnp.arange(M, dtype=I32)[:, None]
    gates = jnp.where(row < n_tok, tg, 0.0)

    xb = _moe_gather(h, row_tok, n_used, n_blk)
    hb = _moe_matmul(xb, [wg, wu], layer, blk_e, blk_first, n_used, lambda accs: jax.nn.silu(accs[0]) * accs[1],
                     BF16, tn=COL_TILE, name="moe_up")
    yb = _moe_matmul(hb, [wd], layer, blk_e, blk_first, n_used, lambda accs: accs[0], F32, tn=COL_TILE,
                     name="moe_down")
    return _moe_combine(x, gates, yb, d[:, 0], d[:, 1], tm=ROW_TILE_DOWN)


def _cmp_z_body(pt_ref, *refs, heads_interleaved):
    n_in = len(refs) - 2
    page_refs = refs[:n_in]
    w_ref = refs[n_in]
    o_ref = refs[n_in + 1]
    n_ch = PAGE_SIZE // CMP_STRIDE
    acc = jnp.zeros((KV_HEADS * PAGES_PER_STEP * n_ch, CMP_R * HEAD_DIM), F32)
    for s in range(CMP_STRIDE):
        if heads_interleaved:
            rows = [pg[pl.ds(KV_HEADS * s + k, n_ch, stride=KV_HEADS * CMP_STRIDE), :]
                    for k in range(KV_HEADS) for pg in page_refs]
        else:
            rows = [pg[pl.ds(s, n_ch, stride=CMP_STRIDE), :] for pg in page_refs]
        acc = acc + _dot(jnp.concatenate(rows, 0).astype(BF16), w_ref[s])
    m = PAGES_PER_STEP * n_ch
    for k in range(KV_HEADS):
        o_ref[k] = acc[k * m:(k + 1) * m]


def _cmp_z(pages, table, sec, w1cat):
    nb, npg = table.shape
    assert npg % PAGES_PER_STEP == 0
    n_ch = PAGE_SIZE // CMP_STRIDE
    m = PAGES_PER_STEP * n_ch
    interleaved = pages.ndim == 2
    if interleaved:
        specs = [pl.BlockSpec((PAGE_SIZE * KV_HEADS, HEAD_DIM),
                              lambda b, g, pt, q=q: (pt[b, g * PAGES_PER_STEP + q], 0)) for q in range(PAGES_PER_STEP)]
    else:
        specs = [pl.BlockSpec((None, PAGE_SIZE, HEAD_DIM),
                              lambda b, g, pt, k=k, q=q: (pt[b, g * PAGES_PER_STEP + q], 0, sec * KV_HEADS + k))
                 for k in range(KV_HEADS) for q in range(PAGES_PER_STEP)]
    return pl.pallas_call(
        functools.partial(_cmp_z_body, heads_interleaved=interleaved),
        grid_spec=pltpu.PrefetchScalarGridSpec(
            num_scalar_prefetch=1, grid=(nb, npg // PAGES_PER_STEP),
            in_specs=specs + [pl.BlockSpec((CMP_STRIDE, HEAD_DIM, CMP_R * HEAD_DIM), lambda b, g, pt: (0, 0, 0))],
            out_specs=pl.BlockSpec((None, KV_HEADS, m, CMP_R * HEAD_DIM), lambda b, g, pt: (b, 0, g, 0))),
        out_shape=jax.ShapeDtypeStruct((nb, KV_HEADS, npg * n_ch, CMP_R * HEAD_DIM), F32),
        compiler_params=_cparams("arbitrary", "arbitrary"), name="cmp_z",
    )(table, *([pages] * len(specs)), w1cat)


def _cmp_fin_body(z_ref, pe_ref, w1_ref, w2_ref, o_ref):
    z = z_ref[...]
    n = z.shape[0]
    pe = jnp.broadcast_to(pe_ref[...], (8, pe_ref.shape[1])).astype(BF16)
    pre = _dot(pe, w1_ref[...].astype(BF16))[0:1]
    pre = pre + z[:, :HEAD_DIM]
    pre = pre + pltpu.roll(z[:, HEAD_DIM:], n - 1, 0)
    o_ref[...] = _dot(jax.nn.gelu(pre).astype(BF16), w2_ref[...].astype(BF16)).astype(o_ref.dtype)


def _cmp_fin(z, pe, w1, w2):
    nb, _, n, _ = z.shape
    return pl.pallas_call(
        _cmp_fin_body, grid=(nb, KV_HEADS),
        in_specs=[pl.BlockSpec((None, None, n, CMP_R * HEAD_DIM), lambda b, k: (b, k, 0, 0)),
                  pl.BlockSpec((1, CMP_LEN * HEAD_DIM), lambda b, k: (0, 0)),
                  pl.BlockSpec((CMP_LEN * HEAD_DIM, HEAD_DIM), lambda b, k: (0, 0)),
                  pl.BlockSpec((HEAD_DIM, HEAD_DIM), lambda b, k: (0, 0))],
        out_specs=pl.BlockSpec((None, None, n, HEAD_DIM), lambda b, k: (b, k, 0, 0)),
        out_shape=jax.ShapeDtypeStruct((nb, KV_HEADS, n, HEAD_DIM), BF16),
        compiler_params=_cparams("arbitrary", "arbitrary"), name="cmp_fin",
    )(z, pe.reshape(1, -1), w1, w2)


def _w1cat(w1):
    w = w1.reshape(CMP_R, CMP_STRIDE, HEAD_DIM, HEAD_DIM)
    return jnp.transpose(w, (1, 2, 0, 3)).reshape(CMP_STRIDE, HEAD_DIM, CMP_R * HEAD_DIM).astype(BF16)


def _compress(pages, table, sec, pe, w1, w2):
    return _cmp_fin(_cmp_z(pages, table, sec, _w1cat(w1)), pe, w1, w2)


def _overlap_matrix(nc, ns, rows, cols):
    cs = np.arange(nc) * CMP_STRIDE
    ss = np.arange(ns) * SEL_BLOCK
    ov = np.minimum(cs[:, None] + CMP_LEN, ss[None, :] + SEL_BLOCK) - np.maximum(cs[:, None], ss[None, :])
    ov = np.clip(ov, 0, None).astype(np.float32) / np.float32(CMP_LEN)
    out = np.zeros((rows, cols), np.float32)
    out[:nc, :ns] = ov
    return jnp.asarray(out)


def _attn_prompt_body(qc_ref, qr_ref, kc_ref, vc_ref, ks_ref, vs_ref, kw_ref, vw_ref, g_ref, ovt_ref, ext_ref, o_ref,
                      allow_ref, vct_ref, vst_ref, vwt_ref, ocmp_ref, osel_ref, acc_ref, *, seq, n_cmp, n_sel):
    qi = pl.program_id(2)
    qpos = qi * Q_TILE + lax.broadcasted_iota(I32, (1, Q_TILE), 1)
    row = lax.broadcasted_iota(I32, (LANES, 1), 0)

    @pl.when(qi == 0)
    def _():
        vct_ref[...] = vc_ref[...].astype(F32).T.astype(BF16)
        vst_ref[...] = vs_ref[...].astype(F32).T.astype(BF16)
        vwt_ref[...] = vw_ref[...].astype(F32).T.astype(BF16)

    cmask = (row * CMP_STRIDE + (CMP_LEN - 1) <= qpos) & (row < n_cmp)
    kc = kc_ref[...]
    psum = jnp.zeros((LANES, Q_TILE), F32)
    for p in range(Q_PER_KV):
        s = _dot_nt(kc, qc_ref[:, p * HEAD_DIM:(p + 1) * HEAD_DIM]) * ATTN_SCALE
        sm = jnp.where(cmask, s, NEG)
        e = jnp.exp(sm - jnp.max(sm, 0, keepdims=True))
        pr = jnp.where(cmask, e / jnp.sum(e, 0, keepdims=True), 0.0)
        psum = psum + pr
        ocmp_ref[p] = _dot(vct_ref[...], pr.astype(BF16))

    n_sel_pad = -(-n_sel // 8) * 8
    imp = _dot3(ovt_ref[...], psum)[:n_sel_pad]
    blk = row[:n_sel_pad]
    qblk = qpos >> 6
    valid = (blk * SEL_BLOCK <= qpos) & (blk < n_sel)
    forced = (blk == 0) | (blk == qblk) | (blk == qblk - 1)
    score = jnp.where(forced & valid, FORCE_SCORE, jnp.where(valid, imp, -1.0))
    rank = jnp.zeros(score.shape, F32)
    for j in range(n_sel):
        cj = score[j:j + 1, :]
        beats = (cj > score) | ((cj == score) & (blk > j))
        rank = rank + jnp.where(beats, 1.0, 0.0)
    sel = jnp.where((rank < SEL_TOP) & (score >= 0.0), 1.0, 0.0)
    sel = jnp.concatenate([sel, jnp.zeros((LANES - n_sel_pad, Q_TILE), F32)], 0).astype(BF16)
    selx = _dot(ext_ref[...], sel)
    kpos_all = lax.broadcasted_iota(I32, (seq, 1), 0)
    allow_ref[...] = jnp.where((selx > 0.5) & (kpos_all <= qpos), 1.0, 0.0)

    def flash(k_ref, vt_ref, lo, hi, mask_fn):
        acc_ref[...] = jnp.zeros(acc_ref.shape, F32)

        def body(kt, carry):
            ms, ls = carry
            start = pl.multiple_of(kt * KEY_TILE, KEY_TILE)
            k = k_ref[pl.ds(start, KEY_TILE), :]
            vt = vt_ref[:, pl.ds(start, KEY_TILE)]
            mask = mask_fn(start)
            new_ms, new_ls = [], []
            for p in range(Q_PER_KV):
                s = _dot_nt(k, qr_ref[:, p * HEAD_DIM:(p + 1) * HEAD_DIM]) * ATTN_SCALE
                s = jnp.where(mask, s, NEG)
                m_new = jnp.maximum(ms[p], jnp.max(s, 0, keepdims=True))
                alpha = jnp.exp(ms[p] - m_new)
                e = jnp.exp(s - m_new)
                new_ls.append(alpha * ls[p] + jnp.sum(e, 0, keepdims=True))
                acc_ref[p] = alpha * acc_ref[p] + _dot(vt, e.astype(BF16))
                new_ms.append(m_new)
            return tuple(new_ms), tuple(new_ls)

        init = (tuple(jnp.full((1, Q_TILE), NEG, F32) for _ in range(Q_PER_KV)),
                tuple(jnp.zeros((1, Q_TILE), F32) for _ in range(Q_PER_KV)))
        _, ls = lax.fori_loop(lo, hi, body, init)
        return ls

    hi = (qi * Q_TILE + Q_TILE - 1) // KEY_TILE + 1
    ls = flash(ks_ref, vst_ref, 0, hi, lambda start: allow_ref[pl.ds(start, KEY_TILE), :] > 0.5)
    for p in range(Q_PER_KV):
        osel_ref[p] = acc_ref[p] / ls[p]

    def win_mask(start):
        kp = start + lax.broadcasted_iota(I32, (KEY_TILE, 1), 0)
        return (kp <= qpos) & (kp > qpos - WINDOW)

    ls = flash(kw_ref, vwt_ref, jnp.maximum(qi * Q_TILE - (WINDOW - 1), 0) // KEY_TILE, hi, win_mask)

    gt = g_ref[...].T
    for p in range(Q_PER_KV):
        ot = (gt[p:p + 1] * ocmp_ref[p] + gt[Q_PER_KV + p:Q_PER_KV + p + 1] * osel_ref[p]
              + gt[2 * Q_PER_KV + p:2 * Q_PER_KV + p + 1] * (acc_ref[p] / ls[p]))
        o_ref[:, p * HEAD_DIM:(p + 1) * HEAD_DIM] = ot.T.astype(o_ref.dtype)


def _attn_prompt(qc, qr, kc, vc, rb, gates, *, batch, seq):
    nq = seq // Q_TILE
    n_chunk = seq // CMP_STRIDE
    n_cmp = n_chunk - CMP_R + 1
    n_sel = seq // SEL_BLOCK
    assert n_chunk == LANES and n_sel <= LANES and seq % KEY_TILE == 0
    ovt = _overlap_matrix(n_cmp, n_sel, LANES, LANES).T
    ext = np.zeros((seq, LANES), np.float32)
    ext[np.arange(seq), np.arange(seq) // SEL_BLOCK] = 1.0
    ext = jnp.asarray(ext, BF16)
    qspec = pl.BlockSpec((Q_TILE, KV_SEC), lambda b, g, q: (b * nq + q, g))
    cspec = pl.BlockSpec((None, None, n_chunk, HEAD_DIM), lambda b, g, q: (b, g, 0, 0))
    kvspec = lambda sec: pl.BlockSpec((seq, HEAD_DIM), lambda b, g, q: (b, sec * KV_HEADS + g))
    body = functools.partial(_attn_prompt_body, seq=seq, n_cmp=n_cmp, n_sel=n_sel)
    return pl.pallas_call(
        body, grid=(batch, KV_HEADS, nq),
        in_specs=[qspec, qspec, cspec, cspec, kvspec(2), kvspec(3), kvspec(4), kvspec(5),
                  pl.BlockSpec((Q_TILE, LANES), lambda b, g, q: (b * nq + q, g)),
                  pl.BlockSpec((LANES, LANES), lambda b, g, q: (0, 0)),
                  pl.BlockSpec((seq, LANES), lambda b, g, q: (0, 0))],
        out_specs=qspec,
        out_shape=jax.ShapeDtypeStruct((batch * seq, N_HEADS * HEAD_DIM), BF16),
        scratch_shapes=[pltpu.VMEM((seq, Q_TILE), F32), pltpu.VMEM((HEAD_DIM, n_chunk), BF16),
                        pltpu.VMEM((HEAD_DIM, seq), BF16), pltpu.VMEM((HEAD_DIM, seq), BF16),
                        pltpu.VMEM((Q_PER_KV, HEAD_DIM, Q_TILE), F32), pltpu.VMEM((Q_PER_KV, HEAD_DIM, Q_TILE), F32),
                        pltpu.VMEM((Q_PER_KV, HEAD_DIM, Q_TILE), F32)],
        compiler_params=_cparams("arbitrary", "arbitrary", "arbitrary"), name="attn_prompt",
    )(qc, qr, kc, vc, rb, rb, rb, rb, gates, ovt, ext)


def _attn_s1_body(q_ref, kc_ref, vc_ref, ov_ref, o_ref, idx_ref, val_ref, *, qpos, n_cmp, n_sel):
    rows = q_ref.shape[0]
    n = kc_ref.shape[0]
    col = lax.broadcasted_iota(I32, (1, n), 1)
    cmask = (col * CMP_STRIDE + (CMP_LEN - 1) <= qpos) & (col < n_cmp)
    s = _dot_nt(q_ref[...], kc_ref[...]) * ATTN_SCALE
    sm = jnp.where(cmask, s, NEG)
    e = jnp.exp(sm - jnp.max(sm, -1, keepdims=True))
    pr = jnp.where(cmask, e / jnp.sum(e, -1, keepdims=True), 0.0)
    o_ref[...] = _dot(pr.astype(BF16), vc_ref[...])
    head = lax.broadcasted_iota(I32, (rows, 1), 0)
    psum = jnp.sum(jnp.where(head < Q_PER_KV, pr, 0.0), 0, keepdims=True)
    imp = _dot3(jnp.broadcast_to(psum, (rows, n)), ov_ref[...])[0:1]
    w = ov_ref.shape[1]
    blk = lax.broadcasted_iota(I32, (1, w), 1)
    qblk = qpos // SEL_BLOCK
    valid = (blk * SEL_BLOCK <= qpos) & (blk < n_sel)
    forced = (blk == 0) | (blk == qblk) | (blk == qblk - 1)
    low = -3e38
    score = jnp.where(blk < n_sel, jnp.where(forced & valid, FORCE_SCORE, jnp.where(valid, imp, -1.0)), low)
    blkf = blk.astype(F32)
    lane = lax.broadcasted_iota(I32, (1, LANES), 1)
    idx_row = jnp.zeros((1, LANES), F32)
    val_row = jnp.full((1, LANES), -1.0, F32)
    for t in range(SEL_TOP):
        m = jnp.max(score, -1, keepdims=True)
        it = jnp.min(jnp.where(score == m, blkf, float(w)), -1, keepdims=True)
        idx_row = jnp.where(lane == t, it, idx_row)
        val_row = jnp.where(lane == t, m, val_row)
        score = jnp.where(blkf == it, low, score)
    idx_ref[...] = jnp.broadcast_to(idx_row, idx_ref.shape).astype(I32)
    val_ref[...] = jnp.broadcast_to(val_row, val_ref.shape)


def _attn_s1(qc, kc, vc, *, qpos):
    nb, _, rows, _ = qc.shape
    n_chunk = kc.shape[2]
    t_all = qpos + 1
    n_cmp = t_all // CMP_STRIDE - CMP_R + 1
    n_sel = -(-t_all // SEL_BLOCK)
    assert n_cmp <= n_chunk and n_sel >= SEL_TOP
    w = -(-n_sel // LANES) * LANES
    ov = _overlap_matrix(n_cmp, n_sel, n_chunk, w)
    body = functools.partial(_attn_s1_body, qpos=qpos, n_cmp=n_cmp, n_sel=n_sel)
    hspec = pl.BlockSpec((None, None, rows, HEAD_DIM), lambda b, g: (b, g, 0, 0))
    cspec = pl.BlockSpec((None, None, n_chunk, HEAD_DIM), lambda b, g: (b, g, 0, 0))
    return pl.pallas_call(
        body, grid=(nb, KV_HEADS),
        in_specs=[hspec, cspec, cspec, pl.BlockSpec((n_chunk, w), lambda b, g: (0, 0))],
        out_specs=[hspec, hspec, hspec],
        out_shape=[jax.ShapeDtypeStruct((nb, KV_HEADS, rows, HEAD_DIM), F32),
                   jax.ShapeDtypeStruct((nb, KV_HEADS, rows, LANES), I32),
                   jax.ShapeDtypeStruct((nb, KV_HEADS, rows, LANES), F32)],
        compiler_params=_cparams("arbitrary", "arbitrary"), name="attn_sample_cmp",
    )(qc, kc, vc, ov)


def _attn_s2_body(phys_ref, idx_ref, ok_ref, new_ref, q_ref, kb_ref, vb_ref, kn_ref, vn_ref, wk_ref, wv_ref,
                  kwn_ref, vwn_ref, oc_ref, g_ref, o_ref, m_ref, l_ref, acc_ref, ow_ref, *, qpos, n_past_blk, past_len):
    b, g, kk = pl.program_id(0), pl.program_id(1), pl.program_id(2)
    q = q_ref[...]
    rows = q.shape[0]

    def update(s, v):
        m_prev = m_ref[...]
        m_new = jnp.maximum(m_prev, jnp.max(s, -1, keepdims=True))
        alpha = jnp.exp(m_prev - m_new)
        e = jnp.exp(s - m_new)
        l_ref[...] = alpha * l_ref[...] + jnp.sum(e, -1, keepdims=True)
        acc_ref[...] = alpha * acc_ref[...] + _dot(e.astype(BF16), v)
        m_ref[...] = m_new

    @pl.when(kk == 0)
    def _():
        wk = wk_ref.shape[0] // KV_HEADS
        kp = (past_len - wk) + lax.broadcasted_iota(I32, (1, wk), 1)
        mask = (kp <= qpos) & (kp > qpos - WINDOW)
        wkg = wk_ref[pl.ds(g, wk, stride=KV_HEADS), :].astype(BF16)
        wvg = wv_ref[pl.ds(g, wk, stride=KV_HEADS), :].astype(BF16)
        s = jnp.where(mask, _dot_nt(q, wkg) * ATTN_SCALE, NEG)
        sn = _dot_nt(q, kwn_ref[...].astype(BF16)) * ATTN_SCALE
        coln = lax.broadcasted_iota(I32, (1, sn.shape[1]), 1)
        sn = jnp.where(coln == 0, sn, NEG)
        m = jnp.maximum(jnp.max(s, -1, keepdims=True), jnp.max(sn, -1, keepdims=True))
        e, en = jnp.exp(s - m), jnp.exp(sn - m)
        den = jnp.sum(e, -1, keepdims=True) + jnp.sum(en, -1, keepdims=True)
        ow_ref[...] = (_dot(e.astype(BF16), wvg) + _dot(en.astype(BF16), vwn_ref[...].astype(BF16))) / den
        m_ref[...] = jnp.full(m_ref.shape, NEG, F32)
        l_ref[...] = jnp.zeros(l_ref.shape, F32)
        acc_ref[...] = jnp.zeros(acc_ref.shape, F32)

    t = (b * KV_HEADS + g) * SEL_TOP + kk
    kp = idx_ref[t] * SEL_BLOCK + lax.broadcasted_iota(I32, (1, SEL_BLOCK), 1)
    limit = jnp.where((ok_ref[t] == 1) & (idx_ref[t] < n_past_blk), qpos, -1)
    kb = kb_ref[pl.ds(g, SEL_BLOCK, stride=KV_HEADS), :].astype(BF16)
    vb = vb_ref[pl.ds(g, SEL_BLOCK, stride=KV_HEADS), :].astype(BF16)
    s = jnp.where(kp <= limit, _dot_nt(q, kb) * ATTN_SCALE, NEG)
    update(s, vb)

    @pl.when(kk == SEL_TOP - 1)
    def _():
        sn = _dot_nt(q, kn_ref[...].astype(BF16)) * ATTN_SCALE
        coln = lax.broadcasted_iota(I32, (1, sn.shape[1]), 1)
        sn = jnp.where(coln < new_ref[b * KV_HEADS + g], sn, NEG)
        update(sn, vn_ref[...].astype(BF16))
        gt = g_ref[...]
        o_ref[...] = gt[:, 0:1] * oc_ref[...] + gt[:, 1:2] * (acc_ref[...] / l_ref[...]) + gt[:, 2:3] * ow_ref[...]


def _attn_s2(qr, pool_k, pool_v, kn, vn, win_k, win_v, kwn, vwn, o_cmp, gates, phys, idx, ok, has_new, *,
             qpos, past_len):
    nb, _, rows, _ = qr.shape
    wk = win_k.shape[0] // (nb * KV_HEADS)
    hspec = pl.BlockSpec((None, None, rows, HEAD_DIM), lambda b, g, k, *_: (b, g, 0, 0))
    bspec = pl.BlockSpec((SEL_BLOCK * KV_HEADS, HEAD_DIM),
                         lambda b, g, k, ph, *_: (ph[(b * KV_HEADS + g) * SEL_TOP + k], 0))
    wspec = pl.BlockSpec((wk * KV_HEADS, HEAD_DIM), lambda b, g, k, *_: (b, 0))
    body = functools.partial(_attn_s2_body, qpos=qpos, n_past_blk=past_len // SEL_BLOCK, past_len=past_len)
    return pl.pallas_call(
        body,
        grid_spec=pltpu.PrefetchScalarGridSpec(
            num_scalar_prefetch=4, grid=(nb, KV_HEADS, SEL_TOP),
            in_specs=[hspec, bspec, bspec, hspec, hspec, wspec, wspec, hspec, hspec, hspec, hspec],
            out_specs=hspec,
            scratch_shapes=[pltpu.VMEM((rows, 1), F32), pltpu.VMEM((rows, 1), F32), pltpu.VMEM((rows, HEAD_DIM), F32),
                            pltpu.VMEM((rows, HEAD_DIM), F32)]),
        out_shape=jax.ShapeDtypeStruct((nb, KV_HEADS, rows, HEAD_DIM), F32),
        compiler_params=_cparams("arbitrary", "arbitrary", "arbitrary"), name="attn_sample_sel_win",
    )(phys, idx, ok, has_new, qr, pool_k, pool_v, kn, vn, win_k, win_v, kwn, vwn, o_cmp, gates)


def _head_rows(a, rows=8):
    nb = a.shape[0]
    p = a.shape[1] // (KV_HEADS * HEAD_DIM)
    a = a.reshape(nb, KV_HEADS, p, HEAD_DIM)
    return jnp.pad(a, ((0, 0), (0, 0), (0, rows - p), (0, 0)))


def kernel(x_prompt, x_sample, cache_cmp_k, cache_cmp_v, cache_sel_k, cache_sel_v, state_win_k, state_win_v, page_table, norm_mix, norm_ffn, norm_kv, norm_final, gm_w_in, gm_ln_g, gm_ln_b, gm_w_s, gm_b_s, gm_w_out, nsa_w_in, nsa_w_out, kv_w, cmp_pe_k, cmp_w1_k, cmp_w2_k, cmp_pe_v, cmp_w1_v, cmp_w2_v, ff_w_gate, ff_w_up, ff_w_down, moe_router, moe_w_gate, moe_w_up, moe_w_down):
    B, S, D = x_prompt.shape
    DB, DS, _ = x_sample.shape
    assert DS == 1 and D == D_MODEL and S % KEY_TILE == 0
    n_prompt, n_sample = B * S, DB * DS
    n_tok = n_prompt + n_sample
    M = n_prompt + GM_CHUNK
    assert M % ROW_TILE == 0 and M % ROW_TILE_DOWN == 0 and n_sample <= 8
    n_pool = cache_cmp_k.shape[0]
    past_len = page_table.shape[1] * PAGE_SIZE
    qpos_s = past_len
    wk = state_win_k.shape[1]
    assert wk == min(WINDOW, past_len)

    x = jnp.concatenate([x_prompt.reshape(n_prompt, D), x_sample.reshape(n_sample, D),
                         jnp.zeros((M - n_tok, D), F32)], 0)

    pos = jnp.concatenate([jnp.tile(jnp.arange(S), B), jnp.full((n_sample,), past_len), jnp.zeros((M - n_tok,), I32)])
    half = ROT_DIM // 2
    inv = ROPE_THETA ** (-jnp.arange(half, dtype=F32) / half)
    ang = pos.astype(F32)[:, None] * inv[None, :]
    cos, sin = jnp.cos(ang), jnp.sin(ang)
    zeros_h = jnp.zeros((M, half), F32)
    rest = HEAD_DIM - ROT_DIM
    rope_c = jnp.concatenate([cos, cos, jnp.ones((M, rest), F32)], 1)
    rope_s1 = jnp.concatenate([-sin, zeros_h, jnp.zeros((M, rest), F32)], 1)
    rope_s2 = jnp.concatenate([zeros_h, sin, jnp.zeros((M, rest), F32)], 1)
    rope_extras = [(t, (ROW_TILE, HEAD_DIM), lambda j, i: (i, 0)) for t in (rope_c, rope_s1, rope_s2)]

    def residual_mm(xin, w, layer, res, tm, name):
        n = w.shape[2]
        return _ws_matmul(xin, [(w, layer)], _ep_residual, [(n, F32)], tm=tm, tn=COL_TILE, n_cols=n,
                          extras=[(res, (tm, COL_TILE), lambda j, i: (i, j))], name=name)[0]

    gm_v = []
    kv = None
    for l in range(DEPTH):
        h = _rmsnorm(x, norm_mix[l], BF16, tm=ROW_TILE)
        if l < N_A_LAYERS:
            z = _ws_matmul(h, [(gm_w_in, l)], _ep_gelu, [(2 * GM_DIM, BF16)], tm=ROW_TILE, tn=COL_TILE,
                           n_cols=2 * GM_DIM, name="gmlp_in")[0]
            o, v_s = _gm_gate(z, gm_ln_g[l], gm_ln_b[l], gm_w_s[l], gm_b_s[l], n_prompt=n_prompt, n_sample=n_sample)
            gm_v.append(v_s[:n_sample].reshape(DB, DS, GM_DIM))
            x = residual_mm(o, gm_w_out, l, x, ROW_TILE, "gmlp_out")
        else:
            bl = l - N_A_LAYERS
            qc, qr = _ws_matmul(h, [(nsa_w_in, bl)], _ep_q, [(N_HEADS * HEAD_DIM, BF16)] * 2, tm=ROW_TILE,
                                tn=COL_TILE, n_cols=N_HEADS * HEAD_DIM, extras=rope_extras, name="nsa_q")
            wg = nsa_w_in[bl, :, N_HEADS * HEAD_DIM:].reshape(D, N_BRANCH, KV_HEADS, Q_PER_KV)
            wg = jnp.transpose(wg, (0, 2, 1, 3)).reshape(D, KV_HEADS, N_BRANCH * Q_PER_KV)
            wg = jnp.pad(wg, ((0, 0), (0, 0), (0, LANES - N_BRANCH * Q_PER_KV))).reshape(1, D, KV_HEADS * LANES)
            gates = _ws_matmul(h, [(wg, 0)], _ep_sigmoid, [(KV_HEADS * LANES, F32)], tm=ROW_TILE, tn=KV_HEADS * LANES,
                               n_cols=KV_HEADS * LANES, name="nsa_gates")[0]
            o = _attn_prompt(qc, qr, kv['kc_p'], kv['vc_p'], kv['rb'], gates, batch=B, seq=S)
            qc_s = _head_rows(qc[n_prompt:n_tok])
            qr_s = _head_rows(qr[n_prompt:n_tok])
            oc_s, idx_s, val_s = _attn_s1(qc_s, kv['kc_s'], kv['vc_s'], qpos=qpos_s)
            idx = idx_s[:, :, 0, :SEL_TOP]
            ok = val_s[:, :, 0, :SEL_TOP] >= 0.0
            n_past_blk = past_len // SEL_BLOCK
            ppb = PAGE_SIZE // SEL_BLOCK
            ip = jnp.minimum(idx, n_past_blk - 1)
            phys = jnp.take_along_axis(page_table, (ip // ppb).reshape(DB, -1), 1).reshape(ip.shape) * ppb + ip % ppb
            has_new = jnp.any((idx >= n_past_blk) & ok, -1)
            g_s = gates[n_prompt:n_tok].reshape(DB, KV_HEADS, LANES)[:, :, :N_BRANCH * Q_PER_KV]
            g_s = jnp.transpose(g_s.reshape(DB, KV_HEADS, N_BRANCH, Q_PER_KV), (0, 1, 3, 2))
            g_s = jnp.pad(g_s, ((0, 0), (0, 0), (0, 8 - Q_PER_KV), (0, HEAD_DIM - N_BRANCH)))
            o_s = _attn_s2(qr_s, kv['pool_k'], kv['pool_v'], kv['ks_new'], kv['vs_new'], kv['win_k'], kv['win_v'],
                           kv['kw_new'], kv['vw_new'], oc_s, g_s, phys.reshape(-1).astype(I32),
                           idx.reshape(-1).astype(I32), ok.reshape(-1).astype(I32), has_new.reshape(-1).astype(I32),
                           qpos=qpos_s, past_len=past_len)
            o_s = o_s[:, :, :Q_PER_KV].reshape(n_sample, N_HEADS * HEAD_DIM).astype(BF16)
            o = jnp.concatenate([o, o_s, jnp.zeros((M - n_tok, N_HEADS * HEAD_DIM), BF16)], 0)
            x = residual_mm(o, nsa_w_out, bl, x, ROW_TILE, "nsa_out")

        if l % 2 == 0:
            h = _rmsnorm(x, norm_ffn[l], BF16, tm=ROW_TILE)
            hh = _ws_matmul(h, [(ff_w_gate, l // 2), (ff_w_up, l // 2)], _ep_swiglu, [(ff_w_gate.shape[2], BF16)],
                            tm=ROW_TILE, tn=COL_TILE, n_cols=ff_w_gate.shape[2], name="ffn_up")[0]
            x = residual_mm(hh, ff_w_down, l // 2, x, ROW_TILE_DOWN, "ffn_down")
        else:
            x = _moe(x, norm_ffn[l], moe_router[l // 2], moe_w_gate, moe_w_up, moe_w_down, l // 2, n_tok=n_tok)

        if l == N_A_LAYERS - 1:
            hkv = _rmsnorm(x, norm_kv, BF16, tm=ROW_TILE)
            r, rb = _ws_matmul(hkv, [(kv_w.reshape(1, D, 6 * KV_SEC), 0)], _ep_kv,
                               [(6 * KV_SEC, F32), (6 * KV_SEC, BF16)], tm=ROW_TILE, tn=KV_SEC,
                               n_cols=6 * KV_SEC, extras=rope_extras, name="kv_proj")
            pages_p = r.reshape(M // PAGE_SIZE, PAGE_SIZE, 6 * KV_SEC)
            table_p = jnp.arange(n_prompt // PAGE_SIZE, dtype=I32).reshape(B, S // PAGE_SIZE)
            r_s = r[n_prompt:n_tok]
            sec_s = lambda i: r_s[:, i * KV_SEC:(i + 1) * KV_SEC]
            rows2d = lambda c: c.reshape(-1, HEAD_DIM)
            kv = dict(
                rb=rb,
                kc_p=_compress(pages_p, table_p, 0, cmp_pe_k, cmp_w1_k, cmp_w2_k),
                vc_p=_compress(pages_p, table_p, 1, cmp_pe_v, cmp_w1_v, cmp_w2_v),
                kc_s=_compress(rows2d(cache_cmp_k), page_table, 0, cmp_pe_k, cmp_w1_k, cmp_w2_k),
                vc_s=_compress(rows2d(cache_cmp_v), page_table, 0, cmp_pe_v, cmp_w1_v, cmp_w2_v),
                pool_k=rows2d(cache_sel_k), pool_v=rows2d(cache_sel_v),
                ks_new=_head_rows(sec_s(2)), vs_new=_head_rows(sec_s(3)),
                kw_new=_head_rows(sec_s(4)), vw_new=_head_rows(sec_s(5)),
                win_k=rows2d(state_win_k), win_v=rows2d(state_win_v),
            )
            r_p = r[:n_prompt]
            pg = lambda i: r_p[:, i * KV_SEC:(i + 1) * KV_SEC].reshape(B, S // PAGE_SIZE, PAGE_SIZE, KV_HEADS, HEAD_DIM)
            wkp = min(WINDOW, S)
            tail = lambda i: r_p[:, i * KV_SEC:(i + 1) * KV_SEC].reshape(B, S, KV_HEADS, HEAD_DIM)[:, S - wkp:]
            row_s = lambda i: sec_s(i).reshape(DB, DS, KV_HEADS, HEAD_DIM)
            state_p = (pg(0), pg(1), pg(2), pg(3), tail(4), tail(5))
            state_s = (row_s(0), row_s(1), row_s(2), row_s(3),
                       jnp.concatenate([state_win_k, row_s(4)], 1)[:, DS:],
                       jnp.concatenate([state_win_v, row_s(5)], 1)[:, DS:])

    y = _rmsnorm(x, norm_final, F32, tm=ROW_TILE)
    y_prompt = y[:n_prompt].reshape(B, S, D)
    y_sample = y[n_prompt:n_tok].reshape(DB, DS, D)
    return (y_prompt, y_sample) + state_p + state_s + (jnp.stack(gm_v, 0),)
```

```python
import functools

import numpy as np
import jax
import jax.numpy as jnp
from jax import lax
from jax.experimental import pallas as pl
from jax.experimental.pallas import tpu as pltpu

F32 = jnp.float32
BF16 = jnp.bfloat16
I32 = jnp.int32

D_MODEL = 2048
DEPTH = 4
PAGE_SIZE = 128
N_A_LAYERS = DEPTH // 2
GM_CHUNK = 128
GM_DIM = D_MODEL
GM_GROUPS = 16
GM_GW = GM_DIM // GM_GROUPS
N_HEADS = 16
HEAD_DIM = 128
KV_HEADS = 4
Q_PER_KV = N_HEADS // KV_HEADS
ROT_DIM = HEAD_DIM // 4
ROPE_THETA = 500000.0
CMP_LEN = 32
CMP_STRIDE = 16
CMP_R = CMP_LEN // CMP_STRIDE
SEL_BLOCK = 64
SEL_TOP = 16
WINDOW = 512
N_BRANCH = 3
KV_SEC = KV_HEADS * HEAD_DIM
ATTN_SCALE = HEAD_DIM ** -0.5
N_EXPERTS = 8
TOP_K = 2
EPS = 1e-6
NEG = -1e30
FORCE_SCORE = 1e6

LANES = 128
VMEM_LIMIT_BYTES = 56 * 1024 * 1024
ROW_TILE = 832
ROW_TILE_DOWN = 416
COL_TILE = 512
MOE_UP_COL_TILE = 1024
MOE_BLOCK = 256
KEY_TILE = 256
Q_TILE = 128
PAGES_PER_STEP = 8


def _cparams(*sem):
    return pltpu.CompilerParams(dimension_semantics=sem, vmem_limit_bytes=VMEM_LIMIT_BYTES)


def _dot(a, b):
    return jnp.dot(a, b, preferred_element_type=F32)


def _dot_nt(a, b):
    return lax.dot_general(a, b, (((1,), (1,)), ((), ())), preferred_element_type=F32)


def _dot3(a, b):
    ah = a.astype(BF16)
    al = (a - ah.astype(F32)).astype(BF16)
    bh = b.astype(BF16)
    bl = (b - bh.astype(F32)).astype(BF16)
    return _dot(ah, bh) + _dot(al, bh) + _dot(ah, bl)


def _ws_body(*refs, nw, n_extra, n_out, epilogue):
    x_ref = refs[0]
    w_refs = refs[1:1 + nw]
    e_refs = refs[1 + nw:1 + nw + n_extra]
    o_refs = refs[1 + nw + n_extra:1 + nw + n_extra + n_out]
    wb_refs = refs[1 + nw + n_extra + n_out:]
    j = pl.program_id(0)

    @pl.when(pl.program_id(1) == 0)
    def _():
        for w_ref, wb in zip(w_refs, wb_refs):
            wb[...] = w_ref[...].astype(BF16)

    accs = [_dot(x_ref[...], wb[...]) for wb in wb_refs]
    outs = epilogue(j, accs, e_refs)
    for o_ref, o in zip(o_refs, outs):
        o_ref[...] = o.astype(o_ref.dtype)


def _ws_matmul(x, ws, epilogue, outs, *, tm, tn, n_cols, extras=(), name):
    M, K = x.shape
    nj = n_cols // tn
    ni = M // tm
    assert nj * tn == n_cols and ni * tm == M
    in_specs = [pl.BlockSpec((tm, K), lambda j, i: (i, 0))]
    in_specs += [pl.BlockSpec((None, K, tn), lambda j, i, layer=layer: (layer, 0, j)) for _, layer in ws]
    ws = [w for w, _ in ws]
    in_specs += [pl.BlockSpec(bs, im) for _, bs, im in extras]
    out_shape = [jax.ShapeDtypeStruct((M, c), dt) for c, dt in outs]
    out_specs = [pl.BlockSpec((tm, c // nj), lambda j, i: (i, j)) for c, _ in outs]
    body = functools.partial(_ws_body, nw=len(ws), n_extra=len(extras), n_out=len(outs), epilogue=epilogue)
    res = pl.pallas_call(
        body, grid=(nj, ni), in_specs=in_specs, out_specs=out_specs, out_shape=out_shape,
        scratch_shapes=[pltpu.VMEM((K, tn), BF16) for _ in ws],
        compiler_params=_cparams("arbitrary", "arbitrary"), name=name,
    )(x, *ws, *[a for a, _, _ in extras])
    return res


def _rope_tile(acc, c_ref, s1_ref, s2_ref):
    c, s1, s2 = c_ref[...], s1_ref[...], s2_ref[...]
    parts = []
    for h in range(acc.shape[1] // HEAD_DIM):
        xh = acc[:, h * HEAD_DIM:(h + 1) * HEAD_DIM]
        parts.append(xh * c + pltpu.roll(xh, HEAD_DIM - ROT_DIM // 2, 1) * s1 + pltpu.roll(xh, ROT_DIM // 2, 1) * s2)
    return jnp.concatenate(parts, 1)


def _ep_gelu(j, accs, e):
    return [jax.nn.gelu(accs[0])]


def _ep_residual(j, accs, e):
    return [e[0][...] + accs[0]]


def _ep_swiglu(j, accs, e):
    return [jax.nn.silu(accs[0]) * accs[1]]


def _ep_sigmoid(j, accs, e):
    return [jax.nn.sigmoid(accs[0])]


def _ep_kv(j, accs, e):
    acc = accs[0]
    roped = _rope_tile(acc, *e)
    r = jnp.where((j == 2) | (j == 4), roped, acc)
    return [r, r]


def _ep_q(j, accs, e):
    acc = accs[0]
    return [acc, _rope_tile(acc, *e)]


def _rms(x, g):
    return (x * lax.rsqrt(jnp.mean(x * x, -1, keepdims=True) + EPS)) * g


def _rms_body(x_ref, g_ref, o_ref):
    o_ref[...] = _rms(x_ref[...], g_ref[...]).astype(o_ref.dtype)


def _rmsnorm(x, g, out_dtype, *, tm):
    M, D = x.shape
    return pl.pallas_call(
        _rms_body, grid=(M // tm,),
        in_specs=[pl.BlockSpec((tm, D), lambda i: (i, 0)), pl.BlockSpec((1, D), lambda i: (0, 0))],
        out_specs=pl.BlockSpec((tm, D), lambda i: (i, 0)),
        out_shape=jax.ShapeDtypeStruct((M, D), out_dtype),
        compiler_params=_cparams("arbitrary"), name="rmsnorm",
    )(x, g.reshape(1, D))


def _rms_router_body(x_ref, g_ref, r_ref, h_ref, te_ref, tg_ref):
    h = _rms(x_ref[...], g_ref[...])
    h_ref[...] = h
    lane = lax.broadcasted_iota(I32, (1, LANES), 1).astype(F32)
    logits = jnp.where(lane < N_EXPERTS, _dot3(h, r_ref[...]), -jnp.inf)
    m1 = jnp.max(logits, -1, keepdims=True)
    i1 = jnp.min(jnp.where(logits == m1, lane, float(LANES)), -1, keepdims=True)
    rest = jnp.where(lane == i1, -jnp.inf, logits)
    m2 = jnp.max(rest, -1, keepdims=True)
    i2 = jnp.min(jnp.where(rest == m2, lane, float(LANES)), -1, keepdims=True)
    e2 = jnp.exp(m2 - m1)
    den = 1.0 + e2
    te_ref[...] = jnp.where(lane == 0, i1, jnp.where(lane == 1, i2, 0.0)).astype(I32)
    tg_ref[...] = jnp.where(lane == 0, 1.0 / den, jnp.where(lane == 1, e2 / den, 0.0))


def _rmsnorm_router(x, g, router, *, tm):
    M, D = x.shape
    rp = jnp.pad(router, ((0, 0), (0, LANES - N_EXPERTS)))
    return pl.pallas_call(
        _rms_router_body, grid=(M // tm,),
        in_specs=[pl.BlockSpec((tm, D), lambda i: (i, 0)), pl.BlockSpec((1, D), lambda i: (0, 0)),
                  pl.BlockSpec((D, LANES), lambda i: (0, 0))],
        out_specs=[pl.BlockSpec((tm, D), lambda i: (i, 0)), pl.BlockSpec((tm, LANES), lambda i: (i, 0)),
                   pl.BlockSpec((tm, LANES), lambda i: (i, 0))],
        out_shape=[jax.ShapeDtypeStruct((M, D), F32), jax.ShapeDtypeStruct((M, LANES), I32),
                   jax.ShapeDtypeStruct((M, LANES), F32)],
        compiler_params=_cparams("arbitrary"), name="rmsnorm_router",
    )(x, g.reshape(1, D), rp)


def _gm_gate_body(z_ref, lng_ref, lnb_ref, ws_ref, bst_ref, w00_ref, b00_ref, o_ref, vs_ref, *,
                  n_prompt_chunks, n_sample):
    c = pl.program_id(0)
    z = z_ref[...]
    u = z[:, :GM_DIM].astype(F32)
    v = z[:, GM_DIM:].astype(F32)
    mu = jnp.mean(v, -1, keepdims=True)
    vc = v - mu
    vn = (vc * lax.rsqrt(jnp.mean(vc * vc, -1, keepdims=True) + EPS)) * lng_ref[...] + lnb_ref[...]

    @pl.when(c < n_prompt_chunks)
    def _():
        vb = vn.astype(BF16)
        tri = lax.broadcasted_iota(I32, (GM_CHUNK, GM_CHUNK), 0) >= lax.broadcasted_iota(I32, (GM_CHUNK, GM_CHUNK), 1)
        for g in range(GM_GROUPS):
            sl = slice(g * GM_GW, (g + 1) * GM_GW)
            wg = jnp.where(tri, ws_ref[g], 0.0).astype(BF16)
            s = _dot(wg, vb[:, sl]) + bst_ref[:, g:g + 1]
            o_ref[:, sl] = (u[:, sl] * s).astype(o_ref.dtype)

    @pl.when(c >= n_prompt_chunks)
    def _():
        row = lax.broadcasted_iota(I32, (GM_CHUNK, 1), 0)
        s = w00_ref[...] * vn + b00_ref[...]
        o_ref[...] = jnp.where(row < n_sample, u * s, 0.0).astype(o_ref.dtype)
        vs_ref[...] = vn[:vs_ref.shape[0]]


def _gm_gate(z, ln_g, ln_b, w_s, b_s, *, n_prompt, n_sample):
    M = z.shape[0]
    n_pc = n_prompt // GM_CHUNK
    assert n_prompt % GM_CHUNK == 0 and M == (n_pc + 1) * GM_CHUNK and n_sample <= GM_CHUNK
    n_vs = -(-n_sample // 8) * 8
    w00 = jnp.repeat(w_s[:, 0, 0], GM_GW).reshape(1, GM_DIM)
    b00 = jnp.repeat(b_s[:, 0], GM_GW).reshape(1, GM_DIM)
    body = functools.partial(_gm_gate_body, n_prompt_chunks=n_pc, n_sample=n_sample)
    full = lambda *shape: pl.BlockSpec(shape, lambda c: (0,) * len(shape))
    return pl.pallas_call(
        body, grid=(n_pc + 1,),
        in_specs=[pl.BlockSpec((GM_CHUNK, 2 * GM_DIM), lambda c: (c, 0)), full(1, GM_DIM), full(1, GM_DIM),
                  full(GM_GROUPS, GM_CHUNK, GM_CHUNK), full(GM_CHUNK, GM_GROUPS), full(1, GM_DIM), full(1, GM_DIM)],
        out_specs=[pl.BlockSpec((GM_CHUNK, GM_DIM), lambda c: (c, 0)), full(n_vs, GM_DIM)],
        out_shape=[jax.ShapeDtypeStruct((M, GM_DIM), BF16), jax.ShapeDtypeStruct((n_vs, GM_DIM), F32)],
        compiler_params=_cparams("arbitrary"), name="gmlp_gate",
    )(z, ln_g.reshape(1, GM_DIM), ln_b.reshape(1, GM_DIM), w_s, b_s.T, w00, b00)


def _moe_gather_body(rt_ref, nu_ref, h_hbm, o_ref, buf, sem):
    i = pl.program_id(0)

    def row_copy(tok, r):
        return pltpu.make_async_copy(h_hbm.at[pl.ds(tok, 1), :], buf.at[pl.ds(r, 1), :], sem)

    @pl.when(i < nu_ref[0])
    def _():
        def issue(r, c):
            row_copy(rt_ref[i * MOE_BLOCK + r], r).start()
            return c

        def wait(r, c):
            row_copy(0, r).wait()
            return c

        lax.fori_loop(0, MOE_BLOCK, issue, 0, unroll=8)
        lax.fori_loop(0, MOE_BLOCK, wait, 0, unroll=8)
        o_ref[...] = buf[...].astype(o_ref.dtype)

    @pl.when(i >= nu_ref[0])
    def _():
        o_ref[...] = jnp.zeros(o_ref.shape, o_ref.dtype)


def _moe_gather(h, row_tok, n_used, n_blk):
    D = h.shape[1]
    return pl.pallas_call(
        _moe_gather_body,
        grid_spec=pltpu.PrefetchScalarGridSpec(
            num_scalar_prefetch=2, grid=(n_blk,),
            in_specs=[pl.BlockSpec(memory_space=pl.ANY)],
            out_specs=pl.BlockSpec((MOE_BLOCK, D), lambda i, rt, nu: (i, 0)),
            scratch_shapes=[pltpu.VMEM((MOE_BLOCK, D), F32), pltpu.SemaphoreType.DMA(())]),
        out_shape=jax.ShapeDtypeStruct((n_blk * MOE_BLOCK, D), BF16),
        compiler_params=_cparams("arbitrary"), name="moe_gather",
    )(row_tok, n_used, h)


def _moe_mm_body(be_ref, bf_ref, nu_ref, *refs, nw, epilogue):
    x_ref = refs[0]
    w_refs = refs[1:1 + nw]
    o_ref = refs[1 + nw]
    wb_refs = refs[2 + nw:]
    i = pl.program_id(1)
    live = i < nu_ref[0]

    @pl.when(live & (bf_ref[i] == 1))
    def _():
        for w_ref, wb in zip(w_refs, wb_refs):
            wb[...] = w_ref[...].astype(BF16)

    @pl.when(live)
    def _():
        accs = [_dot(x_ref[...], wb[...]) for wb in wb_refs]
        o_ref[...] = epilogue(accs).astype(o_ref.dtype)

    @pl.when(jnp.logical_not(live))
    def _():
        o_ref[...] = jnp.zeros(o_ref.shape, o_ref.dtype)


def _moe_matmul(x, ws, layer, blk_e, blk_first, n_used, epilogue, out_dtype, *, tn, name):
    M, K = x.shape
    N = ws[0].shape[3]
    nj, ni = N // tn, M // MOE_BLOCK
    assert nj * tn == N
    clamp = lambda i, nu: jnp.minimum(i, nu[0] - 1)
    in_specs = [pl.BlockSpec((MOE_BLOCK, K), lambda j, i, be, bf, nu: (clamp(i, nu), 0))]
    in_specs += [pl.BlockSpec((None, None, K, tn), lambda j, i, be, bf, nu: (layer, be[clamp(i, nu)], 0, j))
                 for _ in ws]
    body = functools.partial(_moe_mm_body, nw=len(ws), epilogue=epilogue)
    return pl.pallas_call(
        body,
        grid_spec=pltpu.PrefetchScalarGridSpec(
            num_scalar_prefetch=3, grid=(nj, ni), in_specs=in_specs,
            out_specs=pl.BlockSpec((MOE_BLOCK, tn), lambda j, i, be, bf, nu: (i, j)),
            scratch_shapes=[pltpu.VMEM((K, tn), BF16) for _ in ws]),
        out_shape=jax.ShapeDtypeStruct((M, N), out_dtype),
        compiler_params=_cparams("arbitrary", "arbitrary"), name=name,
    )(blk_e, blk_first, n_used, x, *ws)


def _moe_combine_body(d0_ref, d1_ref, x_ref, g_ref, yb_hbm, o_ref, b0, b1, sem, *, tm):
    i = pl.program_id(0)

    def row_copy(src_row, buf, r):
        return pltpu.make_async_copy(yb_hbm.at[pl.ds(src_row, 1), :], buf.at[pl.ds(r, 1), :], sem)

    def issue(r, c):
        row_copy(d0_ref[i * tm + r], b0, r).start()
        row_copy(d1_ref[i * tm + r], b1, r).start()
        return c

    def wait(r, c):
        row_copy(0, b0, r).wait()
        row_copy(0, b1, r).wait()
        return c

    lax.fori_loop(0, tm, issue, 0, unroll=8)
    lax.fori_loop(0, tm, wait, 0, unroll=8)
    g = g_ref[...]
    o_ref[...] = x_ref[...] + (g[:, 0:1] * b0[...] + g[:, 1:2] * b1[...])


def _moe_combine(x, gates, yb, d0, d1, *, tm):
    M, D = x.shape
    body = functools.partial(_moe_combine_body, tm=tm)
    return pl.pallas_call(
        body,
        grid_spec=pltpu.PrefetchScalarGridSpec(
            num_scalar_prefetch=2, grid=(M // tm,),
            in_specs=[pl.BlockSpec((tm, D), lambda i, a, b: (i, 0)), pl.BlockSpec((tm, LANES), lambda i, a, b: (i, 0)),
                      pl.BlockSpec(memory_space=pl.ANY)],
            out_specs=pl.BlockSpec((tm, D), lambda i, a, b: (i, 0)),
            scratch_shapes=[pltpu.VMEM((tm, D), F32), pltpu.VMEM((tm, D), F32), pltpu.SemaphoreType.DMA(())]),
        out_shape=jax.ShapeDtypeStruct((M, D), F32),
        compiler_params=_cparams("arbitrary"), name="moe_combine",
    )(d0, d1, x, gates, yb)


def _moe(x, g_norm, router, wg, wu, wd, layer, *, n_tok):
    M, D = x.shape
    h, te, tg = _rmsnorm_router(x, g_norm, router, tm=ROW_TILE)
    a = n_tok * TOP_K
    n_blk = -(-a // MOE_BLOCK) + N_EXPERTS
    e_flat = te[:n_tok, :TOP_K].reshape(a)
    onehot = (e_flat[:, None] == jnp.arange(N_EXPERTS, dtype=I32)[None, :]).astype(I32)
    csum = jnp.cumsum(onehot, 0)
    pos = jnp.take_along_axis(csum, e_flat[:, None], 1)[:, 0] - 1
    counts = csum[-1]
    padded = (counts + MOE_BLOCK - 1) // MOE_BLOCK * MOE_BLOCK
    end_pad = jnp.cumsum(padded)
    start_pad = end_pad - padded
    dest = (start_pad[e_flat] + pos).astype(I32)
    tok = jnp.arange(a, dtype=I32) // TOP_K
    row_tok = jnp.full((n_blk * MOE_BLOCK,), n_tok, I32).at[dest].set(tok)
    n_used = (end_pad[-1] // MOE_BLOCK).astype(I32).reshape(1)
    blk_e = jnp.minimum(jnp.searchsorted(end_pad, jnp.arange(n_blk, dtype=I32) * MOE_BLOCK, side='right'),
                        N_EXPERTS - 1).astype(I32)
    blk_first = jnp.concatenate([jnp.ones((1,), I32), (blk_e[1:] != blk_e[:-1]).astype(I32)])
    d = jnp.pad(dest.reshape(n_tok, TOP_K), ((0, M - n_tok), (0, 0)))
    row = jnp.arange(M, dtype=I32)[:, None]
    gates = jnp.where(row < n_tok, tg, 0.0)

    xb = _moe_gather(h, row_tok, n_used, n_blk)
    hb = _moe_matmul(xb, [wg, wu], layer, blk_e, blk_first, n_used, lambda accs: jax.nn.silu(accs[0]) * accs[1],
                     BF16, tn=MOE_UP_COL_TILE, name="moe_up")
    yb = _moe_matmul(hb, [wd], layer, blk_e, blk_first, n_used, lambda accs: accs[0], F32, tn=COL_TILE,
                     name="moe_down")
    return _moe_combine(x, gates, yb, d[:, 0], d[:, 1], tm=ROW_TILE_DOWN)


def _cmp_z_body(pt_ref, *refs, heads_interleaved):
    n_in = len(refs) - 2
    page_refs = refs[:n_in]
    w_ref = refs[n_in]
    o_ref = refs[n_in + 1]
    n_ch = PAGE_SIZE // CMP_STRIDE
    acc = jnp.zeros((KV_HEADS * PAGES_PER_STEP * n_ch, CMP_R * HEAD_DIM), F32)
    for s in range(CMP_STRIDE):
        if heads_interleaved:
            rows = [pg[pl.ds(KV_HEADS * s + k, n_ch, stride=KV_HEADS * CMP_STRIDE), :]
                    for k in range(KV_HEADS) for pg in page_refs]
        else:
            rows = [pg[pl.ds(s, n_ch, stride=CMP_STRIDE), :] for pg in page_refs]
        acc = acc + _dot(jnp.concatenate(rows, 0).astype(BF16), w_ref[s])
    m = PAGES_PER_STEP * n_ch
    for k in range(KV_HEADS):
        o_ref[k] = acc[k * m:(k + 1) * m]


def _cmp_z(pages, table, sec, w1cat):
    nb, npg = table.shape
    assert npg % PAGES_PER_STEP == 0
    n_ch = PAGE_SIZE // CMP_STRIDE
    m = PAGES_PER_STEP * n_ch
    interleaved = pages.ndim == 2
    if interleaved:
        specs = [pl.BlockSpec((PAGE_SIZE * KV_HEADS, HEAD_DIM),
                              lambda b, g, pt, q=q: (pt[b, g * PAGES_PER_STEP + q], 0)) for q in range(PAGES_PER_STEP)]
    else:
        specs = [pl.BlockSpec((None, PAGE_SIZE, HEAD_DIM),
                              lambda b, g, pt, k=k, q=q: (pt[b, g * PAGES_PER_STEP + q], 0, sec * KV_HEADS + k))
                 for k in range(KV_HEADS) for q in range(PAGES_PER_STEP)]
    return pl.pallas_call(
        functools.partial(_cmp_z_body, heads_interleaved=interleaved),
        grid_spec=pltpu.PrefetchScalarGridSpec(
            num_scalar_prefetch=1, grid=(nb, npg // PAGES_PER_STEP),
            in_specs=specs + [pl.BlockSpec((CMP_STRIDE, HEAD_DIM, CMP_R * HEAD_DIM), lambda b, g, pt: (0, 0, 0))],
            out_specs=pl.BlockSpec((None, KV_HEADS, m, CMP_R * HEAD_DIM), lambda b, g, pt: (b, 0, g, 0))),
        out_shape=jax.ShapeDtypeStruct((nb, KV_HEADS, npg * n_ch, CMP_R * HEAD_DIM), F32),
        compiler_params=_cparams("arbitrary", "arbitrary"), name="cmp_z",
    )(table, *([pages] * len(specs)), w1cat)


def _cmp_fin_body(z_ref, pe_ref, w1_ref, w2_ref, o_ref):
    z = z_ref[...]
    n = z.shape[0]
    pe = jnp.broadcast_to(pe_ref[...], (8, pe_ref.shape[1])).astype(BF16)
    pre = _dot(pe, w1_ref[...].astype(BF16))[0:1]
    pre = pre + z[:, :HEAD_DIM]
    pre = pre + pltpu.roll(z[:, HEAD_DIM:], n - 1, 0)
    o_ref[...] = _dot(jax.nn.gelu(pre).astype(BF16), w2_ref[...].astype(BF16)).astype(o_ref.dtype)


def _cmp_fin(z, pe, w1, w2):
    nb, _, n, _ = z.shape
    return pl.pallas_call(
        _cmp_fin_body, grid=(nb, KV_HEADS),
        in_specs=[pl.BlockSpec((None, None, n, CMP_R * HEAD_DIM), lambda b, k: (b, k, 0, 0)),
                  pl.BlockSpec((1, CMP_LEN * HEAD_DIM), lambda b, k: (0, 0)),
                  pl.BlockSpec((CMP_LEN * HEAD_DIM, HEAD_DIM), lambda b, k: (0, 0)),
                  pl.BlockSpec((HEAD_DIM, HEAD_DIM), lambda b, k: (0, 0))],
        out_specs=pl.BlockSpec((None, None, n, HEAD_DIM), lambda b, k: (b, k, 0, 0)),
        out_shape=jax.ShapeDtypeStruct((nb, KV_HEADS, n, HEAD_DIM), BF16),
        compiler_params=_cparams("arbitrary", "arbitrary"), name="cmp_fin",
    )(z, pe.reshape(1, -1), w1, w2)


def _w1cat(w1):
    w = w1.reshape(CMP_R, CMP_STRIDE, HEAD_DIM, HEAD_DIM)
    return jnp.transpose(w, (1, 2, 0, 3)).reshape(CMP_STRIDE, HEAD_DIM, CMP_R * HEAD_DIM).astype(BF16)


def _compress(pages, table, sec, pe, w1, w2):
    return _cmp_fin(_cmp_z(pages, table, sec, _w1cat(w1)), pe, w1, w2)


def _overlap_matrix(nc, ns, rows, cols):
    cs = np.arange(nc) * CMP_STRIDE
    ss = np.arange(ns) * SEL_BLOCK
    ov = np.minimum(cs[:, None] + CMP_LEN, ss[None, :] + SEL_BLOCK) - np.maximum(cs[:, None], ss[None, :])
    ov = np.clip(ov, 0, None).astype(np.float32) / np.float32(CMP_LEN)
    out = np.zeros((rows, cols), np.float32)
    out[:nc, :ns] = ov
    return jnp.asarray(out)


def _attn_prompt_body(qc_ref, qr_ref, kc_ref, vc_ref, ks_ref, vs_ref, kw_ref, vw_ref, g_ref, ovt_ref, ext_ref, o_ref,
                      allow_ref, vct_ref, vst_ref, vwt_ref, q4_ref, s_ref, ocmp_ref, osel_ref, acc_ref, *,
                      seq, n_cmp, n_sel):
    qi = pl.program_id(2)
    qpos = qi * Q_TILE + lax.broadcasted_iota(I32, (1, Q_TILE), 1)
    row = lax.broadcasted_iota(I32, (LANES, 1), 0)

    @pl.when(qi == 0)
    def _():
        vct_ref[...] = vc_ref[...].astype(F32).T.astype(BF16)
        vst_ref[...] = vs_ref[...].astype(F32).T.astype(BF16)
        vwt_ref[...] = vw_ref[...].astype(F32).T.astype(BF16)

    cmask = (row * CMP_STRIDE + (CMP_LEN - 1) <= qpos) & (row < n_cmp)
    kc = kc_ref[...]
    psum = jnp.zeros((LANES, Q_TILE), F32)
    for p in range(Q_PER_KV):
        s = _dot_nt(kc, qc_ref[:, p * HEAD_DIM:(p + 1) * HEAD_DIM]) * ATTN_SCALE
        sm = jnp.where(cmask, s, NEG)
        e = jnp.exp(sm - jnp.max(sm, 0, keepdims=True))
        pr = jnp.where(cmask, e / jnp.sum(e, 0, keepdims=True), 0.0)
        psum = psum + pr
        ocmp_ref[p] = _dot(vct_ref[...], pr.astype(BF16))

    n_sel_pad = -(-n_sel // 8) * 8
    imp = _dot3(ovt_ref[...], psum)[:n_sel_pad]
    blk = row[:n_sel_pad]
    qblk = qpos >> 6
    valid = (blk * SEL_BLOCK <= qpos) & (blk < n_sel)
    forced = (blk == 0) | (blk == qblk) | (blk == qblk - 1)
    score = jnp.where(forced & valid, FORCE_SCORE, jnp.where(valid, imp, -1.0))
    rank = jnp.zeros(score.shape, F32)
    for j in range(n_sel):
        cj = score[j:j + 1, :]
        beats = (cj > score) | ((cj == score) & (blk > j))
        rank = rank + jnp.where(beats, 1.0, 0.0)
    sel = jnp.where((rank < SEL_TOP) & (score >= 0.0), 1.0, 0.0)
    sel = jnp.concatenate([sel, jnp.zeros((LANES - n_sel_pad, Q_TILE), F32)], 0).astype(BF16)
    selx = _dot(ext_ref[...], sel)
    kpos_all = lax.broadcasted_iota(I32, (seq, 1), 0)
    allow_ref[...] = jnp.where((selx > 0.5) & (kpos_all <= qpos), 1.0, 0.0)

    q4_ref[...] = jnp.concatenate([qr_ref[:, p * HEAD_DIM:(p + 1) * HEAD_DIM] for p in range(Q_PER_KV)], 0)

    def flash(k_ref, vt_ref, lo, hi, mask_fn):
        acc_ref[...] = jnp.zeros(acc_ref.shape, F32)
        last = seq // KEY_TILE - 1

        def scores(kt):
            start = pl.multiple_of(kt * KEY_TILE, KEY_TILE)
            return _dot_nt(k_ref[pl.ds(start, KEY_TILE), :], q4_ref[...])

        s_ref[lo % 2] = scores(lo)

        def body(kt, carry):
            m, l = carry
            start = pl.multiple_of(kt * KEY_TILE, KEY_TILE)
            s = s_ref[kt % 2] * ATTN_SCALE
            s_ref[(kt + 1) % 2] = scores(jnp.minimum(kt + 1, last))
            mask = mask_fn(start)
            s = jnp.where(jnp.concatenate([mask] * Q_PER_KV, 1), s, NEG)
            m_new = jnp.maximum(m, jnp.max(s, 0, keepdims=True))
            alpha = jnp.exp(m - m_new)
            e = jnp.exp(s - m_new)
            l = alpha * l + jnp.sum(e, 0, keepdims=True)
            acc_ref[...] = alpha * acc_ref[...] + _dot(vt_ref[:, pl.ds(start, KEY_TILE)], e.astype(BF16))
            return m_new, l

        init = (jnp.full((1, Q_PER_KV * Q_TILE), NEG, F32), jnp.zeros((1, Q_PER_KV * Q_TILE), F32))
        return lax.fori_loop(lo, hi, body, init)[1]

    hi = (qi * Q_TILE + Q_TILE - 1) // KEY_TILE + 1
    l = flash(ks_ref, vst_ref, 0, hi, lambda start: allow_ref[pl.ds(start, KEY_TILE), :] > 0.5)
    osel_ref[...] = acc_ref[...] / l

    def win_mask(start):
        kp = start + lax.broadcasted_iota(I32, (KEY_TILE, 1), 0)
        return (kp <= qpos) & (kp > qpos - WINDOW)

    l = flash(kw_ref, vwt_ref, jnp.maximum(qi * Q_TILE - (WINDOW - 1), 0) // KEY_TILE, hi, win_mask)
    owin = acc_ref[...] / l

    gt = g_ref[...].T
    for p in range(Q_PER_KV):
        sl = slice(p * Q_TILE, (p + 1) * Q_TILE)
        ot = (gt[p:p + 1] * ocmp_ref[p] + gt[Q_PER_KV + p:Q_PER_KV + p + 1] * osel_ref[:, sl]
              + gt[2 * Q_PER_KV + p:2 * Q_PER_KV + p + 1] * owin[:, sl])
        o_ref[:, p * HEAD_DIM:(p + 1) * HEAD_DIM] = ot.T.astype(o_ref.dtype)


def _attn_prompt(qc, qr, kc, vc, rb, gates, *, batch, seq):
    nq = seq // Q_TILE
    n_chunk = seq // CMP_STRIDE
    n_cmp = n_chunk - CMP_R + 1
    n_sel = seq // SEL_BLOCK
    assert n_chunk == LANES and n_sel <= LANES and seq % KEY_TILE == 0
    ovt = _overlap_matrix(n_cmp, n_sel, LANES, LANES).T
    ext = np.zeros((seq, LANES), np.float32)
    ext[np.arange(seq), np.arange(seq) // SEL_BLOCK] = 1.0
    ext = jnp.asarray(ext, BF16)
    qspec = pl.BlockSpec((Q_TILE, KV_SEC), lambda b, g, q: (b * nq + q, g))
    cspec = pl.BlockSpec((None, None, n_chunk, HEAD_DIM), lambda b, g, q: (b, g, 0, 0))
    kvspec = lambda sec: pl.BlockSpec((seq, HEAD_DIM), lambda b, g, q: (b, sec * KV_HEADS + g))
    body = functools.partial(_attn_prompt_body, seq=seq, n_cmp=n_cmp, n_sel=n_sel)
    return pl.pallas_call(
        body, grid=(batch, KV_HEADS, nq),
        in_specs=[qspec, qspec, cspec, cspec, kvspec(2), kvspec(3), kvspec(4), kvspec(5),
                  pl.BlockSpec((Q_TILE, LANES), lambda b, g, q: (b * nq + q, g)),
                  pl.BlockSpec((LANES, LANES), lambda b, g, q: (0, 0)),
                  pl.BlockSpec((seq, LANES), lambda b, g, q: (0, 0))],
        out_specs=qspec,
        out_shape=jax.ShapeDtypeStruct((batch * seq, N_HEADS * HEAD_DIM), BF16),
        scratch_shapes=[pltpu.VMEM((seq, Q_TILE), F32), pltpu.VMEM((HEAD_DIM, n_chunk), BF16),
                        pltpu.VMEM((HEAD_DIM, seq), BF16), pltpu.VMEM((HEAD_DIM, seq), BF16),
                        pltpu.VMEM((Q_PER_KV * Q_TILE, HEAD_DIM), BF16),
                        pltpu.VMEM((2, KEY_TILE, Q_PER_KV * Q_TILE), F32),
                        pltpu.VMEM((Q_PER_KV, HEAD_DIM, Q_TILE), F32), pltpu.VMEM((HEAD_DIM, Q_PER_KV * Q_TILE), F32),
                        pltpu.VMEM((HEAD_DIM, Q_PER_KV * Q_TILE), F32)],
        compiler_params=_cparams("arbitrary", "arbitrary", "arbitrary"), name="attn_prompt",
    )(qc, qr, kc, vc, rb, rb, rb, rb, gates, ovt, ext)


def _attn_s1_body(q_ref, kc_ref, vc_ref, ov_ref, o_ref, idx_ref, val_ref, *, qpos, n_cmp, n_sel):
    rows = q_ref.shape[0]
    n = kc_ref.shape[0]
    col = lax.broadcasted_iota(I32, (1, n), 1)
    cmask = (col * CMP_STRIDE + (CMP_LEN - 1) <= qpos) & (col < n_cmp)
    s = _dot_nt(q_ref[...], kc_ref[...]) * ATTN_SCALE
    sm = jnp.where(cmask, s, NEG)
    e = jnp.exp(sm - jnp.max(sm, -1, keepdims=True))
    pr = jnp.where(cmask, e / jnp.sum(e, -1, keepdims=True), 0.0)
    o_ref[...] = _dot(pr.astype(BF16), vc_ref[...])
    head = lax.broadcasted_iota(I32, (rows, 1), 0)
    psum = jnp.sum(jnp.where(head < Q_PER_KV, pr, 0.0), 0, keepdims=True)
    imp = _dot3(jnp.broadcast_to(psum, (rows, n)), ov_ref[...])[0:1]
    w = ov_ref.shape[1]
    blk = lax.broadcasted_iota(I32, (1, w), 1)
    qblk = qpos // SEL_BLOCK
    valid = (blk * SEL_BLOCK <= qpos) & (blk < n_sel)
    forced = (blk == 0) | (blk == qblk) | (blk == qblk - 1)
    low = -3e38
    score = jnp.where(blk < n_sel, jnp.where(forced & valid, FORCE_SCORE, jnp.where(valid, imp, -1.0)), low)
    blkf = blk.astype(F32)
    lane = lax.broadcasted_iota(I32, (1, LANES), 1)
    idx_row = jnp.zeros((1, LANES), F32)
    val_row = jnp.full((1, LANES), -1.0, F32)
    for t in range(SEL_TOP):
        m = jnp.max(score, -1, keepdims=True)
        it = jnp.min(jnp.where(score == m, blkf, float(w)), -1, keepdims=True)
        idx_row = jnp.where(lane == t, it, idx_row)
        val_row = jnp.where(lane == t, m, val_row)
        score = jnp.where(blkf == it, low, score)
    idx_ref[...] = jnp.broadcast_to(idx_row, idx_ref.shape).astype(I32)
    val_ref[...] = jnp.broadcast_to(val_row, val_ref.shape)


def _attn_s1(qc, kc, vc, *, qpos):
    nb, _, rows, _ = qc.shape
    n_chunk = kc.shape[2]
    t_all = qpos + 1
    n_cmp = t_all // CMP_STRIDE - CMP_R + 1
    n_sel = -(-t_all // SEL_BLOCK)
    assert n_cmp <= n_chunk and n_sel >= SEL_TOP
    w = -(-n_sel // LANES) * LANES
    ov = _overlap_matrix(n_cmp, n_sel, n_chunk, w)
    body = functools.partial(_attn_s1_body, qpos=qpos, n_cmp=n_cmp, n_sel=n_sel)
    hspec = pl.BlockSpec((None, None, rows, HEAD_DIM), lambda b, g: (b, g, 0, 0))
    cspec = pl.BlockSpec((None, None, n_chunk, HEAD_DIM), lambda b, g: (b, g, 0, 0))
    return pl.pallas_call(
        body, grid=(nb, KV_HEADS),
        in_specs=[hspec, cspec, cspec, pl.BlockSpec((n_chunk, w), lambda b, g: (0, 0))],
        out_specs=[hspec, hspec, hspec],
        out_shape=[jax.ShapeDtypeStruct((nb, KV_HEADS, rows, HEAD_DIM), F32),
                   jax.ShapeDtypeStruct((nb, KV_HEADS, rows, LANES), I32),
                   jax.ShapeDtypeStruct((nb, KV_HEADS, rows, LANES), F32)],
        compiler_params=_cparams("arbitrary", "arbitrary"), name="attn_sample_cmp",
    )(qc, kc, vc, ov)


def _attn_s2_body(phys_ref, idx_ref, ok_ref, new_ref, q_ref, *refs, qpos, n_past_blk, past_len):
    kb_refs = refs[:SEL_TOP]
    vb_refs = refs[SEL_TOP:2 * SEL_TOP]
    kn_ref, vn_ref, wk_ref, wv_ref, kwn_ref, vwn_ref, oc_ref, g_ref, o_ref = refs[2 * SEL_TOP:]
    b, g = pl.program_id(0), pl.program_id(1)
    q = q_ref[...]

    def attend(s, v, sn, vn):
        m = jnp.maximum(jnp.max(s, -1, keepdims=True), jnp.max(sn, -1, keepdims=True))
        e, en = jnp.exp(s - m), jnp.exp(sn - m)
        den = jnp.sum(e, -1, keepdims=True) + jnp.sum(en, -1, keepdims=True)
        return (_dot(e.astype(BF16), v) + _dot(en.astype(BF16), vn)) / den

    def head_rows(ref, n):
        return ref[pl.ds(g, n, stride=KV_HEADS), :].astype(BF16)

    wk = wk_ref.shape[0] // KV_HEADS
    kp = (past_len - wk) + lax.broadcasted_iota(I32, (1, wk), 1)
    s = jnp.where((kp <= qpos) & (kp > qpos - WINDOW), _dot_nt(q, head_rows(wk_ref, wk)) * ATTN_SCALE, NEG)
    sn = _dot_nt(q, kwn_ref[...].astype(BF16)) * ATTN_SCALE
    coln = lax.broadcasted_iota(I32, (1, sn.shape[1]), 1)
    o_win = attend(s, head_rows(wv_ref, wk), jnp.where(coln == 0, sn, NEG), vwn_ref[...].astype(BF16))

    kcat = jnp.concatenate([head_rows(r, SEL_BLOCK) for r in kb_refs], 0)
    vcat = jnp.concatenate([head_rows(r, SEL_BLOCK) for r in vb_refs], 0)
    lane = lax.broadcasted_iota(I32, (1, SEL_TOP * SEL_BLOCK), 1)
    limit = jnp.full(lane.shape, -1, I32)
    for kk in range(SEL_TOP):
        t = (b * KV_HEADS + g) * SEL_TOP + kk
        lim = jnp.where((ok_ref[t] == 1) & (idx_ref[t] < n_past_blk), qpos - idx_ref[t] * SEL_BLOCK, -1)
        limit = jnp.where((lane >> 6) == kk, lim, limit)
    s = jnp.where((lane & (SEL_BLOCK - 1)) <= limit, _dot_nt(q, kcat) * ATTN_SCALE, NEG)
    sn = _dot_nt(q, kn_ref[...].astype(BF16)) * ATTN_SCALE
    sn = jnp.where(coln < new_ref[b * KV_HEADS + g], sn, NEG)
    o_sel = attend(s, vcat, sn, vn_ref[...].astype(BF16))

    gt = g_ref[...]
    o_ref[...] = gt[:, 0:1] * oc_ref[...] + gt[:, 1:2] * o_sel + gt[:, 2:3] * o_win


def _attn_s2(qr, pool_k, pool_v, kn, vn, win_k, win_v, kwn, vwn, o_cmp, gates, phys, idx, ok, has_new, *,
             qpos, past_len):
    nb, _, rows, _ = qr.shape
    wk = win_k.shape[0] // (nb * KV_HEADS)
    assert SEL_BLOCK == 64
    hspec = pl.BlockSpec((None, None, rows, HEAD_DIM), lambda b, g, *_: (b, g, 0, 0))
    bspec = lambda k: pl.BlockSpec((SEL_BLOCK * KV_HEADS, HEAD_DIM),
                                   lambda b, g, ph, *_: (ph[(b * KV_HEADS + g) * SEL_TOP + k], 0))
    wspec = pl.BlockSpec((wk * KV_HEADS, HEAD_DIM), lambda b, g, *_: (b, 0))
    blocks = [bspec(k) for k in range(SEL_TOP)]
    body = functools.partial(_attn_s2_body, qpos=qpos, n_past_blk=past_len // SEL_BLOCK, past_len=past_len)
    return pl.pallas_call(
        body,
        grid_spec=pltpu.PrefetchScalarGridSpec(
            num_scalar_prefetch=4, grid=(nb, KV_HEADS),
            in_specs=[hspec] + blocks + blocks + [hspec, hspec, wspec, wspec, hspec, hspec, hspec, hspec],
            out_specs=hspec),
        out_shape=jax.ShapeDtypeStruct((nb, KV_HEADS, rows, HEAD_DIM), F32),
        compiler_params=_cparams("arbitrary", "arbitrary"), name="attn_sample_sel_win",
    )(phys, idx, ok, has_new, qr, *([pool_k] * SEL_TOP), *([pool_v] * SEL_TOP), kn, vn, win_k, win_v, kwn, vwn,
      o_cmp, gates)


def _head_rows(a, rows=8):
    nb = a.shape[0]
    p = a.shape[1] // (KV_HEADS * HEAD_DIM)
    a = a.reshape(nb, KV_HEADS, p, HEAD_DIM)
    return jnp.pad(a, ((0, 0), (0, 0), (0, rows - p), (0, 0)))


def kernel(x_prompt, x_sample, cache_cmp_k, cache_cmp_v, cache_sel_k, cache_sel_v, state_win_k, state_win_v, page_table, norm_mix, norm_ffn, norm_kv, norm_final, gm_w_in, gm_ln_g, gm_ln_b, gm_w_s, gm_b_s, gm_w_out, nsa_w_in, nsa_w_out, kv_w, cmp_pe_k, cmp_w1_k, cmp_w2_k, cmp_pe_v, cmp_w1_v, cmp_w2_v, ff_w_gate, ff_w_up, ff_w_down, moe_router, moe_w_gate, moe_w_up, moe_w_down):
    B, S, D = x_prompt.shape
    DB, DS, _ = x_sample.shape
    assert DS == 1 and D == D_MODEL and S % KEY_TILE == 0
    n_prompt, n_sample = B * S, DB * DS
    n_tok = n_prompt + n_sample
    M = n_prompt + GM_CHUNK
    assert M % ROW_TILE == 0 and M % ROW_TILE_DOWN == 0 and n_sample <= 8
    n_pool = cache_cmp_k.shape[0]
    past_len = page_table.shape[1] * PAGE_SIZE
    qpos_s = past_len
    wk = state_win_k.shape[1]
    assert wk == min(WINDOW, past_len)

    x = jnp.concatenate([x_prompt.reshape(n_prompt, D), x_sample.reshape(n_sample, D),
                         jnp.zeros((M - n_tok, D), F32)], 0)

    pos = jnp.concatenate([jnp.tile(jnp.arange(S), B), jnp.full((n_sample,), past_len), jnp.zeros((M - n_tok,), I32)])
    half = ROT_DIM // 2
    inv = ROPE_THETA ** (-jnp.arange(half, dtype=F32) / half)
    ang = pos.astype(F32)[:, None] * inv[None, :]
    cos, sin = jnp.cos(ang), jnp.sin(ang)
    zeros_h = jnp.zeros((M, half), F32)
    rest = HEAD_DIM - ROT_DIM
    rope_c = jnp.concatenate([cos, cos, jnp.ones((M, rest), F32)], 1)
    rope_s1 = jnp.concatenate([-sin, zeros_h, jnp.zeros((M, rest), F32)], 1)
    rope_s2 = jnp.concatenate([zeros_h, sin, jnp.zeros((M, rest), F32)], 1)
    rope_extras = [(t, (ROW_TILE, HEAD_DIM), lambda j, i: (i, 0)) for t in (rope_c, rope_s1, rope_s2)]

    def residual_mm(xin, w, layer, res, tm, name):
        n = w.shape[2]
        return _ws_matmul(xin, [(w, layer)], _ep_residual, [(n, F32)], tm=tm, tn=COL_TILE, n_cols=n,
                          extras=[(res, (tm, COL_TILE), lambda j, i: (i, j))], name=name)[0]

    gm_v = []
    kv = None
    for l in range(DEPTH):
        h = _rmsnorm(x, norm_mix[l], BF16, tm=ROW_TILE)
        if l < N_A_LAYERS:
            z = _ws_matmul(h, [(gm_w_in, l)], _ep_gelu, [(2 * GM_DIM, BF16)], tm=ROW_TILE, tn=COL_TILE,
                           n_cols=2 * GM_DIM, name="gmlp_in")[0]
            o, v_s = _gm_gate(z, gm_ln_g[l], gm_ln_b[l], gm_w_s[l], gm_b_s[l], n_prompt=n_prompt, n_sample=n_sample)
            gm_v.append(v_s[:n_sample].reshape(DB, DS, GM_DIM))
            x = residual_mm(o, gm_w_out, l, x, ROW_TILE, "gmlp_out")
        else:
            bl = l - N_A_LAYERS
            qc, qr = _ws_matmul(h, [(nsa_w_in, bl)], _ep_q, [(N_HEADS * HEAD_DIM, BF16)] * 2, tm=ROW_TILE,
                                tn=COL_TILE, n_cols=N_HEADS * HEAD_DIM, extras=rope_extras, name="nsa_q")
            wg = nsa_w_in[bl, :, N_HEADS * HEAD_DIM:].reshape(D, N_BRANCH, KV_HEADS, Q_PER_KV)
            wg = jnp.transpose(wg, (0, 2, 1, 3)).reshape(D, KV_HEADS, N_BRANCH * Q_PER_KV)
            wg = jnp.pad(wg, ((0, 0), (0, 0), (0, LANES - N_BRANCH * Q_PER_KV))).reshape(1, D, KV_HEADS * LANES)
            gates = _ws_matmul(h, [(wg, 0)], _ep_sigmoid, [(KV_HEADS * LANES, F32)], tm=ROW_TILE, tn=KV_HEADS * LANES,
                               n_cols=KV_HEADS * LANES, name="nsa_gates")[0]
            o = _attn_prompt(qc, qr, kv['kc_p'], kv['vc_p'], kv['rb'], gates, batch=B, seq=S)
            qc_s = _head_rows(qc[n_prompt:n_tok])
            qr_s = _head_rows(qr[n_prompt:n_tok])
            oc_s, idx_s, val_s = _attn_s1(qc_s, kv['kc_s'], kv['vc_s'], qpos=qpos_s)
            idx = idx_s[:, :, 0, :SEL_TOP]
            ok = val_s[:, :, 0, :SEL_TOP] >= 0.0
            n_past_blk = past_len // SEL_BLOCK
            ppb = PAGE_SIZE // SEL_BLOCK
            ip = jnp.minimum(idx, n_past_blk - 1)
            phys = jnp.take_along_axis(page_table, (ip // ppb).reshape(DB, -1), 1).reshape(ip.shape) * ppb + ip % ppb
            has_new = jnp.any((idx >= n_past_blk) & ok, -1)
            g_s = gates[n_prompt:n_tok].reshape(DB, KV_HEADS, LANES)[:, :, :N_BRANCH * Q_PER_KV]
            g_s = jnp.transpose(g_s.reshape(DB, KV_HEADS, N_BRANCH, Q_PER_KV), (0, 1, 3, 2))
            g_s = jnp.pad(g_s, ((0, 0), (0, 0), (0, 8 - Q_PER_KV), (0, HEAD_DIM - N_BRANCH)))
            o_s = _attn_s2(qr_s, kv['pool_k'], kv['pool_v'], kv['ks_new'], kv['vs_new'], kv['win_k'], kv['win_v'],
                           kv['kw_new'], kv['vw_new'], oc_s, g_s, phys.reshape(-1).astype(I32),
                           idx.reshape(-1).astype(I32), ok.reshape(-1).astype(I32), has_new.reshape(-1).astype(I32),
                           qpos=qpos_s, past_len=past_len)
            o_s = o_s[:, :, :Q_PER_KV].reshape(n_sample, N_HEADS * HEAD_DIM).astype(BF16)
            o = jnp.concatenate([o, o_s, jnp.zeros((M - n_tok, N_HEADS * HEAD_DIM), BF16)], 0)
            x = residual_mm(o, nsa_w_out, bl, x, ROW_TILE, "nsa_out")

        if l % 2 == 0:
            h = _rmsnorm(x, norm_ffn[l], BF16, tm=ROW_TILE)
            hh = _ws_matmul(h, [(ff_w_gate, l // 2), (ff_w_up, l // 2)], _ep_swiglu, [(ff_w_gate.shape[2], BF16)],
                            tm=ROW_TILE, tn=COL_TILE, n_cols=ff_w_gate.shape[2], name="ffn_up")[0]
            x = residual_mm(hh, ff_w_down, l // 2, x, ROW_TILE_DOWN, "ffn_down")
        else:
            x = _moe(x, norm_ffn[l], moe_router[l // 2], moe_w_gate, moe_w_up, moe_w_down, l // 2, n_tok=n_tok)

        if l == N_A_LAYERS - 1:
            hkv = _rmsnorm(x, norm_kv, BF16, tm=ROW_TILE)
            r, rb = _ws_matmul(hkv, [(kv_w.reshape(1, D, 6 * KV_SEC), 0)], _ep_kv,
                               [(6 * KV_SEC, F32), (6 * KV_SEC, BF16)], tm=ROW_TILE, tn=KV_SEC,
                               n_cols=6 * KV_SEC, extras=rope_extras, name="kv_proj")
            pages_p = r.reshape(M // PAGE_SIZE, PAGE_SIZE, 6 * KV_SEC)
            table_p = jnp.arange(n_prompt // PAGE_SIZE, dtype=I32).reshape(B, S // PAGE_SIZE)
            r_s = r[n_prompt:n_tok]
            sec_s = lambda i: r_s[:, i * KV_SEC:(i + 1) * KV_SEC]
            rows2d = lambda c: c.reshape(-1, HEAD_DIM)
            kv = dict(
                rb=rb,
                kc_p=_compress(pages_p, table_p, 0, cmp_pe_k, cmp_w1_k, cmp_w2_k),
                vc_p=_compress(pages_p, table_p, 1, cmp_pe_v, cmp_w1_v, cmp_w2_v),
                kc_s=_compress(rows2d(cache_cmp_k), page_table, 0, cmp_pe_k, cmp_w1_k, cmp_w2_k),
                vc_s=_compress(rows2d(cache_cmp_v), page_table, 0, cmp_pe_v, cmp_w1_v, cmp_w2_v),
                pool_k=rows2d(cache_sel_k), pool_v=rows2d(cache_sel_v),
                ks_new=_head_rows(sec_s(2)), vs_new=_head_rows(sec_s(3)),
                kw_new=_head_rows(sec_s(4)), vw_new=_head_rows(sec_s(5)),
                win_k=rows2d(state_win_k), win_v=rows2d(state_win_v),
            )
            r_p = r[:n_prompt]
            pg = lambda i: r_p[:, i * KV_SEC:(i + 1) * KV_SEC].reshape(B, S // PAGE_SIZE, PAGE_SIZE, KV_HEADS, HEAD_DIM)
            wkp = min(WINDOW, S)
            tail = lambda i: r_p[:, i * KV_SEC:(i + 1) * KV_SEC].reshape(B, S, KV_HEADS, HEAD_DIM)[:, S - wkp:]
            row_s = lambda i: sec_s(i).reshape(DB, DS, KV_HEADS, HEAD_DIM)
            state_p = (pg(0), pg(1), pg(2), pg(3), tail(4), tail(5))
            state_s = (row_s(0), row_s(1), row_s(2), row_s(3),
                       jnp.concatenate([state_win_k, row_s(4)], 1)[:, DS:],
                       jnp.concatenate([state_win_v, row_s(5)], 1)[:, DS:])

    y = _rmsnorm(x, norm_final, F32, tm=ROW_TILE)
    y_prompt = y[:n_prompt].reshape(B, S, D)
    y_sample = y[n_prompt:n_tok].reshape(DB, DS, D)
    return (y_prompt, y_sample) + state_p + state_s + (jnp.stack(gm_v, 0),)
```

```python
import functools

import numpy as np
import jax
import jax.numpy as jnp
from jax import lax
from jax.experimental import pallas as pl
from jax.experimental.pallas import tpu as pltpu

F32 = jnp.float32
BF16 = jnp.bfloat16
I32 = jnp.int32

D_MODEL = 2048
DEPTH = 4
PAGE_SIZE = 128
N_A_LAYERS = DEPTH // 2
GM_CHUNK = 128
GM_DIM = D_MODEL
GM_GROUPS = 16
GM_GW = GM_DIM // GM_GROUPS
N_HEADS = 16
HEAD_DIM = 128
KV_HEADS = 4
Q_PER_KV = N_HEADS // KV_HEADS
ROT_DIM = HEAD_DIM // 4
ROPE_THETA = 500000.0
CMP_LEN = 32
CMP_STRIDE = 16
CMP_R = CMP_LEN // CMP_STRIDE
SEL_BLOCK = 64
SEL_TOP = 16
WINDOW = 512
N_BRANCH = 3
KV_SEC = KV_HEADS * HEAD_DIM
ATTN_SCALE = HEAD_DIM ** -0.5
N_EXPERTS = 8
TOP_K = 2
EPS = 1e-6
NEG = -1e30
FORCE_SCORE = 1e6

LANES = 128
VMEM_LIMIT_BYTES = 56 * 1024 * 1024
ROW_TILE = 832
ROW_TILE_DOWN = 416
COL_TILE = 512
MOE_UP_COL_TILE = 1024
MOE_BLOCK = 256
KEY_TILE = 256
Q_TILE = 256
PAGES_PER_STEP = 8


def _cparams(*sem):
    return pltpu.CompilerParams(dimension_semantics=sem, vmem_limit_bytes=VMEM_LIMIT_BYTES)


def _dot(a, b):
    return jnp.dot(a, b, preferred_element_type=F32)


def _dot_nt(a, b):
    return lax.dot_general(a, b, (((1,), (1,)), ((), ())), preferred_element_type=F32)


def _dot3(a, b):
    ah = a.astype(BF16)
    al = (a - ah.astype(F32)).astype(BF16)
    bh = b.astype(BF16)
    bl = (b - bh.astype(F32)).astype(BF16)
    return _dot(ah, bh) + _dot(al, bh) + _dot(ah, bl)


def _ws_body(*refs, nw, n_extra, n_out, epilogue):
    x_ref = refs[0]
    w_refs = refs[1:1 + nw]
    e_refs = refs[1 + nw:1 + nw + n_extra]
    o_refs = refs[1 + nw + n_extra:1 + nw + n_extra + n_out]
    wb_refs = refs[1 + nw + n_extra + n_out:]
    j = pl.program_id(0)

    @pl.when(pl.program_id(1) == 0)
    def _():
        for w_ref, wb in zip(w_refs, wb_refs):
            wb[...] = w_ref[...].astype(BF16)

    accs = [_dot(x_ref[...], wb[...]) for wb in wb_refs]
    outs = epilogue(j, accs, e_refs)
    for o_ref, o in zip(o_refs, outs):
        o_ref[...] = o.astype(o_ref.dtype)


def _ws_matmul(x, ws, epilogue, outs, *, tm, tn, n_cols, extras=(), name):
    M, K = x.shape
    nj = n_cols // tn
    ni = M // tm
    assert nj * tn == n_cols and ni * tm == M
    in_specs = [pl.BlockSpec((tm, K), lambda j, i: (i, 0))]
    in_specs += [pl.BlockSpec((None, K, tn), lambda j, i, layer=layer: (layer, 0, j)) for _, layer in ws]
    ws = [w for w, _ in ws]
    in_specs += [pl.BlockSpec(bs, im) for _, bs, im in extras]
    out_shape = [jax.ShapeDtypeStruct((M, c), dt) for c, dt in outs]
    out_specs = [pl.BlockSpec((tm, c // nj), lambda j, i: (i, j)) for c, _ in outs]
    body = functools.partial(_ws_body, nw=len(ws), n_extra=len(extras), n_out=len(outs), epilogue=epilogue)
    res = pl.pallas_call(
        body, grid=(nj, ni), in_specs=in_specs, out_specs=out_specs, out_shape=out_shape,
        scratch_shapes=[pltpu.VMEM((K, tn), BF16) for _ in ws],
        compiler_params=_cparams("arbitrary", "arbitrary"), name=name,
    )(x, *ws, *[a for a, _, _ in extras])
    return res


def _rope_tile(acc, c_ref, s1_ref, s2_ref):
    c, s1, s2 = c_ref[...], s1_ref[...], s2_ref[...]
    parts = []
    for h in range(acc.shape[1] // HEAD_DIM):
        xh = acc[:, h * HEAD_DIM:(h + 1) * HEAD_DIM]
        parts.append(xh * c + pltpu.roll(xh, HEAD_DIM - ROT_DIM // 2, 1) * s1 + pltpu.roll(xh, ROT_DIM // 2, 1) * s2)
    return jnp.concatenate(parts, 1)


def _ep_gelu(j, accs, e):
    return [jax.nn.gelu(accs[0])]


def _ep_residual(j, accs, e):
    return [e[0][...] + accs[0]]


def _ep_swiglu(j, accs, e):
    return [jax.nn.silu(accs[0]) * accs[1]]


def _ep_sigmoid(j, accs, e):
    return [jax.nn.sigmoid(accs[0])]


def _ep_kv(j, accs, e):
    acc = accs[0]
    roped = _rope_tile(acc, *e)
    r = jnp.where((j == 2) | (j == 4), roped, acc)
    return [r, r]


def _ep_q(j, accs, e):
    acc = accs[0]
    return [acc, _rope_tile(acc, *e)]


def _rms(x, g):
    return (x * lax.rsqrt(jnp.mean(x * x, -1, keepdims=True) + EPS)) * g


def _rms_body(x_ref, g_ref, o_ref):
    o_ref[...] = _rms(x_ref[...], g_ref[...]).astype(o_ref.dtype)


def _rmsnorm(x, g, out_dtype, *, tm):
    M, D = x.shape
    return pl.pallas_call(
        _rms_body, grid=(M // tm,),
        in_specs=[pl.BlockSpec((tm, D), lambda i: (i, 0)), pl.BlockSpec((1, D), lambda i: (0, 0))],
        out_specs=pl.BlockSpec((tm, D), lambda i: (i, 0)),
        out_shape=jax.ShapeDtypeStruct((M, D), out_dtype),
        compiler_params=_cparams("arbitrary"), name="rmsnorm",
    )(x, g.reshape(1, D))


def _rms_router_body(x_ref, g_ref, r_ref, h_ref, te_ref, tg_ref):
    h = _rms(x_ref[...], g_ref[...])
    h_ref[...] = h
    lane = lax.broadcasted_iota(I32, (1, LANES), 1).astype(F32)
    logits = jnp.where(lane < N_EXPERTS, _dot3(h, r_ref[...]), -jnp.inf)
    m1 = jnp.max(logits, -1, keepdims=True)
    i1 = jnp.min(jnp.where(logits == m1, lane, float(LANES)), -1, keepdims=True)
    rest = jnp.where(lane == i1, -jnp.inf, logits)
    m2 = jnp.max(rest, -1, keepdims=True)
    i2 = jnp.min(jnp.where(rest == m2, lane, float(LANES)), -1, keepdims=True)
    e2 = jnp.exp(m2 - m1)
    den = 1.0 + e2
    te_ref[...] = jnp.where(lane == 0, i1, jnp.where(lane == 1, i2, 0.0)).astype(I32)
    tg_ref[...] = jnp.where(lane == 0, 1.0 / den, jnp.where(lane == 1, e2 / den, 0.0))


def _rmsnorm_router(x, g, router, *, tm):
    M, D = x.shape
    rp = jnp.pad(router, ((0, 0), (0, LANES - N_EXPERTS)))
    return pl.pallas_call(
        _rms_router_body, grid=(M // tm,),
        in_specs=[pl.BlockSpec((tm, D), lambda i: (i, 0)), pl.BlockSpec((1, D), lambda i: (0, 0)),
                  pl.BlockSpec((D, LANES), lambda i: (0, 0))],
        out_specs=[pl.BlockSpec((tm, D), lambda i: (i, 0)), pl.BlockSpec((tm, LANES), lambda i: (i, 0)),
                   pl.BlockSpec((tm, LANES), lambda i: (i, 0))],
        out_shape=[jax.ShapeDtypeStruct((M, D), F32), jax.ShapeDtypeStruct((M, LANES), I32),
                   jax.ShapeDtypeStruct((M, LANES), F32)],
        compiler_params=_cparams("arbitrary"), name="rmsnorm_router",
    )(x, g.reshape(1, D), rp)


def _gm_gate_body(z_ref, lng_ref, lnb_ref, ws_ref, bst_ref, w00_ref, b00_ref, o_ref, vs_ref, *,
                  n_prompt_chunks, n_sample):
    c = pl.program_id(0)
    z = z_ref[...]
    u = z[:, :GM_DIM].astype(F32)
    v = z[:, GM_DIM:].astype(F32)
    mu = jnp.mean(v, -1, keepdims=True)
    vc = v - mu
    vn = (vc * lax.rsqrt(jnp.mean(vc * vc, -1, keepdims=True) + EPS)) * lng_ref[...] + lnb_ref[...]

    @pl.when(c < n_prompt_chunks)
    def _():
        vb = vn.astype(BF16)
        tri = lax.broadcasted_iota(I32, (GM_CHUNK, GM_CHUNK), 0) >= lax.broadcasted_iota(I32, (GM_CHUNK, GM_CHUNK), 1)
        for g in range(GM_GROUPS):
            sl = slice(g * GM_GW, (g + 1) * GM_GW)
            wg = jnp.where(tri, ws_ref[g], 0.0).astype(BF16)
            s = _dot(wg, vb[:, sl]) + bst_ref[:, g:g + 1]
            o_ref[:, sl] = (u[:, sl] * s).astype(o_ref.dtype)

    @pl.when(c >= n_prompt_chunks)
    def _():
        row = lax.broadcasted_iota(I32, (GM_CHUNK, 1), 0)
        s = w00_ref[...] * vn + b00_ref[...]
        o_ref[...] = jnp.where(row < n_sample, u * s, 0.0).astype(o_ref.dtype)
        vs_ref[...] = vn[:vs_ref.shape[0]]


def _gm_gate(z, ln_g, ln_b, w_s, b_s, *, n_prompt, n_sample):
    M = z.shape[0]
    n_pc = n_prompt // GM_CHUNK
    assert n_prompt % GM_CHUNK == 0 and M == (n_pc + 1) * GM_CHUNK and n_sample <= GM_CHUNK
    n_vs = -(-n_sample // 8) * 8
    w00 = jnp.repeat(w_s[:, 0, 0], GM_GW).reshape(1, GM_DIM)
    b00 = jnp.repeat(b_s[:, 0], GM_GW).reshape(1, GM_DIM)
    body = functools.partial(_gm_gate_body, n_prompt_chunks=n_pc, n_sample=n_sample)
    full = lambda *shape: pl.BlockSpec(shape, lambda c: (0,) * len(shape))
    return pl.pallas_call(
        body, grid=(n_pc + 1,),
        in_specs=[pl.BlockSpec((GM_CHUNK, 2 * GM_DIM), lambda c: (c, 0)), full(1, GM_DIM), full(1, GM_DIM),
                  full(GM_GROUPS, GM_CHUNK, GM_CHUNK), full(GM_CHUNK, GM_GROUPS), full(1, GM_DIM), full(1, GM_DIM)],
        out_specs=[pl.BlockSpec((GM_CHUNK, GM_DIM), lambda c: (c, 0)), full(n_vs, GM_DIM)],
        out_shape=[jax.ShapeDtypeStruct((M, GM_DIM), BF16), jax.ShapeDtypeStruct((n_vs, GM_DIM), F32)],
        compiler_params=_cparams("arbitrary"), name="gmlp_gate",
    )(z, ln_g.reshape(1, GM_DIM), ln_b.reshape(1, GM_DIM), w_s, b_s.T, w00, b00)


def _moe_gather_body(rt_ref, nu_ref, h_hbm, o_ref, buf, sem):
    i = pl.program_id(0)
    n_used = nu_ref[0]

    def row_copy(tok, slot, r):
        return pltpu.make_async_copy(h_hbm.at[pl.ds(tok, 1), :], buf.at[slot, pl.ds(r, 1), :], sem.at[slot])

    def start_block(blk, slot):
        def issue(r, c):
            row_copy(rt_ref[blk * MOE_BLOCK + r], slot, r).start()
            return c

        lax.fori_loop(0, MOE_BLOCK, issue, 0, unroll=8)

    @pl.when((i == 0) & (n_used > 0))
    def _():
        start_block(0, 0)

    @pl.when(i + 1 < n_used)
    def _():
        start_block(i + 1, (i + 1) % 2)

    @pl.when(i < n_used)
    def _():
        slot = i % 2

        def wait(r, c):
            row_copy(0, slot, r).wait()
            return c

        lax.fori_loop(0, MOE_BLOCK, wait, 0, unroll=8)
        o_ref[...] = buf[slot].astype(o_ref.dtype)

    @pl.when(i >= n_used)
    def _():
        o_ref[...] = jnp.zeros(o_ref.shape, o_ref.dtype)


def _moe_gather(h, row_tok, n_used, n_blk):
    D = h.shape[1]
    return pl.pallas_call(
        _moe_gather_body,
        grid_spec=pltpu.PrefetchScalarGridSpec(
            num_scalar_prefetch=2, grid=(n_blk,),
            in_specs=[pl.BlockSpec(memory_space=pl.ANY)],
            out_specs=pl.BlockSpec((MOE_BLOCK, D), lambda i, rt, nu: (i, 0)),
            scratch_shapes=[pltpu.VMEM((2, MOE_BLOCK, D), F32), pltpu.SemaphoreType.DMA((2,))]),
        out_shape=jax.ShapeDtypeStruct((n_blk * MOE_BLOCK, D), BF16),
        compiler_params=_cparams("arbitrary"), name="moe_gather",
    )(row_tok, n_used, h)


def _moe_mm_body(be_ref, bf_ref, nu_ref, *refs, nw, epilogue):
    x_ref = refs[0]
    w_refs = refs[1:1 + nw]
    o_ref = refs[1 + nw]
    wb_refs = refs[2 + nw:]
    i = pl.program_id(1)
    live = i < nu_ref[0]

    @pl.when(live & (bf_ref[i] == 1))
    def _():
        for w_ref, wb in zip(w_refs, wb_refs):
            wb[...] = w_ref[...].astype(BF16)

    @pl.when(live)
    def _():
        accs = [_dot(x_ref[...], wb[...]) for wb in wb_refs]
        o_ref[...] = epilogue(accs).astype(o_ref.dtype)

    @pl.when(jnp.logical_not(live))
    def _():
        o_ref[...] = jnp.zeros(o_ref.shape, o_ref.dtype)


def _moe_matmul(x, ws, layer, blk_e, blk_first, n_used, epilogue, out_dtype, *, tn, name):
    M, K = x.shape
    N = ws[0].shape[3]
    nj, ni = N // tn, M // MOE_BLOCK
    assert nj * tn == N
    clamp = lambda i, nu: jnp.minimum(i, nu[0] - 1)
    in_specs = [pl.BlockSpec((MOE_BLOCK, K), lambda j, i, be, bf, nu: (clamp(i, nu), 0))]
    in_specs += [pl.BlockSpec((None, None, K, tn), lambda j, i, be, bf, nu: (layer, be[clamp(i, nu)], 0, j))
                 for _ in ws]
    body = functools.partial(_moe_mm_body, nw=len(ws), epilogue=epilogue)
    return pl.pallas_call(
        body,
        grid_spec=pltpu.PrefetchScalarGridSpec(
            num_scalar_prefetch=3, grid=(nj, ni), in_specs=in_specs,
            out_specs=pl.BlockSpec((MOE_BLOCK, tn), lambda j, i, be, bf, nu: (i, j)),
            scratch_shapes=[pltpu.VMEM((K, tn), BF16) for _ in ws]),
        out_shape=jax.ShapeDtypeStruct((M, N), out_dtype),
        compiler_params=_cparams("arbitrary", "arbitrary"), name=name,
    )(blk_e, blk_first, n_used, x, *ws)


def _moe_combine_body(d0_ref, d1_ref, x_ref, g_ref, yb_hbm, o_ref, b0, b1, sem, *, tm):
    i = pl.program_id(0)

    def row_copy(src_row, buf, slot, r):
        return pltpu.make_async_copy(yb_hbm.at[pl.ds(src_row, 1), :], buf.at[slot, pl.ds(r, 1), :], sem.at[slot])

    def start_tile(t, slot):
        def issue(r, c):
            row_copy(d0_ref[t * tm + r], b0, slot, r).start()
            row_copy(d1_ref[t * tm + r], b1, slot, r).start()
            return c

        lax.fori_loop(0, tm, issue, 0, unroll=8)

    @pl.when(i == 0)
    def _():
        start_tile(0, 0)

    @pl.when(i + 1 < pl.num_programs(0))
    def _():
        start_tile(i + 1, (i + 1) % 2)

    slot = i % 2

    def wait(r, c):
        row_copy(0, b0, slot, r).wait()
        row_copy(0, b1, slot, r).wait()
        return c

    lax.fori_loop(0, tm, wait, 0, unroll=8)
    g = g_ref[...]
    o_ref[...] = x_ref[...] + (g[:, 0:1] * b0[slot] + g[:, 1:2] * b1[slot])


def _moe_combine(x, gates, yb, d0, d1, *, tm):
    M, D = x.shape
    body = functools.partial(_moe_combine_body, tm=tm)
    return pl.pallas_call(
        body,
        grid_spec=pltpu.PrefetchScalarGridSpec(
            num_scalar_prefetch=2, grid=(M // tm,),
            in_specs=[pl.BlockSpec((tm, D), lambda i, a, b: (i, 0)), pl.BlockSpec((tm, LANES), lambda i, a, b: (i, 0)),
                      pl.BlockSpec(memory_space=pl.ANY)],
            out_specs=pl.BlockSpec((tm, D), lambda i, a, b: (i, 0)),
            scratch_shapes=[pltpu.VMEM((2, tm, D), F32), pltpu.VMEM((2, tm, D), F32), pltpu.SemaphoreType.DMA((2,))]),
        out_shape=jax.ShapeDtypeStruct((M, D), F32),
        compiler_params=_cparams("arbitrary"), name="moe_combine",
    )(d0, d1, x, gates, yb)


def _moe(x, g_norm, router, wg, wu, wd, layer, *, n_tok):
    M, D = x.shape
    h, te, tg = _rmsnorm_router(x, g_norm, router, tm=ROW_TILE)
    a = n_tok * TOP_K
    n_blk = -(-a // MOE_BLOCK) + N_EXPERTS
    e_flat = te[:n_tok, :TOP_K].reshape(a)
    onehot = (e_flat[:, None] == jnp.arange(N_EXPERTS, dtype=I32)[None, :]).astype(I32)
    csum = jnp.cumsum(onehot, 0)
    pos = jnp.take_along_axis(csum, e_flat[:, None], 1)[:, 0] - 1
    counts = csum[-1]
    padded = (counts + MOE_BLOCK - 1) // MOE_BLOCK * MOE_BLOCK
    end_pad = jnp.cumsum(padded)
    start_pad = end_pad - padded
    dest = (start_pad[e_flat] + pos).astype(I32)
    tok = jnp.arange(a, dtype=I32) // TOP_K
    row_tok = jnp.full((n_blk * MOE_BLOCK,), n_tok, I32).at[dest].set(tok)
    n_used = (end_pad[-1] // MOE_BLOCK).astype(I32).reshape(1)
    blk_e = jnp.minimum(jnp.searchsorted(end_pad, jnp.arange(n_blk, dtype=I32) * MOE_BLOCK, side='right'),
                        N_EXPERTS - 1).astype(I32)
    blk_first = jnp.concatenate([jnp.ones((1,), I32), (blk_e[1:] != blk_e[:-1]).astype(I32)])
    d = jnp.pad(dest.reshape(n_tok, TOP_K), ((0, M - n_tok), (0, 0)))
    row = jnp.arange(M, dtype=I32)[:, None]
    gates = jnp.where(row < n_tok, tg, 0.0)

    xb = _moe_gather(h, row_tok, n_used, n_blk)
    hb = _moe_matmul(xb, [wg, wu], layer, blk_e, blk_first, n_used, lambda accs: jax.nn.silu(accs[0]) * accs[1],
                     BF16, tn=MOE_UP_COL_TILE, name="moe_up")
    yb = _moe_matmul(hb, [wd], layer, blk_e, blk_first, n_used, lambda accs: accs[0], F32, tn=COL_TILE,
                     name="moe_down")
    return _moe_combine(x, gates, yb, d[:, 0], d[:, 1], tm=ROW_TILE_DOWN)


def _cmp_z_body(pt_ref, *refs, heads_interleaved):
    n_in = len(refs) - 2
    page_refs = refs[:n_in]
    w_ref = refs[n_in]
    o_ref = refs[n_in + 1]
    n_ch = PAGE_SIZE // CMP_STRIDE
    cols = []
    for s in range(CMP_STRIDE):
        if heads_interleaved:
            rows = [pg[pl.ds(KV_HEADS * s + k, n_ch, stride=KV_HEADS * CMP_STRIDE), :]
                    for k in range(KV_HEADS) for pg in page_refs]
        else:
            rows = [pg[pl.ds(s, n_ch, stride=CMP_STRIDE), :] for pg in page_refs]
        cols.append(jnp.concatenate(rows, 0).astype(BF16))
    acc = _dot(jnp.concatenate(cols, 1), w_ref[...])
    m = PAGES_PER_STEP * n_ch
    for k in range(KV_HEADS):
        o_ref[k] = acc[k * m:(k + 1) * m]


def _cmp_z(pages, table, sec, w1cat):
    nb, npg = table.shape
    assert npg % PAGES_PER_STEP == 0
    n_ch = PAGE_SIZE // CMP_STRIDE
    m = PAGES_PER_STEP * n_ch
    interleaved = pages.ndim == 2
    if interleaved:
        specs = [pl.BlockSpec((PAGE_SIZE * KV_HEADS, HEAD_DIM),
                              lambda b, g, pt, q=q: (pt[b, g * PAGES_PER_STEP + q], 0)) for q in range(PAGES_PER_STEP)]
    else:
        specs = [pl.BlockSpec((None, PAGE_SIZE, HEAD_DIM),
                              lambda b, g, pt, k=k, q=q: (pt[b, g * PAGES_PER_STEP + q], 0, sec * KV_HEADS + k))
                 for k in range(KV_HEADS) for q in range(PAGES_PER_STEP)]
    return pl.pallas_call(
        functools.partial(_cmp_z_body, heads_interleaved=interleaved),
        grid_spec=pltpu.PrefetchScalarGridSpec(
            num_scalar_prefetch=1, grid=(nb, npg // PAGES_PER_STEP),
            in_specs=specs + [pl.BlockSpec((CMP_STRIDE * HEAD_DIM, CMP_R * HEAD_DIM), lambda b, g, pt: (0, 0))],
            out_specs=pl.BlockSpec((None, KV_HEADS, m, CMP_R * HEAD_DIM), lambda b, g, pt: (b, 0, g, 0))),
        out_shape=jax.ShapeDtypeStruct((nb, KV_HEADS, npg * n_ch, CMP_R * HEAD_DIM), F32),
        compiler_params=_cparams("arbitrary", "arbitrary"), name="cmp_z",
    )(table, *([pages] * len(specs)), w1cat)


def _cmp_fin_body(z_ref, pe_ref, w1_ref, w2_ref, o_ref):
    z = z_ref[...]
    n = z.shape[0]
    pe = jnp.broadcast_to(pe_ref[...], (8, pe_ref.shape[1])).astype(BF16)
    pre = _dot(pe, w1_ref[...].astype(BF16))[0:1]
    pre = pre + z[:, :HEAD_DIM]
    pre = pre + pltpu.roll(z[:, HEAD_DIM:], n - 1, 0)
    o_ref[...] = _dot(jax.nn.gelu(pre).astype(BF16), w2_ref[...].astype(BF16)).astype(o_ref.dtype)


def _cmp_fin(z, pe, w1, w2):
    nb, _, n, _ = z.shape
    return pl.pallas_call(
        _cmp_fin_body, grid=(nb, KV_HEADS),
        in_specs=[pl.BlockSpec((None, None, n, CMP_R * HEAD_DIM), lambda b, k: (b, k, 0, 0)),
                  pl.BlockSpec((1, CMP_LEN * HEAD_DIM), lambda b, k: (0, 0)),
                  pl.BlockSpec((CMP_LEN * HEAD_DIM, HEAD_DIM), lambda b, k: (0, 0)),
                  pl.BlockSpec((HEAD_DIM, HEAD_DIM), lambda b, k: (0, 0))],
        out_specs=pl.BlockSpec((None, None, n, HEAD_DIM), lambda b, k: (b, k, 0, 0)),
        out_shape=jax.ShapeDtypeStruct((nb, KV_HEADS, n, HEAD_DIM), BF16),
        compiler_params=_cparams("arbitrary", "arbitrary"), name="cmp_fin",
    )(z, pe.reshape(1, -1), w1, w2)


def _w1cat(w1):
    w = w1.reshape(CMP_R, CMP_STRIDE, HEAD_DIM, HEAD_DIM)
    return jnp.transpose(w, (1, 2, 0, 3)).reshape(CMP_STRIDE * HEAD_DIM, CMP_R * HEAD_DIM).astype(BF16)


def _compress(pages, table, sec, pe, w1, w2):
    return _cmp_fin(_cmp_z(pages, table, sec, _w1cat(w1)), pe, w1, w2)


def _overlap_matrix(nc, ns, rows, cols):
    cs = np.arange(nc) * CMP_STRIDE
    ss = np.arange(ns) * SEL_BLOCK
    ov = np.minimum(cs[:, None] + CMP_LEN, ss[None, :] + SEL_BLOCK) - np.maximum(cs[:, None], ss[None, :])
    ov = np.clip(ov, 0, None).astype(np.float32) / np.float32(CMP_LEN)
    out = np.zeros((rows, cols), np.float32)
    out[:nc, :ns] = ov
    return jnp.asarray(out)


def _attn_prompt_body(qc_ref, qr_ref, kc_ref, vc_ref, ks_ref, vs_ref, kw_ref, vw_ref, g_ref, ovt_ref, ext_ref, o_ref,
                      allow_ref, vct_ref, vst_ref, vwt_ref, q4_ref, s_ref, ocmp_ref, osel_ref, acc_ref, *,
                      seq, n_cmp, n_sel):
    qi = pl.program_id(2)
    qpos = qi * Q_TILE + lax.broadcasted_iota(I32, (1, Q_TILE), 1)
    row = lax.broadcasted_iota(I32, (LANES, 1), 0)

    @pl.when(qi == 0)
    def _():
        vct_ref[...] = vc_ref[...].astype(F32).T.astype(BF16)
        vst_ref[...] = vs_ref[...].astype(F32).T.astype(BF16)
        vwt_ref[...] = vw_ref[...].astype(F32).T.astype(BF16)

    cmask = (row * CMP_STRIDE + (CMP_LEN - 1) <= qpos) & (row < n_cmp)
    kc = kc_ref[...]
    psum = jnp.zeros((LANES, Q_TILE), F32)
    for p in range(Q_PER_KV):
        s = _dot_nt(kc, qc_ref[:, p * HEAD_DIM:(p + 1) * HEAD_DIM]) * ATTN_SCALE
        sm = jnp.where(cmask, s, NEG)
        e = jnp.exp(sm - jnp.max(sm, 0, keepdims=True))
        pr = jnp.where(cmask, e / jnp.sum(e, 0, keepdims=True), 0.0)
        psum = psum + pr
        ocmp_ref[p] = _dot(vct_ref[...], pr.astype(BF16))

    n_sel_pad = -(-n_sel // 8) * 8
    imp = _dot3(ovt_ref[...], psum)[:n_sel_pad]
    blk = row[:n_sel_pad]
    qblk = qpos >> 6
    valid = (blk * SEL_BLOCK <= qpos) & (blk < n_sel)
    forced = (blk == 0) | (blk == qblk) | (blk == qblk - 1)
    score = jnp.where(forced & valid, FORCE_SCORE, jnp.where(valid, imp, -1.0))
    rank = jnp.zeros(score.shape, F32)
    for j in range(n_sel):
        cj = score[j:j + 1, :]
        beats = (cj > score) | ((cj == score) & (blk > j))
        rank = rank + jnp.where(beats, 1.0, 0.0)
    sel = jnp.where((rank < SEL_TOP) & (score >= 0.0), 1.0, 0.0)
    sel = jnp.concatenate([sel, jnp.zeros((LANES - n_sel_pad, Q_TILE), F32)], 0).astype(BF16)
    selx = _dot(ext_ref[...], sel)
    kpos_all = lax.broadcasted_iota(I32, (seq, 1), 0)
    allow_ref[...] = jnp.where((selx > 0.5) & (kpos_all <= qpos), 1.0, 0.0)

    q4_ref[...] = jnp.concatenate([qr_ref[:, p * HEAD_DIM:(p + 1) * HEAD_DIM] for p in range(Q_PER_KV)], 0)

    def flash(k_ref, vt_ref, lo, hi, mask_fn):
        acc_ref[...] = jnp.zeros(acc_ref.shape, F32)
        last = seq // KEY_TILE - 1

        def scores(kt):
            start = pl.multiple_of(kt * KEY_TILE, KEY_TILE)
            return _dot_nt(k_ref[pl.ds(start, KEY_TILE), :], q4_ref[...])

        s_ref[lo % 2] = scores(lo)

        def body(kt, carry):
            m, l = carry
            start = pl.multiple_of(kt * KEY_TILE, KEY_TILE)
            s = s_ref[kt % 2] * ATTN_SCALE
            s_ref[(kt + 1) % 2] = scores(jnp.minimum(kt + 1, last))
            mask = mask_fn(start)
            s = jnp.where(jnp.concatenate([mask] * Q_PER_KV, 1), s, NEG)
            m_new = jnp.maximum(m, jnp.max(s, 0, keepdims=True))
            alpha = jnp.exp(m - m_new)
            e = jnp.exp(s - m_new)
            l = alpha * l + jnp.sum(e, 0, keepdims=True)
            acc_ref[...] = alpha * acc_ref[...] + _dot(vt_ref[:, pl.ds(start, KEY_TILE)], e.astype(BF16))
            return m_new, l

        init = (jnp.full((1, Q_PER_KV * Q_TILE), NEG, F32), jnp.zeros((1, Q_PER_KV * Q_TILE), F32))
        return lax.fori_loop(lo, hi, body, init)[1]

    hi = (qi * Q_TILE + Q_TILE - 1) // KEY_TILE + 1
    l = flash(ks_ref, vst_ref, 0, hi, lambda start: allow_ref[pl.ds(start, KEY_TILE), :] > 0.5)
    osel_ref[...] = acc_ref[...] / l

    def win_mask(start):
        kp = start + lax.broadcasted_iota(I32, (KEY_TILE, 1), 0)
        return (kp <= qpos) & (kp > qpos - WINDOW)

    l = flash(kw_ref, vwt_ref, jnp.maximum(qi * Q_TILE - (WINDOW - 1), 0) // KEY_TILE, hi, win_mask)
    owin = acc_ref[...] / l

    gt = g_ref[...].T
    for p in range(Q_PER_KV):
        sl = slice(p * Q_TILE, (p + 1) * Q_TILE)
        ot = (gt[p:p + 1] * ocmp_ref[p] + gt[Q_PER_KV + p:Q_PER_KV + p + 1] * osel_ref[:, sl]
              + gt[2 * Q_PER_KV + p:2 * Q_PER_KV + p + 1] * owin[:, sl])
        o_ref[:, p * HEAD_DIM:(p + 1) * HEAD_DIM] = ot.T.astype(o_ref.dtype)


def _attn_prompt(qc, qr, kc, vc, rb, gates, *, batch, seq):
    nq = seq // Q_TILE
    n_chunk = seq // CMP_STRIDE
    n_cmp = n_chunk - CMP_R + 1
    n_sel = seq // SEL_BLOCK
    assert n_chunk == LANES and n_sel <= LANES and seq % KEY_TILE == 0
    ovt = _overlap_matrix(n_cmp, n_sel, LANES, LANES).T
    ext = np.zeros((seq, LANES), np.float32)
    ext[np.arange(seq), np.arange(seq) // SEL_BLOCK] = 1.0
    ext = jnp.asarray(ext, BF16)
    qspec = pl.BlockSpec((Q_TILE, KV_SEC), lambda b, g, q: (b * nq + q, g))
    cspec = pl.BlockSpec((None, None, n_chunk, HEAD_DIM), lambda b, g, q: (b, g, 0, 0))
    kvspec = lambda sec: pl.BlockSpec((seq, HEAD_DIM), lambda b, g, q: (b, sec * KV_HEADS + g))
    body = functools.partial(_attn_prompt_body, seq=seq, n_cmp=n_cmp, n_sel=n_sel)
    return pl.pallas_call(
        body, grid=(batch, KV_HEADS, nq),
        in_specs=[qspec, qspec, cspec, cspec, kvspec(2), kvspec(3), kvspec(4), kvspec(5),
                  pl.BlockSpec((Q_TILE, LANES), lambda b, g, q: (b * nq + q, g)),
                  pl.BlockSpec((LANES, LANES), lambda b, g, q: (0, 0)),
                  pl.BlockSpec((seq, LANES), lambda b, g, q: (0, 0))],
        out_specs=qspec,
        out_shape=jax.ShapeDtypeStruct((batch * seq, N_HEADS * HEAD_DIM), BF16),
        scratch_shapes=[pltpu.VMEM((seq, Q_TILE), F32), pltpu.VMEM((HEAD_DIM, n_chunk), BF16),
                        pltpu.VMEM((HEAD_DIM, seq), BF16), pltpu.VMEM((HEAD_DIM, seq), BF16),
                        pltpu.VMEM((Q_PER_KV * Q_TILE, HEAD_DIM), BF16),
                        pltpu.VMEM((2, KEY_TILE, Q_PER_KV * Q_TILE), F32),
                        pltpu.VMEM((Q_PER_KV, HEAD_DIM, Q_TILE), F32), pltpu.VMEM((HEAD_DIM, Q_PER_KV * Q_TILE), F32),
                        pltpu.VMEM((HEAD_DIM, Q_PER_KV * Q_TILE), F32)],
        compiler_params=_cparams("arbitrary", "arbitrary", "arbitrary"), name="attn_prompt",
    )(qc, qr, kc, vc, rb, rb, rb, rb, gates, ovt, ext)


def _attn_s1_body(q_ref, kc_ref, vc_ref, ov_ref, o_ref, idx_ref, val_ref, sc_ref, *, qpos, n_cmp, n_sel):
    step = pl.program_id(0) * pl.num_programs(1) + pl.program_id(1)
    n_steps = pl.num_programs(0) * pl.num_programs(1)
    rows = q_ref.shape[0]
    n = kc_ref.shape[0]
    col = lax.broadcasted_iota(I32, (1, n), 1)
    cmask = (col * CMP_STRIDE + (CMP_LEN - 1) <= qpos) & (col < n_cmp)
    s = _dot_nt(q_ref[...], kc_ref[...]) * ATTN_SCALE
    sm = jnp.where(cmask, s, NEG)
    e = jnp.exp(sm - jnp.max(sm, -1, keepdims=True))
    pr = jnp.where(cmask, e / jnp.sum(e, -1, keepdims=True), 0.0)
    o_ref[...] = _dot(pr.astype(BF16), vc_ref[...])
    head = lax.broadcasted_iota(I32, (rows, 1), 0)
    psum = jnp.sum(jnp.where(head < Q_PER_KV, pr, 0.0), 0, keepdims=True)
    imp = _dot3(jnp.broadcast_to(psum, (rows, n)), ov_ref[...])[0:1]
    w = ov_ref.shape[1]
    blk = lax.broadcasted_iota(I32, (1, w), 1)
    qblk = qpos // SEL_BLOCK
    valid = (blk * SEL_BLOCK <= qpos) & (blk < n_sel)
    forced = (blk == 0) | (blk == qblk) | (blk == qblk - 1)
    low = -3e38
    sc_ref[pl.ds(step, 1), :] = jnp.where(
        blk < n_sel, jnp.where(forced & valid, FORCE_SCORE, jnp.where(valid, imp, -1.0)), low)

    @pl.when(step == n_steps - 1)
    def _():
        score = sc_ref[...]
        blkf = blk.astype(F32)
        lane = lax.broadcasted_iota(I32, (1, LANES), 1)
        idx_rows = jnp.zeros(idx_ref.shape, F32)
        val_rows = jnp.full(val_ref.shape, -1.0, F32)
        for t in range(SEL_TOP):
            m = jnp.max(score, -1, keepdims=True)
            it = jnp.min(jnp.where(score == m, blkf, float(w)), -1, keepdims=True)
            idx_rows = jnp.where(lane == t, it, idx_rows)
            val_rows = jnp.where(lane == t, m, val_rows)
            score = jnp.where(blkf == it, low, score)
        idx_ref[...] = idx_rows.astype(I32)
        val_ref[...] = val_rows


def _attn_s1(qc, kc, vc, *, qpos):
    nb, _, rows, _ = qc.shape
    n_chunk = kc.shape[2]
    t_all = qpos + 1
    n_cmp = t_all // CMP_STRIDE - CMP_R + 1
    n_sel = -(-t_all // SEL_BLOCK)
    assert n_cmp <= n_chunk and n_sel >= SEL_TOP
    w = -(-n_sel // LANES) * LANES
    ov = _overlap_matrix(n_cmp, n_sel, n_chunk, w)
    body = functools.partial(_attn_s1_body, qpos=qpos, n_cmp=n_cmp, n_sel=n_sel)
    hspec = pl.BlockSpec((None, None, rows, HEAD_DIM), lambda b, g: (b, g, 0, 0))
    cspec = pl.BlockSpec((None, None, n_chunk, HEAD_DIM), lambda b, g: (b, g, 0, 0))
    n_rows = nb * KV_HEADS
    assert n_rows % 8 == 0
    pick_spec = pl.BlockSpec((n_rows, LANES), lambda b, g: (0, 0))
    return pl.pallas_call(
        body, grid=(nb, KV_HEADS),
        in_specs=[hspec, cspec, cspec, pl.BlockSpec((n_chunk, w), lambda b, g: (0, 0))],
        out_specs=[hspec, pick_spec, pick_spec],
        out_shape=[jax.ShapeDtypeStruct((nb, KV_HEADS, rows, HEAD_DIM), F32),
                   jax.ShapeDtypeStruct((n_rows, LANES), I32),
                   jax.ShapeDtypeStruct((n_rows, LANES), F32)],
        scratch_shapes=[pltpu.VMEM((n_rows, w), F32)],
        compiler_params=_cparams("arbitrary", "arbitrary"), name="attn_sample_cmp",
    )(qc, kc, vc, ov)


def _attn_s2_body(phys_ref, idx_ref, ok_ref, new_ref, q_ref, *refs, qpos, n_past_blk, past_len):
    kb_refs = refs[:SEL_TOP]
    vb_refs = refs[SEL_TOP:2 * SEL_TOP]
    kn_ref, vn_ref, wk_ref, wv_ref, kwn_ref, vwn_ref, oc_ref, g_ref, o_ref = refs[2 * SEL_TOP:]
    b, g = pl.program_id(0), pl.program_id(1)
    q = q_ref[...]

    def attend(s, v, sn, vn):
        m = jnp.maximum(jnp.max(s, -1, keepdims=True), jnp.max(sn, -1, keepdims=True))
        e, en = jnp.exp(s - m), jnp.exp(sn - m)
        den = jnp.sum(e, -1, keepdims=True) + jnp.sum(en, -1, keepdims=True)
        return (_dot(e.astype(BF16), v) + _dot(en.astype(BF16), vn)) / den

    def head_rows(ref, n):
        return ref[pl.ds(g, n, stride=KV_HEADS), :].astype(BF16)

    wk = wk_ref.shape[0] // KV_HEADS
    kp = (past_len - wk) + lax.broadcasted_iota(I32, (1, wk), 1)
    s = jnp.where((kp <= qpos) & (kp > qpos - WINDOW), _dot_nt(q, head_rows(wk_ref, wk)) * ATTN_SCALE, NEG)
    sn = _dot_nt(q, kwn_ref[...].astype(BF16)) * ATTN_SCALE
    coln = lax.broadcasted_iota(I32, (1, sn.shape[1]), 1)
    o_win = attend(s, head_rows(wv_ref, wk), jnp.where(coln == 0, sn, NEG), vwn_ref[...].astype(BF16))

    kcat = jnp.concatenate([head_rows(r, SEL_BLOCK) for r in kb_refs], 0)
    vcat = jnp.concatenate([head_rows(r, SEL_BLOCK) for r in vb_refs], 0)
    lane = lax.broadcasted_iota(I32, (1, SEL_TOP * SEL_BLOCK), 1)
    limit = jnp.full(lane.shape, -1, I32)
    for kk in range(SEL_TOP):
        t = (b * KV_HEADS + g) * SEL_TOP + kk
        lim = jnp.where((ok_ref[t] == 1) & (idx_ref[t] < n_past_blk), qpos - idx_ref[t] * SEL_BLOCK, -1)
        limit = jnp.where((lane >> 6) == kk, lim, limit)
    s = jnp.where((lane & (SEL_BLOCK - 1)) <= limit, _dot_nt(q, kcat) * ATTN_SCALE, NEG)
    sn = _dot_nt(q, kn_ref[...].astype(BF16)) * ATTN_SCALE
    sn = jnp.where(coln < new_ref[b * KV_HEADS + g], sn, NEG)
    o_sel = attend(s, vcat, sn, vn_ref[...].astype(BF16))

    gt = g_ref[...]
    o_ref[...] = gt[:, 0:1] * oc_ref[...] + gt[:, 1:2] * o_sel + gt[:, 2:3] * o_win


def _attn_s2(qr, pool_k, pool_v, kn, vn, win_k, win_v, kwn, vwn, o_cmp, gates, phys, idx, ok, has_new, *,
             qpos, past_len):
    nb, _, rows, _ = qr.shape
    wk = win_k.shape[0] // (nb * KV_HEADS)
    assert SEL_BLOCK == 64
    hspec = pl.BlockSpec((None, None, rows, HEAD_DIM), lambda b, g, *_: (b, g, 0, 0))
    bspec = lambda k: pl.BlockSpec((SEL_BLOCK * KV_HEADS, HEAD_DIM),
                                   lambda b, g, ph, *_: (ph[(b * KV_HEADS + g) * SEL_TOP + k], 0))
    wspec = pl.BlockSpec((wk * KV_HEADS, HEAD_DIM), lambda b, g, *_: (b, 0))
    blocks = [bspec(k) for k in range(SEL_TOP)]
    body = functools.partial(_attn_s2_body, qpos=qpos, n_past_blk=past_len // SEL_BLOCK, past_len=past_len)
    return pl.pallas_call(
        body,
        grid_spec=pltpu.PrefetchScalarGridSpec(
            num_scalar_prefetch=4, grid=(nb, KV_HEADS),
            in_specs=[hspec] + blocks + blocks + [hspec, hspec, wspec, wspec, hspec, hspec, hspec, hspec],
            out_specs=hspec),
        out_shape=jax.ShapeDtypeStruct((nb, KV_HEADS, rows, HEAD_DIM), F32),
        compiler_params=_cparams("arbitrary", "arbitrary"), name="attn_sample_sel_win",
    )(phys, idx, ok, has_new, qr, *([pool_k] * SEL_TOP), *([pool_v] * SEL_TOP), kn, vn, win_k, win_v, kwn, vwn,
      o_cmp, gates)


def _head_rows(a, rows=8):
    nb = a.shape[0]
    p = a.shape[1] // (KV_HEADS * HEAD_DIM)
    a = a.reshape(nb, KV_HEADS, p, HEAD_DIM)
    return jnp.pad(a, ((0, 0), (0, 0), (0, rows - p), (0, 0)))


def kernel(x_prompt, x_sample, cache_cmp_k, cache_cmp_v, cache_sel_k, cache_sel_v, state_win_k, state_win_v, page_table, norm_mix, norm_ffn, norm_kv, norm_final, gm_w_in, gm_ln_g, gm_ln_b, gm_w_s, gm_b_s, gm_w_out, nsa_w_in, nsa_w_out, kv_w, cmp_pe_k, cmp_w1_k, cmp_w2_k, cmp_pe_v, cmp_w1_v, cmp_w2_v, ff_w_gate, ff_w_up, ff_w_down, moe_router, moe_w_gate, moe_w_up, moe_w_down):
    B, S, D = x_prompt.shape
    DB, DS, _ = x_sample.shape
    assert DS == 1 and D == D_MODEL and S % KEY_TILE == 0
    n_prompt, n_sample = B * S, DB * DS
    n_tok = n_prompt + n_sample
    M = n_prompt + GM_CHUNK
    assert M % ROW_TILE == 0 and M % ROW_TILE_DOWN == 0 and n_sample <= 8
    n_pool = cache_cmp_k.shape[0]
    past_len = page_table.shape[1] * PAGE_SIZE
    qpos_s = past_len
    wk = state_win_k.shape[1]
    assert wk == min(WINDOW, past_len)

    x = jnp.concatenate([x_prompt.reshape(n_prompt, D), x_sample.reshape(n_sample, D),
                         jnp.zeros((M - n_tok, D), F32)], 0)

    pos = jnp.concatenate([jnp.tile(jnp.arange(S), B), jnp.full((n_sample,), past_len), jnp.zeros((M - n_tok,), I32)])
    half = ROT_DIM // 2
    inv = ROPE_THETA ** (-jnp.arange(half, dtype=F32) / half)
    ang = pos.astype(F32)[:, None] * inv[None, :]
    cos, sin = jnp.cos(ang), jnp.sin(ang)
    zeros_h = jnp.zeros((M, half), F32)
    rest = HEAD_DIM - ROT_DIM
    rope_c = jnp.concatenate([cos, cos, jnp.ones((M, rest), F32)], 1)
    rope_s1 = jnp.concatenate([-sin, zeros_h, jnp.zeros((M, rest), F32)], 1)
    rope_s2 = jnp.concatenate([zeros_h, sin, jnp.zeros((M, rest), F32)], 1)
    rope_extras = [(t, (ROW_TILE, HEAD_DIM), lambda j, i: (i, 0)) for t in (rope_c, rope_s1, rope_s2)]

    def residual_mm(xin, w, layer, res, tm, name):
        n = w.shape[2]
        return _ws_matmul(xin, [(w, layer)], _ep_residual, [(n, F32)], tm=tm, tn=COL_TILE, n_cols=n,
                          extras=[(res, (tm, COL_TILE), lambda j, i: (i, j))], name=name)[0]

    gm_v = []
    kv = None
    for l in range(DEPTH):
        h = _rmsnorm(x, norm_mix[l], BF16, tm=ROW_TILE)
        if l < N_A_LAYERS:
            z = _ws_matmul(h, [(gm_w_in, l)], _ep_gelu, [(2 * GM_DIM, BF16)], tm=ROW_TILE, tn=COL_TILE,
                           n_cols=2 * GM_DIM, name="gmlp_in")[0]
            o, v_s = _gm_gate(z, gm_ln_g[l], gm_ln_b[l], gm_w_s[l], gm_b_s[l], n_prompt=n_prompt, n_sample=n_sample)
            gm_v.append(v_s[:n_sample].reshape(DB, DS, GM_DIM))
            x = residual_mm(o, gm_w_out, l, x, ROW_TILE, "gmlp_out")
        else:
            bl = l - N_A_LAYERS
            qc, qr = _ws_matmul(h, [(nsa_w_in, bl)], _ep_q, [(N_HEADS * HEAD_DIM, BF16)] * 2, tm=ROW_TILE,
                                tn=COL_TILE, n_cols=N_HEADS * HEAD_DIM, extras=rope_extras, name="nsa_q")
            wg = nsa_w_in[bl, :, N_HEADS * HEAD_DIM:].reshape(D, N_BRANCH, KV_HEADS, Q_PER_KV)
            wg = jnp.transpose(wg, (0, 2, 1, 3)).reshape(D, KV_HEADS, N_BRANCH * Q_PER_KV)
            wg = jnp.pad(wg, ((0, 0), (0, 0), (0, LANES - N_BRANCH * Q_PER_KV))).reshape(1, D, KV_HEADS * LANES)
            gates = _ws_matmul(h, [(wg, 0)], _ep_sigmoid, [(KV_HEADS * LANES, F32)], tm=ROW_TILE, tn=KV_HEADS * LANES,
                               n_cols=KV_HEADS * LANES, name="nsa_gates")[0]
            o = _attn_prompt(qc, qr, kv['kc_p'], kv['vc_p'], kv['rb'], gates, batch=B, seq=S)
            qc_s = _head_rows(qc[n_prompt:n_tok])
            qr_s = _head_rows(qr[n_prompt:n_tok])
            oc_s, idx_s, val_s = _attn_s1(qc_s, kv['kc_s'], kv['vc_s'], qpos=qpos_s)
            idx = idx_s[:, :SEL_TOP].reshape(DB, KV_HEADS, SEL_TOP)
            ok = val_s[:, :SEL_TOP].reshape(DB, KV_HEADS, SEL_TOP) >= 0.0
            n_past_blk = past_len // SEL_BLOCK
            ppb = PAGE_SIZE // SEL_BLOCK
            ip = jnp.minimum(idx, n_past_blk - 1)
            phys = jnp.take_along_axis(page_table, (ip // ppb).reshape(DB, -1), 1).reshape(ip.shape) * ppb + ip % ppb
            has_new = jnp.any((idx >= n_past_blk) & ok, -1)
            g_s = gates[n_prompt:n_tok].reshape(DB, KV_HEADS, LANES)[:, :, :N_BRANCH * Q_PER_KV]
            g_s = jnp.transpose(g_s.reshape(DB, KV_HEADS, N_BRANCH, Q_PER_KV), (0, 1, 3, 2))
            g_s = jnp.pad(g_s, ((0, 0), (0, 0), (0, 8 - Q_PER_KV), (0, HEAD_DIM - N_BRANCH)))
            o_s = _attn_s2(qr_s, kv['pool_k'], kv['pool_v'], kv['ks_new'], kv['vs_new'], kv['win_k'], kv['win_v'],
                           kv['kw_new'], kv['vw_new'], oc_s, g_s, phys.reshape(-1).astype(I32),
                           idx.reshape(-1).astype(I32), ok.reshape(-1).astype(I32), has_new.reshape(-1).astype(I32),
                           qpos=qpos_s, past_len=past_len)
            o_s = o_s[:, :, :Q_PER_KV].reshape(n_sample, N_HEADS * HEAD_DIM).astype(BF16)
            o = jnp.concatenate([o, o_s, jnp.zeros((M - n_tok, N_HEADS * HEAD_DIM), BF16)], 0)
            x = residual_mm(o, nsa_w_out, bl, x, ROW_TILE, "nsa_out")

        if l % 2 == 0:
            h = _rmsnorm(x, norm_ffn[l], BF16, tm=ROW_TILE)
            hh = _ws_matmul(h, [(ff_w_gate, l // 2), (ff_w_up, l // 2)], _ep_swiglu, [(ff_w_gate.shape[2], BF16)],
                            tm=ROW_TILE, tn=COL_TILE, n_cols=ff_w_gate.shape[2], name="ffn_up")[0]
            x = residual_mm(hh, ff_w_down, l // 2, x, ROW_TILE_DOWN, "ffn_down")
        else:
            x = _moe(x, norm_ffn[l], moe_router[l // 2], moe_w_gate, moe_w_up, moe_w_down, l // 2, n_tok=n_tok)

        if l == N_A_LAYERS - 1:
            hkv = _rmsnorm(x, norm_kv, BF16, tm=ROW_TILE)
            r, rb = _ws_matmul(hkv, [(kv_w.reshape(1, D, 6 * KV_SEC), 0)], _ep_kv,
                               [(6 * KV_SEC, F32), (6 * KV_SEC, BF16)], tm=ROW_TILE, tn=KV_SEC,
                               n_cols=6 * KV_SEC, extras=rope_extras, name="kv_proj")
            pages_p = r.reshape(M // PAGE_SIZE, PAGE_SIZE, 6 * KV_SEC)
            table_p = jnp.arange(n_prompt // PAGE_SIZE, dtype=I32).reshape(B, S // PAGE_SIZE)
            r_s = r[n_prompt:n_tok]
            sec_s = lambda i: r_s[:, i * KV_SEC:(i + 1) * KV_SEC]
            rows2d = lambda c: c.reshape(-1, HEAD_DIM)
            kv = dict(
                rb=rb,
                kc_p=_compress(pages_p, table_p, 0, cmp_pe_k, cmp_w1_k, cmp_w2_k),
                vc_p=_compress(pages_p, table_p, 1, cmp_pe_v, cmp_w1_v, cmp_w2_v),
                kc_s=_compress(rows2d(cache_cmp_k), page_table, 0, cmp_pe_k, cmp_w1_k, cmp_w2_k),
                vc_s=_compress(rows2d(cache_cmp_v), page_table, 0, cmp_pe_v, cmp_w1_v, cmp_w2_v),
                pool_k=rows2d(cache_sel_k), pool_v=rows2d(cache_sel_v),
                ks_new=_head_rows(sec_s(2)), vs_new=_head_rows(sec_s(3)),
                kw_new=_head_rows(sec_s(4)), vw_new=_head_rows(sec_s(5)),
                win_k=rows2d(state_win_k), win_v=rows2d(state_win_v),
            )
            r_p = r[:n_prompt]
            pg = lambda i: r_p[:, i * KV_SEC:(i + 1) * KV_SEC].reshape(B, S // PAGE_SIZE, PAGE_SIZE, KV_HEADS, HEAD_DIM)
            wkp = min(WINDOW, S)
            tail = lambda i: r_p[:, i * KV_SEC:(i + 1) * KV_SEC].reshape(B, S, KV_HEADS, HEAD_DIM)[:, S - wkp:]
            row_s = lambda i: sec_s(i).reshape(DB, DS, KV_HEADS, HEAD_DIM)
            state_p = (pg(0), pg(1), pg(2), pg(3), tail(4), tail(5))
            state_s = (row_s(0), row_s(1), row_s(2), row_s(3),
                       jnp.concatenate([state_win_k, row_s(4)], 1)[:, DS:],
                       jnp.concatenate([state_win_v, row_s(5)], 1)[:, DS:])

    y = _rmsnorm(x, norm_final, F32, tm=ROW_TILE)
    y_prompt = y[:n_prompt].reshape(B, S, D)
    y_sample = y[n_prompt:n_tok].reshape(DB, DS, D)
    return (y_prompt, y_sample) + state_p + state_s + (jnp.stack(gm_v, 0),)
```

```python
import functools

import numpy as np
import jax
import jax.numpy as jnp
from jax import lax
from jax.experimental import pallas as pl
from jax.experimental.pallas import tpu as pltpu

F32 = jnp.float32
BF16 = jnp.bfloat16
I32 = jnp.int32

D_MODEL = 2048
DEPTH = 4
PAGE_SIZE = 128
N_A_LAYERS = DEPTH // 2
GM_CHUNK = 128
GM_DIM = D_MODEL
GM_GROUPS = 16
GM_GW = GM_DIM // GM_GROUPS
N_HEADS = 16
HEAD_DIM = 128
KV_HEADS = 4
Q_PER_KV = N_HEADS // KV_HEADS
ROT_DIM = HEAD_DIM // 4
ROPE_THETA = 500000.0
CMP_LEN = 32
CMP_STRIDE = 16
CMP_R = CMP_LEN // CMP_STRIDE
SEL_BLOCK = 64
SEL_TOP = 16
WINDOW = 512
N_BRANCH = 3
KV_SEC = KV_HEADS * HEAD_DIM
ATTN_SCALE = HEAD_DIM ** -0.5
N_EXPERTS = 8
TOP_K = 2
EPS = 1e-6
NEG = -1e30
FORCE_SCORE = 1e6

LANES = 128
VMEM_LIMIT_BYTES = 56 * 1024 * 1024
ROW_TILE = 832
ROW_TILE_DOWN = 416
COL_TILE = 512
MOE_BLOCK = 256
MOE_SEG_BLOCKS = 10
MOE_FF_TILE = 256
KEY_TILE = 256
Q_TILE = 256
PAGES_PER_STEP = 8


def _cparams(*sem):
    return pltpu.CompilerParams(dimension_semantics=sem, vmem_limit_bytes=VMEM_LIMIT_BYTES)


def _dot(a, b):
    return jnp.dot(a, b, preferred_element_type=F32)


def _dot_nt(a, b):
    return lax.dot_general(a, b, (((1,), (1,)), ((), ())), preferred_element_type=F32)


def _dot3(a, b):
    ah = a.astype(BF16)
    al = (a - ah.astype(F32)).astype(BF16)
    bh = b.astype(BF16)
    bl = (b - bh.astype(F32)).astype(BF16)
    return _dot(ah, bh) + _dot(al, bh) + _dot(ah, bl)


def _ws_body(*refs, nw, n_extra, n_out, epilogue):
    x_ref = refs[0]
    w_refs = refs[1:1 + nw]
    e_refs = refs[1 + nw:1 + nw + n_extra]
    o_refs = refs[1 + nw + n_extra:1 + nw + n_extra + n_out]
    wb_refs = refs[1 + nw + n_extra + n_out:]
    j = pl.program_id(0)

    @pl.when(pl.program_id(1) == 0)
    def _():
        for w_ref, wb in zip(w_refs, wb_refs):
            wb[...] = w_ref[...].astype(BF16)

    accs = [_dot(x_ref[...], wb[...]) for wb in wb_refs]
    outs = epilogue(j, accs, e_refs)
    for o_ref, o in zip(o_refs, outs):
        o_ref[...] = o.astype(o_ref.dtype)


def _ws_matmul(x, ws, epilogue, outs, *, tm, tn, n_cols, extras=(), name):
    M, K = x.shape
    nj = n_cols // tn
    ni = M // tm
    assert nj * tn == n_cols and ni * tm == M
    in_specs = [pl.BlockSpec((tm, K), lambda j, i: (i, 0))]
    in_specs += [pl.BlockSpec((None, K, tn), lambda j, i, layer=layer: (layer, 0, j)) for _, layer in ws]
    ws = [w for w, _ in ws]
    in_specs += [pl.BlockSpec(bs, im) for _, bs, im in extras]
    out_shape = [jax.ShapeDtypeStruct((M, c), dt) for c, dt in outs]
    out_specs = [pl.BlockSpec((tm, c // nj), lambda j, i: (i, j)) for c, _ in outs]
    body = functools.partial(_ws_body, nw=len(ws), n_extra=len(extras), n_out=len(outs), epilogue=epilogue)
    res = pl.pallas_call(
        body, grid=(nj, ni), in_specs=in_specs, out_specs=out_specs, out_shape=out_shape,
        scratch_shapes=[pltpu.VMEM((K, tn), BF16) for _ in ws],
        compiler_params=_cparams("arbitrary", "arbitrary"), name=name,
    )(x, *ws, *[a for a, _, _ in extras])
    return res


def _rope_tile(acc, c_ref, s1_ref, s2_ref):
    c, s1, s2 = c_ref[...], s1_ref[...], s2_ref[...]
    parts = []
    for h in range(acc.shape[1] // HEAD_DIM):
        xh = acc[:, h * HEAD_DIM:(h + 1) * HEAD_DIM]
        parts.append(xh * c + pltpu.roll(xh, HEAD_DIM - ROT_DIM // 2, 1) * s1 + pltpu.roll(xh, ROT_DIM // 2, 1) * s2)
    return jnp.concatenate(parts, 1)


def _ep_gelu(j, accs, e):
    return [jax.nn.gelu(accs[0])]


def _ep_residual(j, accs, e):
    return [e[0][...] + accs[0]]


def _ep_swiglu(j, accs, e):
    return [jax.nn.silu(accs[0]) * accs[1]]


def _ep_sigmoid(j, accs, e):
    return [jax.nn.sigmoid(accs[0])]


def _ep_kv(j, accs, e):
    acc = accs[0]
    roped = _rope_tile(acc, *e)
    r = jnp.where((j == 2) | (j == 4), roped, acc)
    return [r, r]


def _ep_q(j, accs, e):
    acc = accs[0]
    return [acc, _rope_tile(acc, *e)]


def _rms(x, g):
    return (x * lax.rsqrt(jnp.mean(x * x, -1, keepdims=True) + EPS)) * g


def _rms_body(x_ref, g_ref, o_ref):
    o_ref[...] = _rms(x_ref[...], g_ref[...]).astype(o_ref.dtype)


def _rmsnorm(x, g, out_dtype, *, tm):
    M, D = x.shape
    return pl.pallas_call(
        _rms_body, grid=(M // tm,),
        in_specs=[pl.BlockSpec((tm, D), lambda i: (i, 0)), pl.BlockSpec((1, D), lambda i: (0, 0))],
        out_specs=pl.BlockSpec((tm, D), lambda i: (i, 0)),
        out_shape=jax.ShapeDtypeStruct((M, D), out_dtype),
        compiler_params=_cparams("arbitrary"), name="rmsnorm",
    )(x, g.reshape(1, D))


def _rms_router_body(x_ref, g_ref, r_ref, h_ref, te_ref, tg_ref):
    h = _rms(x_ref[...], g_ref[...])
    h_ref[...] = h
    lane = lax.broadcasted_iota(I32, (1, LANES), 1).astype(F32)
    logits = jnp.where(lane < N_EXPERTS, _dot3(h, r_ref[...]), -jnp.inf)
    m1 = jnp.max(logits, -1, keepdims=True)
    i1 = jnp.min(jnp.where(logits == m1, lane, float(LANES)), -1, keepdims=True)
    rest = jnp.where(lane == i1, -jnp.inf, logits)
    m2 = jnp.max(rest, -1, keepdims=True)
    i2 = jnp.min(jnp.where(rest == m2, lane, float(LANES)), -1, keepdims=True)
    e2 = jnp.exp(m2 - m1)
    den = 1.0 + e2
    te_ref[...] = jnp.where(lane == 0, i1, jnp.where(lane == 1, i2, 0.0)).astype(I32)
    tg_ref[...] = jnp.where(lane == 0, 1.0 / den, jnp.where(lane == 1, e2 / den, 0.0))


def _rmsnorm_router(x, g, router, *, tm):
    M, D = x.shape
    rp = jnp.pad(router, ((0, 0), (0, LANES - N_EXPERTS)))
    return pl.pallas_call(
        _rms_router_body, grid=(M // tm,),
        in_specs=[pl.BlockSpec((tm, D), lambda i: (i, 0)), pl.BlockSpec((1, D), lambda i: (0, 0)),
                  pl.BlockSpec((D, LANES), lambda i: (0, 0))],
        out_specs=[pl.BlockSpec((tm, D), lambda i: (i, 0)), pl.BlockSpec((tm, LANES), lambda i: (i, 0)),
                   pl.BlockSpec((tm, LANES), lambda i: (i, 0))],
        out_shape=[jax.ShapeDtypeStruct((M, D), F32), jax.ShapeDtypeStruct((M, LANES), I32),
                   jax.ShapeDtypeStruct((M, LANES), F32)],
        compiler_params=_cparams("arbitrary"), name="rmsnorm_router",
    )(x, g.reshape(1, D), rp)


def _gm_gate_body(z_ref, lng_ref, lnb_ref, ws_ref, bst_ref, w00_ref, b00_ref, o_ref, vs_ref, *,
                  n_prompt_chunks, n_sample):
    c = pl.program_id(0)
    z = z_ref[...]
    u = z[:, :GM_DIM].astype(F32)
    v = z[:, GM_DIM:].astype(F32)
    mu = jnp.mean(v, -1, keepdims=True)
    vc = v - mu
    vn = (vc * lax.rsqrt(jnp.mean(vc * vc, -1, keepdims=True) + EPS)) * lng_ref[...] + lnb_ref[...]

    @pl.when(c < n_prompt_chunks)
    def _():
        vb = vn.astype(BF16)
        tri = lax.broadcasted_iota(I32, (GM_CHUNK, GM_CHUNK), 0) >= lax.broadcasted_iota(I32, (GM_CHUNK, GM_CHUNK), 1)
        for g in range(GM_GROUPS):
            sl = slice(g * GM_GW, (g + 1) * GM_GW)
            wg = jnp.where(tri, ws_ref[g], 0.0).astype(BF16)
            s = _dot(wg, vb[:, sl]) + bst_ref[:, g:g + 1]
            o_ref[:, sl] = (u[:, sl] * s).astype(o_ref.dtype)

    @pl.when(c >= n_prompt_chunks)
    def _():
        row = lax.broadcasted_iota(I32, (GM_CHUNK, 1), 0)
        s = w00_ref[...] * vn + b00_ref[...]
        o_ref[...] = jnp.where(row < n_sample, u * s, 0.0).astype(o_ref.dtype)
        vs_ref[...] = vn[:vs_ref.shape[0]]


def _gm_gate(z, ln_g, ln_b, w_s, b_s, *, n_prompt, n_sample):
    M = z.shape[0]
    n_pc = n_prompt // GM_CHUNK
    assert n_prompt % GM_CHUNK == 0 and M == (n_pc + 1) * GM_CHUNK and n_sample <= GM_CHUNK
    n_vs = -(-n_sample // 8) * 8
    w00 = jnp.repeat(w_s[:, 0, 0], GM_GW).reshape(1, GM_DIM)
    b00 = jnp.repeat(b_s[:, 0], GM_GW).reshape(1, GM_DIM)
    body = functools.partial(_gm_gate_body, n_prompt_chunks=n_pc, n_sample=n_sample)
    full = lambda *shape: pl.BlockSpec(shape, lambda c: (0,) * len(shape))
    return pl.pallas_call(
        body, grid=(n_pc + 1,),
        in_specs=[pl.BlockSpec((GM_CHUNK, 2 * GM_DIM), lambda c: (c, 0)), full(1, GM_DIM), full(1, GM_DIM),
                  full(GM_GROUPS, GM_CHUNK, GM_CHUNK), full(GM_CHUNK, GM_GROUPS), full(1, GM_DIM), full(1, GM_DIM)],
        out_specs=[pl.BlockSpec((GM_CHUNK, GM_DIM), lambda c: (c, 0)), full(n_vs, GM_DIM)],
        out_shape=[jax.ShapeDtypeStruct((M, GM_DIM), BF16), jax.ShapeDtypeStruct((n_vs, GM_DIM), F32)],
        compiler_params=_cparams("arbitrary"), name="gmlp_gate",
    )(z, ln_g.reshape(1, GM_DIM), ln_b.reshape(1, GM_DIM), w_s, b_s.T, w00, b00)


def _moe_gather_body(rt_ref, nu_ref, h_hbm, o_ref, buf, sem):
    i = pl.program_id(0)
    n_used = nu_ref[0]

    def row_copy(tok, slot, r):
        return pltpu.make_async_copy(h_hbm.at[pl.ds(tok, 1), :], buf.at[slot, pl.ds(r, 1), :], sem.at[slot])

    def start_block(blk, slot):
        def issue(r, c):
            row_copy(rt_ref[blk * MOE_BLOCK + r], slot, r).start()
            return c

        lax.fori_loop(0, MOE_BLOCK, issue, 0, unroll=8)

    @pl.when((i == 0) & (n_used > 0))
    def _():
        start_block(0, 0)

    @pl.when(i + 1 < n_used)
    def _():
        start_block(i + 1, (i + 1) % 2)

    @pl.when(i < n_used)
    def _():
        slot = i % 2

        def wait(r, c):
            row_copy(0, slot, r).wait()
            return c

        lax.fori_loop(0, MOE_BLOCK, wait, 0, unroll=8)
        o_ref[...] = buf[slot].astype(o_ref.dtype)

    @pl.when(i >= n_used)
    def _():
        o_ref[...] = jnp.zeros(o_ref.shape, o_ref.dtype)


def _moe_gather(h, row_tok, n_used, n_blk):
    D = h.shape[1]
    return pl.pallas_call(
        _moe_gather_body,
        grid_spec=pltpu.PrefetchScalarGridSpec(
            num_scalar_prefetch=2, grid=(n_blk,),
            in_specs=[pl.BlockSpec(memory_space=pl.ANY)],
            out_specs=pl.BlockSpec((MOE_BLOCK, D), lambda i, rt, nu: (i, 0)),
            scratch_shapes=[pltpu.VMEM((2, MOE_BLOCK, D), F32), pltpu.SemaphoreType.DMA((2,))]),
        out_shape=jax.ShapeDtypeStruct((n_blk * MOE_BLOCK, D), BF16),
        compiler_params=_cparams("arbitrary"), name="moe_gather",
    )(row_tok, n_used, h)


def _moe_ffn_body(se_ref, sb_ref, sn_ref, ns_ref, nu_ref, xb_hbm, wg_ref, wu_ref, wd_ref, yb_hbm,
                  x_vmem, y_acc, wgb, wub, wdb, sem_in, sem_out, *, n_j, n_seg_max, n_blk):
    s, j = pl.program_id(0), pl.program_id(1)
    nblk = sn_ref[jnp.minimum(s, n_seg_max - 1)]
    row0 = sb_ref[jnp.minimum(s, n_seg_max - 1)] * MOE_BLOCK

    def block_copy(src, src_row, dst, dst_row, sem):
        return pltpu.make_async_copy(src.at[pl.ds(pl.multiple_of(src_row, MOE_BLOCK), MOE_BLOCK), :],
                                     dst.at[pl.ds(pl.multiple_of(dst_row, MOE_BLOCK), MOE_BLOCK), :], sem)

    def for_blocks(lo, hi, fn):
        def body(blk, c):
            fn(blk)
            return c

        lax.fori_loop(lo, hi, body, 0)

    @pl.when(s < ns_ref[0])
    def _():
        @pl.when(j == 0)
        def _():
            for_blocks(0, nblk, lambda blk: block_copy(xb_hbm, row0 + blk * MOE_BLOCK, x_vmem, blk * MOE_BLOCK,
                                                       sem_in).start())
            for_blocks(0, nblk, lambda blk: block_copy(xb_hbm, row0, x_vmem, blk * MOE_BLOCK, sem_in).wait())

        wgb[...] = wg_ref[...].astype(BF16)
        wub[...] = wu_ref[...].astype(BF16)
        wdb[...] = wd_ref[...].astype(BF16)

        def partial_out(blk):
            rows = pl.ds(pl.multiple_of(blk * MOE_BLOCK, MOE_BLOCK), MOE_BLOCK)
            x = x_vmem[rows, :]
            hidden = (jax.nn.silu(_dot(x, wgb[...])) * _dot(x, wub[...])).astype(BF16)
            return rows, _dot(hidden, wdb[...])

        def first(blk):
            rows, part = partial_out(blk)
            y_acc[rows, :] = part

        def accumulate(blk):
            rows, part = partial_out(blk)
            y_acc[rows, :] = y_acc[rows, :] + part

        @pl.when(j == 0)
        def _():
            for_blocks(0, nblk, first)

        @pl.when(j > 0)
        def _():
            for_blocks(0, nblk, accumulate)

        @pl.when(j == n_j - 1)
        def _():
            for_blocks(0, nblk, lambda blk: block_copy(y_acc, blk * MOE_BLOCK, yb_hbm, row0 + blk * MOE_BLOCK,
                                                       sem_out).start())
            for_blocks(0, nblk, lambda blk: block_copy(y_acc, blk * MOE_BLOCK, yb_hbm, row0, sem_out).wait())

    @pl.when((s == n_seg_max) & (j == 0))
    def _():
        y_acc[pl.ds(0, MOE_BLOCK), :] = jnp.zeros((MOE_BLOCK, y_acc.shape[1]), F32)
        for_blocks(nu_ref[0], n_blk, lambda blk: block_copy(y_acc, 0, yb_hbm, blk * MOE_BLOCK, sem_out).start())
        for_blocks(nu_ref[0], n_blk, lambda blk: block_copy(y_acc, 0, yb_hbm, 0, sem_out).wait())


def _moe_ffn(xb, wg, wu, wd, layer, seg_e, seg_blk, seg_nblk, n_seg, n_used, *, n_seg_max):
    M, D = xb.shape
    F = wg.shape[3]
    n_j = F // MOE_FF_TILE
    assert n_j * MOE_FF_TILE == F
    n_blk = M // MOE_BLOCK

    def w_index(s, j, se, sb, sn, ns, nu):
        live = s < ns[0]
        return se[jnp.minimum(s, ns[0] - 1)], jnp.where(live, j, n_j - 1)

    def up_map(s, j, *pref):
        e, jj = w_index(s, j, *pref)
        return layer, e, 0, jj

    def down_map(s, j, *pref):
        e, jj = w_index(s, j, *pref)
        return layer, e, jj, 0

    seg_rows = MOE_SEG_BLOCKS * MOE_BLOCK
    body = functools.partial(_moe_ffn_body, n_j=n_j, n_seg_max=n_seg_max, n_blk=n_blk)
    return pl.pallas_call(
        body,
        grid_spec=pltpu.PrefetchScalarGridSpec(
            num_scalar_prefetch=5, grid=(n_seg_max + 1, n_j),
            in_specs=[pl.BlockSpec(memory_space=pl.ANY),
                      pl.BlockSpec((None, None, D, MOE_FF_TILE), up_map),
                      pl.BlockSpec((None, None, D, MOE_FF_TILE), up_map),
                      pl.BlockSpec((None, None, MOE_FF_TILE, D), down_map)],
            out_specs=pl.BlockSpec(memory_space=pl.ANY),
            scratch_shapes=[pltpu.VMEM((seg_rows, D), BF16), pltpu.VMEM((seg_rows, D), F32),
                            pltpu.VMEM((D, MOE_FF_TILE), BF16), pltpu.VMEM((D, MOE_FF_TILE), BF16),
                            pltpu.VMEM((MOE_FF_TILE, D), BF16),
                            pltpu.SemaphoreType.DMA(()), pltpu.SemaphoreType.DMA(())]),
        out_shape=jax.ShapeDtypeStruct((M, D), F32),
        compiler_params=_cparams("arbitrary", "arbitrary"), name="moe_ffn",
    )(seg_e, seg_blk, seg_nblk, n_seg, n_used, xb, wg, wu, wd)


def _moe_combine_body(d0_ref, d1_ref, x_ref, g_ref, yb_hbm, o_ref, b0, b1, sem, *, tm):
    i = pl.program_id(0)

    def row_copy(src_row, buf, slot, r):
        return pltpu.make_async_copy(yb_hbm.at[pl.ds(src_row, 1), :], buf.at[slot, pl.ds(r, 1), :], sem.at[slot])

    def start_tile(t, slot):
        def issue(r, c):
            row_copy(d0_ref[t * tm + r], b0, slot, r).start()
            row_copy(d1_ref[t * tm + r], b1, slot, r).start()
            return c

        lax.fori_loop(0, tm, issue, 0, unroll=8)

    @pl.when(i == 0)
    def _():
        start_tile(0, 0)

    @pl.when(i + 1 < pl.num_programs(0))
    def _():
        start_tile(i + 1, (i + 1) % 2)

    slot = i % 2

    def wait(r, c):
        row_copy(0, b0, slot, r).wait()
        row_copy(0, b1, slot, r).wait()
        return c

    lax.fori_loop(0, tm, wait, 0, unroll=8)
    g = g_ref[...]
    o_ref[...] = x_ref[...] + (g[:, 0:1] * b0[slot] + g[:, 1:2] * b1[slot])


def _moe_combine(x, gates, yb, d0, d1, *, tm):
    M, D = x.shape
    body = functools.partial(_moe_combine_body, tm=tm)
    return pl.pallas_call(
        body,
        grid_spec=pltpu.PrefetchScalarGridSpec(
            num_scalar_prefetch=2, grid=(M // tm,),
            in_specs=[pl.BlockSpec((tm, D), lambda i, a, b: (i, 0)), pl.BlockSpec((tm, LANES), lambda i, a, b: (i, 0)),
                      pl.BlockSpec(memory_space=pl.ANY)],
            out_specs=pl.BlockSpec((tm, D), lambda i, a, b: (i, 0)),
            scratch_shapes=[pltpu.VMEM((2, tm, D), F32), pltpu.VMEM((2, tm, D), F32), pltpu.SemaphoreType.DMA((2,))]),
        out_shape=jax.ShapeDtypeStruct((M, D), F32),
        compiler_params=_cparams("arbitrary"), name="moe_combine",
    )(d0, d1, x, gates, yb)


def _moe(x, g_norm, router, wg, wu, wd, layer, *, n_tok):
    M, D = x.shape
    h, te, tg = _rmsnorm_router(x, g_norm, router, tm=ROW_TILE)
    a = n_tok * TOP_K
    n_blk = -(-a // MOE_BLOCK) + N_EXPERTS
    e_flat = te[:n_tok, :TOP_K].reshape(a)
    onehot = (e_flat[:, None] == jnp.arange(N_EXPERTS, dtype=I32)[None, :]).astype(I32)
    csum = jnp.cumsum(onehot, 0)
    pos = jnp.take_along_axis(csum, e_flat[:, None], 1)[:, 0] - 1
    counts = csum[-1]
    padded = (counts + MOE_BLOCK - 1) // MOE_BLOCK * MOE_BLOCK
    end_pad = jnp.cumsum(padded)
    start_pad = end_pad - padded
    dest = (start_pad[e_flat] + pos).astype(I32)
    tok = jnp.arange(a, dtype=I32) // TOP_K
    row_tok = jnp.full((n_blk * MOE_BLOCK,), n_tok, I32).at[dest].set(tok)
    n_used = (end_pad[-1] // MOE_BLOCK).astype(I32).reshape(1)
    n_seg_max = n_blk // MOE_SEG_BLOCKS + N_EXPERTS + 1
    e_blocks = padded // MOE_BLOCK
    e_segs = (e_blocks + MOE_SEG_BLOCKS - 1) // MOE_SEG_BLOCKS
    seg_end = jnp.cumsum(e_segs)
    s_ids = jnp.arange(n_seg_max, dtype=I32)
    seg_e = jnp.minimum(jnp.sum((seg_end[None, :] <= s_ids[:, None]).astype(I32), 1), N_EXPERTS - 1)
    local = s_ids - (seg_end - e_segs)[seg_e]
    seg_blk = (start_pad // MOE_BLOCK)[seg_e] + local * MOE_SEG_BLOCKS
    seg_nblk = jnp.where(s_ids < seg_end[-1], jnp.clip(e_blocks[seg_e] - local * MOE_SEG_BLOCKS, 0, MOE_SEG_BLOCKS), 0)
    d = jnp.pad(dest.reshape(n_tok, TOP_K), ((0, M - n_tok), (0, 0)))
    row = jnp.arange(M, dtype=I32)[:, None]
    gates = jnp.where(row < n_tok, tg, 0.0)

    xb = _moe_gather(h, row_tok, n_used, n_blk)
    yb = _moe_ffn(xb, wg, wu, wd, layer, seg_e.astype(I32), seg_blk.astype(I32), seg_nblk.astype(I32),
                  seg_end[-1:].astype(I32), n_used, n_seg_max=n_seg_max)
    return _moe_combine(x, gates, yb, d[:, 0], d[:, 1], tm=ROW_TILE_DOWN)


def _cmp_z_body(pt_ref, *refs, heads_interleaved):
    n_in = len(refs) - 2
    page_refs = refs[:n_in]
    w_ref = refs[n_in]
    o_ref = refs[n_in + 1]
    n_ch = PAGE_SIZE // CMP_STRIDE
    cols = []
    for s in range(CMP_STRIDE):
        if heads_interleaved:
            rows = [pg[pl.ds(KV_HEADS * s + k, n_ch, stride=KV_HEADS * CMP_STRIDE), :]
                    for k in range(KV_HEADS) for pg in page_refs]
        else:
            rows = [pg[pl.ds(s, n_ch, stride=CMP_STRIDE), :] for pg in page_refs]
        cols.append(jnp.concatenate(rows, 0).astype(BF16))
    acc = _dot(jnp.concatenate(cols, 1), w_ref[...])
    m = PAGES_PER_STEP * n_ch
    for k in range(KV_HEADS):
        o_ref[k] = acc[k * m:(k + 1) * m]


def _cmp_z(pages, table, sec, w1cat):
    nb, npg = table.shape
    assert npg % PAGES_PER_STEP == 0
    n_ch = PAGE_SIZE // CMP_STRIDE
    m = PAGES_PER_STEP * n_ch
    interleaved = pages.ndim == 2
    if interleaved:
        specs = [pl.BlockSpec((PAGE_SIZE * KV_HEADS, HEAD_DIM),
                              lambda b, g, pt, q=q: (pt[b, g * PAGES_PER_STEP + q], 0)) for q in range(PAGES_PER_STEP)]
    else:
        specs = [pl.BlockSpec((None, PAGE_SIZE, HEAD_DIM),
                              lambda b, g, pt, k=k, q=q: (pt[b, g * PAGES_PER_STEP + q], 0, sec * KV_HEADS + k))
                 for k in range(KV_HEADS) for q in range(PAGES_PER_STEP)]
    return pl.pallas_call(
        functools.partial(_cmp_z_body, heads_interleaved=interleaved),
        grid_spec=pltpu.PrefetchScalarGridSpec(
            num_scalar_prefetch=1, grid=(nb, npg // PAGES_PER_STEP),
            in_specs=specs + [pl.BlockSpec((CMP_STRIDE * HEAD_DIM, CMP_R * HEAD_DIM), lambda b, g, pt: (0, 0))],
            out_specs=pl.BlockSpec((None, KV_HEADS, m, CMP_R * HEAD_DIM), lambda b, g, pt: (b, 0, g, 0))),
        out_shape=jax.ShapeDtypeStruct((nb, KV_HEADS, npg * n_ch, CMP_R * HEAD_DIM), F32),
        compiler_params=_cparams("arbitrary", "arbitrary"), name="cmp_z",
    )(table, *([pages] * len(specs)), w1cat)


def _cmp_fin_body(z_ref, pe_ref, w1_ref, w2_ref, o_ref):
    z = z_ref[...]
    n = z.shape[0]
    pe = jnp.broadcast_to(pe_ref[...], (8, pe_ref.shape[1])).astype(BF16)
    pre = _dot(pe, w1_ref[...].astype(BF16))[0:1]
    pre = pre + z[:, :HEAD_DIM]
    pre = pre + pltpu.roll(z[:, HEAD_DIM:], n - 1, 0)
    o_ref[...] = _dot(jax.nn.gelu(pre).astype(BF16), w2_ref[...].astype(BF16)).astype(o_ref.dtype)


def _cmp_fin(z, pe, w1, w2):
    nb, _, n, _ = z.shape
    return pl.pallas_call(
        _cmp_fin_body, grid=(nb, KV_HEADS),
        in_specs=[pl.BlockSpec((None, None, n, CMP_R * HEAD_DIM), lambda b, k: (b, k, 0, 0)),
                  pl.BlockSpec((1, CMP_LEN * HEAD_DIM), lambda b, k: (0, 0)),
                  pl.BlockSpec((CMP_LEN * HEAD_DIM, HEAD_DIM), lambda b, k: (0, 0)),
                  pl.BlockSpec((HEAD_DIM, HEAD_DIM), lambda b, k: (0, 0))],
        out_specs=pl.BlockSpec((None, None, n, HEAD_DIM), lambda b, k: (b, k, 0, 0)),
        out_shape=jax.ShapeDtypeStruct((nb, KV_HEADS, n, HEAD_DIM), BF16),
        compiler_params=_cparams("arbitrary", "arbitrary"), name="cmp_fin",
    )(z, pe.reshape(1, -1), w1, w2)


def _w1cat(w1):
    w = w1.reshape(CMP_R, CMP_STRIDE, HEAD_DIM, HEAD_DIM)
    return jnp.transpose(w, (1, 2, 0, 3)).reshape(CMP_STRIDE * HEAD_DIM, CMP_R * HEAD_DIM).astype(BF16)


def _compress(pages, table, sec, pe, w1, w2):
    return _cmp_fin(_cmp_z(pages, table, sec, _w1cat(w1)), pe, w1, w2)


def _overlap_matrix(nc, ns, rows, cols):
    cs = np.arange(nc) * CMP_STRIDE
    ss = np.arange(ns) * SEL_BLOCK
    ov = np.minimum(cs[:, None] + CMP_LEN, ss[None, :] + SEL_BLOCK) - np.maximum(cs[:, None], ss[None, :])
    ov = np.clip(ov, 0, None).astype(np.float32) / np.float32(CMP_LEN)
    out = np.zeros((rows, cols), np.float32)
    out[:nc, :ns] = ov
    return jnp.asarray(out)


def _attn_prompt_body(qc_ref, qr_ref, kc_ref, vc_ref, ks_ref, vs_ref, kw_ref, vw_ref, g_ref, ovt_ref, ext_ref, o_ref,
                      allow_ref, vct_ref, vst_ref, vwt_ref, q4_ref, s_ref, ocmp_ref, osel_ref, acc_ref, *,
                      seq, n_cmp, n_sel):
    qi = pl.program_id(2)
    qpos = qi * Q_TILE + lax.broadcasted_iota(I32, (1, Q_TILE), 1)
    row = lax.broadcasted_iota(I32, (LANES, 1), 0)

    @pl.when(qi == 0)
    def _():
        vct_ref[...] = vc_ref[...].astype(F32).T.astype(BF16)
        vst_ref[...] = vs_ref[...].astype(F32).T.astype(BF16)
        vwt_ref[...] = vw_ref[...].astype(F32).T.astype(BF16)

    cmask = (row * CMP_STRIDE + (CMP_LEN - 1) <= qpos) & (row < n_cmp)
    kc = kc_ref[...]
    psum = jnp.zeros((LANES, Q_TILE), F32)
    for p in range(Q_PER_KV):
        s = _dot_nt(kc, qc_ref[:, p * HEAD_DIM:(p + 1) * HEAD_DIM]) * ATTN_SCALE
        sm = jnp.where(cmask, s, NEG)
        e = jnp.exp(sm - jnp.max(sm, 0, keepdims=True))
        pr = jnp.where(cmask, e / jnp.sum(e, 0, keepdims=True), 0.0)
        psum = psum + pr
        ocmp_ref[p] = _dot(vct_ref[...], pr.astype(BF16))

    n_sel_pad = -(-n_sel // 8) * 8
    imp = _dot3(ovt_ref[...], psum)[:n_sel_pad]
    blk = row[:n_sel_pad]
    qblk = qpos >> 6
    valid = (blk * SEL_BLOCK <= qpos) & (blk < n_sel)
    forced = (blk == 0) | (blk == qblk) | (blk == qblk - 1)
    score = jnp.where(forced & valid, FORCE_SCORE, jnp.where(valid, imp, -1.0))
    rank = jnp.zeros(score.shape, F32)
    for j in range(n_sel):
        cj = score[j:j + 1, :]
        beats = (cj > score) | ((cj == score) & (blk > j))
        rank = rank + jnp.where(beats, 1.0, 0.0)
    sel = jnp.where((rank < SEL_TOP) & (score >= 0.0), 1.0, 0.0)
    sel = jnp.concatenate([sel, jnp.zeros((LANES - n_sel_pad, Q_TILE), F32)], 0).astype(BF16)
    selx = _dot(ext_ref[...], sel)
    kpos_all = lax.broadcasted_iota(I32, (seq, 1), 0)
    allow_ref[...] = jnp.where((selx > 0.5) & (kpos_all <= qpos), 1.0, 0.0)

    q4_ref[...] = jnp.concatenate([qr_ref[:, p * HEAD_DIM:(p + 1) * HEAD_DIM] for p in range(Q_PER_KV)], 0)

    def flash(k_ref, vt_ref, lo, hi, mask_fn):
        acc_ref[...] = jnp.zeros(acc_ref.shape, F32)
        last = seq // KEY_TILE - 1

        def scores(kt):
            start = pl.multiple_of(kt * KEY_TILE, KEY_TILE)
            return _dot_nt(k_ref[pl.ds(start, KEY_TILE), :], q4_ref[...])

        s_ref[lo % 2] = scores(lo)

        def body(kt, carry):
            m, l = carry
            start = pl.multiple_of(kt * KEY_TILE, KEY_TILE)
            s = s_ref[kt % 2] * ATTN_SCALE
            s_ref[(kt + 1) % 2] = scores(jnp.minimum(kt + 1, last))
            mask = mask_fn(start)
            s = jnp.where(jnp.concatenate([mask] * Q_PER_KV, 1), s, NEG)
            m_new = jnp.maximum(m, jnp.max(s, 0, keepdims=True))
            alpha = jnp.exp(m - m_new)
            e = jnp.exp(s - m_new)
            l = alpha * l + jnp.sum(e, 0, keepdims=True)
            acc_ref[...] = alpha * acc_ref[...] + _dot(vt_ref[:, pl.ds(start, KEY_TILE)], e.astype(BF16))
            return m_new, l

        init = (jnp.full((1, Q_PER_KV * Q_TILE), NEG, F32), jnp.zeros((1, Q_PER_KV * Q_TILE), F32))
        return lax.fori_loop(lo, hi, body, init)[1]

    hi = (qi * Q_TILE + Q_TILE - 1) // KEY_TILE + 1
    l = flash(ks_ref, vst_ref, 0, hi, lambda start: allow_ref[pl.ds(start, KEY_TILE), :] > 0.5)
    osel_ref[...] = acc_ref[...] / l

    def win_mask(start):
        kp = start + lax.broadcasted_iota(I32, (KEY_TILE, 1), 0)
        return (kp <= qpos) & (kp > qpos - WINDOW)

    l = flash(kw_ref, vwt_ref, jnp.maximum(qi * Q_TILE - (WINDOW - 1), 0) // KEY_TILE, hi, win_mask)
    owin = acc_ref[...] / l

    gt = g_ref[...].T
    for p in range(Q_PER_KV):
        sl = slice(p * Q_TILE, (p + 1) * Q_TILE)
        ot = (gt[p:p + 1] * ocmp_ref[p] + gt[Q_PER_KV + p:Q_PER_KV + p + 1] * osel_ref[:, sl]
              + gt[2 * Q_PER_KV + p:2 * Q_PER_KV + p + 1] * owin[:, sl])
        o_ref[:, p * HEAD_DIM:(p + 1) * HEAD_DIM] = ot.T.astype(o_ref.dtype)


def _attn_prompt(qc, qr, kc, vc, rb, gates, *, batch, seq):
    nq = seq // Q_TILE
    n_chunk = seq // CMP_STRIDE
    n_cmp = n_chunk - CMP_R + 1
    n_sel = seq // SEL_BLOCK
    assert n_chunk == LANES and n_sel <= LANES and seq % KEY_TILE == 0
    ovt = _overlap_matrix(n_cmp, n_sel, LANES, LANES).T
    ext = np.zeros((seq, LANES), np.float32)
    ext[np.arange(seq), np.arange(seq) // SEL_BLOCK] = 1.0
    ext = jnp.asarray(ext, BF16)
    qspec = pl.BlockSpec((Q_TILE, KV_SEC), lambda b, g, q: (b * nq + q, g))
    cspec = pl.BlockSpec((None, None, n_chunk, HEAD_DIM), lambda b, g, q: (b, g, 0, 0))
    kvspec = lambda sec: pl.BlockSpec((seq, HEAD_DIM), lambda b, g, q: (b, sec * KV_HEADS + g))
    body = functools.partial(_attn_prompt_body, seq=seq, n_cmp=n_cmp, n_sel=n_sel)
    return pl.pallas_call(
        body, grid=(batch, KV_HEADS, nq),
        in_specs=[qspec, qspec, cspec, cspec, kvspec(2), kvspec(3), kvspec(4), kvspec(5),
                  pl.BlockSpec((Q_TILE, LANES), lambda b, g, q: (b * nq + q, g)),
                  pl.BlockSpec((LANES, LANES), lambda b, g, q: (0, 0)),
                  pl.BlockSpec((seq, LANES), lambda b, g, q: (0, 0))],
        out_specs=qspec,
        out_shape=jax.ShapeDtypeStruct((batch * seq, N_HEADS * HEAD_DIM), BF16),
        scratch_shapes=[pltpu.VMEM((seq, Q_TILE), F32), pltpu.VMEM((HEAD_DIM, n_chunk), BF16),
                        pltpu.VMEM((HEAD_DIM, seq), BF16), pltpu.VMEM((HEAD_DIM, seq), BF16),
                        pltpu.VMEM((Q_PER_KV * Q_TILE, HEAD_DIM), BF16),
                        pltpu.VMEM((2, KEY_TILE, Q_PER_KV * Q_TILE), F32),
                        pltpu.VMEM((Q_PER_KV, HEAD_DIM, Q_TILE), F32), pltpu.VMEM((HEAD_DIM, Q_PER_KV * Q_TILE), F32),
                        pltpu.VMEM((HEAD_DIM, Q_PER_KV * Q_TILE), F32)],
        compiler_params=_cparams("arbitrary", "arbitrary", "arbitrary"), name="attn_prompt",
    )(qc, qr, kc, vc, rb, rb, rb, rb, gates, ovt, ext)


def _attn_s1_body(q_ref, kc_ref, vc_ref, ov_ref, o_ref, idx_ref, val_ref, sc_ref, *, qpos, n_cmp, n_sel):
    step = pl.program_id(0) * pl.num_programs(1) + pl.program_id(1)
    n_steps = pl.num_programs(0) * pl.num_programs(1)
    rows = q_ref.shape[0]
    n = kc_ref.shape[0]
    col = lax.broadcasted_iota(I32, (1, n), 1)
    cmask = (col * CMP_STRIDE + (CMP_LEN - 1) <= qpos) & (col < n_cmp)
    s = _dot_nt(q_ref[...], kc_ref[...]) * ATTN_SCALE
    sm = jnp.where(cmask, s, NEG)
    e = jnp.exp(sm - jnp.max(sm, -1, keepdims=True))
    pr = jnp.where(cmask, e / jnp.sum(e, -1, keepdims=True), 0.0)
    o_ref[...] = _dot(pr.astype(BF16), vc_ref[...])
    head = lax.broadcasted_iota(I32, (rows, 1), 0)
    psum = jnp.sum(jnp.where(head < Q_PER_KV, pr, 0.0), 0, keepdims=True)
    imp = _dot3(jnp.broadcast_to(psum, (rows, n)), ov_ref[...])[0:1]
    w = ov_ref.shape[1]
    blk = lax.broadcasted_iota(I32, (1, w), 1)
    qblk = qpos // SEL_BLOCK
    valid = (blk * SEL_BLOCK <= qpos) & (blk < n_sel)
    forced = (blk == 0) | (blk == qblk) | (blk == qblk - 1)
    low = -3e38
    sc_ref[pl.ds(step, 1), :] = jnp.where(
        blk < n_sel, jnp.where(forced & valid, FORCE_SCORE, jnp.where(valid, imp, -1.0)), low)

    @pl.when(step == n_steps - 1)
    def _():
        score = sc_ref[...]
        blkf = blk.astype(F32)
        lane = lax.broadcasted_iota(I32, (1, LANES), 1)
        idx_rows = jnp.zeros(idx_ref.shape, F32)
        val_rows = jnp.full(val_ref.shape, -1.0, F32)
        for t in range(SEL_TOP):
            m = jnp.max(score, -1, keepdims=True)
            it = jnp.min(jnp.where(score == m, blkf, float(w)), -1, keepdims=True)
            idx_rows = jnp.where(lane == t, it, idx_rows)
            val_rows = jnp.where(lane == t, m, val_rows)
            score = jnp.where(blkf == it, low, score)
        idx_ref[...] = idx_rows.astype(I32)
        val_ref[...] = val_rows


def _attn_s1(qc, kc, vc, *, qpos):
    nb, _, rows, _ = qc.shape
    n_chunk = kc.shape[2]
    t_all = qpos + 1
    n_cmp = t_all // CMP_STRIDE - CMP_R + 1
    n_sel = -(-t_all // SEL_BLOCK)
    assert n_cmp <= n_chunk and n_sel >= SEL_TOP
    w = -(-n_sel // LANES) * LANES
    ov = _overlap_matrix(n_cmp, n_sel, n_chunk, w)
    body = functools.partial(_attn_s1_body, qpos=qpos, n_cmp=n_cmp, n_sel=n_sel)
    hspec = pl.BlockSpec((None, None, rows, HEAD_DIM), lambda b, g: (b, g, 0, 0))
    cspec = pl.BlockSpec((None, None, n_chunk, HEAD_DIM), lambda b, g: (b, g, 0, 0))
    n_rows = nb * KV_HEADS
    assert n_rows % 8 == 0
    pick_spec = pl.BlockSpec((n_rows, LANES), lambda b, g: (0, 0))
    return pl.pallas_call(
        body, grid=(nb, KV_HEADS),
        in_specs=[hspec, cspec, cspec, pl.BlockSpec((n_chunk, w), lambda b, g: (0, 0))],
        out_specs=[hspec, pick_spec, pick_spec],
        out_shape=[jax.ShapeDtypeStruct((nb, KV_HEADS, rows, HEAD_DIM), F32),
                   jax.ShapeDtypeStruct((n_rows, LANES), I32),
                   jax.ShapeDtypeStruct((n_rows, LANES), F32)],
        scratch_shapes=[pltpu.VMEM((n_rows, w), F32)],
        compiler_params=_cparams("arbitrary", "arbitrary"), name="attn_sample_cmp",
    )(qc, kc, vc, ov)


def _attn_s2_body(phys_ref, idx_ref, ok_ref, new_ref, q_ref, *refs, qpos, n_past_blk, past_len):
    kb_refs = refs[:SEL_TOP]
    vb_refs = refs[SEL_TOP:2 * SEL_TOP]
    kn_ref, vn_ref, wk_ref, wv_ref, kwn_ref, vwn_ref, oc_ref, g_ref, o_ref = refs[2 * SEL_TOP:]
    b, g = pl.program_id(0), pl.program_id(1)
    q = q_ref[...]

    def attend(s, v, sn, vn):
        m = jnp.maximum(jnp.max(s, -1, keepdims=True), jnp.max(sn, -1, keepdims=True))
        e, en = jnp.exp(s - m), jnp.exp(sn - m)
        den = jnp.sum(e, -1, keepdims=True) + jnp.sum(en, -1, keepdims=True)
        return (_dot(e.astype(BF16), v) + _dot(en.astype(BF16), vn)) / den

    def head_rows(ref, n):
        return ref[pl.ds(g, n, stride=KV_HEADS), :].astype(BF16)

    wk = wk_ref.shape[0] // KV_HEADS
    kp = (past_len - wk) + lax.broadcasted_iota(I32, (1, wk), 1)
    s = jnp.where((kp <= qpos) & (kp > qpos - WINDOW), _dot_nt(q, head_rows(wk_ref, wk)) * ATTN_SCALE, NEG)
    sn = _dot_nt(q, kwn_ref[...].astype(BF16)) * ATTN_SCALE
    coln = lax.broadcasted_iota(I32, (1, sn.shape[1]), 1)
    o_win = attend(s, head_rows(wv_ref, wk), jnp.where(coln == 0, sn, NEG), vwn_ref[...].astype(BF16))

    kcat = jnp.concatenate([head_rows(r, SEL_BLOCK) for r in kb_refs], 0)
    vcat = jnp.concatenate([head_rows(r, SEL_BLOCK) for r in vb_refs], 0)
    lane = lax.broadcasted_iota(I32, (1, SEL_TOP * SEL_BLOCK), 1)
    limit = jnp.full(lane.shape, -1, I32)
    for kk in range(SEL_TOP):
        t = (b * KV_HEADS + g) * SEL_TOP + kk
        lim = jnp.where((ok_ref[t] == 1) & (idx_ref[t] < n_past_blk), qpos - idx_ref[t] * SEL_BLOCK, -1)
        limit = jnp.where((lane >> 6) == kk, lim, limit)
    s = jnp.where((lane & (SEL_BLOCK - 1)) <= limit, _dot_nt(q, kcat) * ATTN_SCALE, NEG)
    sn = _dot_nt(q, kn_ref[...].astype(BF16)) * ATTN_SCALE
    sn = jnp.where(coln < new_ref[b * KV_HEADS + g], sn, NEG)
    o_sel = attend(s, vcat, sn, vn_ref[...].astype(BF16))

    gt = g_ref[...]
    o_ref[...] = gt[:, 0:1] * oc_ref[...] + gt[:, 1:2] * o_sel + gt[:, 2:3] * o_win


def _attn_s2(qr, pool_k, pool_v, kn, vn, win_k, win_v, kwn, vwn, o_cmp, gates, phys, idx, ok, has_new, *,
             qpos, past_len):
    nb, _, rows, _ = qr.shape
    wk = win_k.shape[0] // (nb * KV_HEADS)
    assert SEL_BLOCK == 64
    hspec = pl.BlockSpec((None, None, rows, HEAD_DIM), lambda b, g, *_: (b, g, 0, 0))
    bspec = lambda k: pl.BlockSpec((SEL_BLOCK * KV_HEADS, HEAD_DIM),
                                   lambda b, g, ph, *_: (ph[(b * KV_HEADS + g) * SEL_TOP + k], 0))
    wspec = pl.BlockSpec((wk * KV_HEADS, HEAD_DIM), lambda b, g, *_: (b, 0))
    blocks = [bspec(k) for k in range(SEL_TOP)]
    body = functools.partial(_attn_s2_body, qpos=qpos, n_past_blk=past_len // SEL_BLOCK, past_len=past_len)
    return pl.pallas_call(
        body,
        grid_spec=pltpu.PrefetchScalarGridSpec(
            num_scalar_prefetch=4, grid=(nb, KV_HEADS),
            in_specs=[hspec] + blocks + blocks + [hspec, hspec, wspec, wspec, hspec, hspec, hspec, hspec],
            out_specs=hspec),
        out_shape=jax.ShapeDtypeStruct((nb, KV_HEADS, rows, HEAD_DIM), F32),
        compiler_params=_cparams("arbitrary", "arbitrary"), name="attn_sample_sel_win",
    )(phys, idx, ok, has_new, qr, *([pool_k] * SEL_TOP), *([pool_v] * SEL_TOP), kn, vn, win_k, win_v, kwn, vwn,
      o_cmp, gates)


def _head_rows(a, rows=8):
    nb = a.shape[0]
    p = a.shape[1] // (KV_HEADS * HEAD_DIM)
    a = a.reshape(nb, KV_HEADS, p, HEAD_DIM)
    return jnp.pad(a, ((0, 0), (0, 0), (0, rows - p), (0, 0)))


def kernel(x_prompt, x_sample, cache_cmp_k, cache_cmp_v, cache_sel_k, cache_sel_v, state_win_k, state_win_v, page_table, norm_mix, norm_ffn, norm_kv, norm_final, gm_w_in, gm_ln_g, gm_ln_b, gm_w_s, gm_b_s, gm_w_out, nsa_w_in, nsa_w_out, kv_w, cmp_pe_k, cmp_w1_k, cmp_w2_k, cmp_pe_v, cmp_w1_v, cmp_w2_v, ff_w_gate, ff_w_up, ff_w_down, moe_router, moe_w_gate, moe_w_up, moe_w_down):
    B, S, D = x_prompt.shape
    DB, DS, _ = x_sample.shape
    assert DS == 1 and D == D_MODEL and S % KEY_TILE == 0
    n_prompt, n_sample = B * S, DB * DS
    n_tok = n_prompt + n_sample
    M = n_prompt + GM_CHUNK
    assert M % ROW_TILE == 0 and M % ROW_TILE_DOWN == 0 and n_sample <= 8
    n_pool = cache_cmp_k.shape[0]
    past_len = page_table.shape[1] * PAGE_SIZE
    qpos_s = past_len
    wk = state_win_k.shape[1]
    assert wk == min(WINDOW, past_len)

    x = jnp.concatenate([x_prompt.reshape(n_prompt, D), x_sample.reshape(n_sample, D),
                         jnp.zeros((M - n_tok, D), F32)], 0)

    pos = jnp.concatenate([jnp.tile(jnp.arange(S), B), jnp.full((n_sample,), past_len), jnp.zeros((M - n_tok,), I32)])
    half = ROT_DIM // 2
    inv = ROPE_THETA ** (-jnp.arange(half, dtype=F32) / half)
    ang = pos.astype(F32)[:, None] * inv[None, :]
    cos, sin = jnp.cos(ang), jnp.sin(ang)
    zeros_h = jnp.zeros((M, half), F32)
    rest = HEAD_DIM - ROT_DIM
    rope_c = jnp.concatenate([cos, cos, jnp.ones((M, rest), F32)], 1)
    rope_s1 = jnp.concatenate([-sin, zeros_h, jnp.zeros((M, rest), F32)], 1)
    rope_s2 = jnp.concatenate([zeros_h, sin, jnp.zeros((M, rest), F32)], 1)
    rope_extras = [(t, (ROW_TILE, HEAD_DIM), lambda j, i: (i, 0)) for t in (rope_c, rope_s1, rope_s2)]

    def residual_mm(xin, w, layer, res, tm, name):
        n = w.shape[2]
        return _ws_matmul(xin, [(w, layer)], _ep_residual, [(n, F32)], tm=tm, tn=COL_TILE, n_cols=n,
                          extras=[(res, (tm, COL_TILE), lambda j, i: (i, j))], name=name)[0]

    gm_v = []
    kv = None
    for l in range(DEPTH):
        h = _rmsnorm(x, norm_mix[l], BF16, tm=ROW_TILE)
        if l < N_A_LAYERS:
            z = _ws_matmul(h, [(gm_w_in, l)], _ep_gelu, [(2 * GM_DIM, BF16)], tm=ROW_TILE, tn=COL_TILE,
                           n_cols=2 * GM_DIM, name="gmlp_in")[0]
            o, v_s = _gm_gate(z, gm_ln_g[l], gm_ln_b[l], gm_w_s[l], gm_b_s[l], n_prompt=n_prompt, n_sample=n_sample)
            gm_v.append(v_s[:n_sample].reshape(DB, DS, GM_DIM))
            x = residual_mm(o, gm_w_out, l, x, ROW_TILE, "gmlp_out")
        else:
            bl = l - N_A_LAYERS
            qc, qr = _ws_matmul(h, [(nsa_w_in, bl)], _ep_q, [(N_HEADS * HEAD_DIM, BF16)] * 2, tm=ROW_TILE,
                                tn=COL_TILE, n_cols=N_HEADS * HEAD_DIM, extras=rope_extras, name="nsa_q")
            wg = nsa_w_in[bl, :, N_HEADS * HEAD_DIM:].reshape(D, N_BRANCH, KV_HEADS, Q_PER_KV)
            wg = jnp.transpose(wg, (0, 2, 1, 3)).reshape(D, KV_HEADS, N_BRANCH * Q_PER_KV)
            wg = jnp.pad(wg, ((0, 0), (0, 0), (0, LANES - N_BRANCH * Q_PER_KV))).reshape(1, D, KV_HEADS * LANES)
            gates = _ws_matmul(h, [(wg, 0)], _ep_sigmoid, [(KV_HEADS * LANES, F32)], tm=ROW_TILE, tn=KV_HEADS * LANES,
                               n_cols=KV_HEADS * LANES, name="nsa_gates")[0]
            o = _attn_prompt(qc, qr, kv['kc_p'], kv['vc_p'], kv['rb'], gates, batch=B, seq=S)
            qc_s = _head_rows(qc[n_prompt:n_tok])
            qr_s = _head_rows(qr[n_prompt:n_tok])
            oc_s, idx_s, val_s = _attn_s1(qc_s, kv['kc_s'], kv['vc_s'], qpos=qpos_s)
            idx = idx_s[:, :SEL_TOP].reshape(DB, KV_HEADS, SEL_TOP)
            ok = val_s[:, :SEL_TOP].reshape(DB, KV_HEADS, SEL_TOP) >= 0.0
            n_past_blk = past_len // SEL_BLOCK
            ppb = PAGE_SIZE // SEL_BLOCK
            ip = jnp.minimum(idx, n_past_blk - 1)
            phys = jnp.take_along_axis(page_table, (ip // ppb).reshape(DB, -1), 1).reshape(ip.shape) * ppb + ip % ppb
            has_new = jnp.any((idx >= n_past_blk) & ok, -1)
            g_s = gates[n_prompt:n_tok].reshape(DB, KV_HEADS, LANES)[:, :, :N_BRANCH * Q_PER_KV]
            g_s = jnp.transpose(g_s.reshape(DB, KV_HEADS, N_BRANCH, Q_PER_KV), (0, 1, 3, 2))
            g_s = jnp.pad(g_s, ((0, 0), (0, 0), (0, 8 - Q_PER_KV), (0, HEAD_DIM - N_BRANCH)))
            o_s = _attn_s2(qr_s, kv['pool_k'], kv['pool_v'], kv['ks_new'], kv['vs_new'], kv['win_k'], kv['win_v'],
                           kv['kw_new'], kv['vw_new'], oc_s, g_s, phys.reshape(-1).astype(I32),
                           idx.reshape(-1).astype(I32), ok.reshape(-1).astype(I32), has_new.reshape(-1).astype(I32),
                           qpos=qpos_s, past_len=past_len)
            o_s = o_s[:, :, :Q_PER_KV].reshape(n_sample, N_HEADS * HEAD_DIM).astype(BF16)
            o = jnp.concatenate([o, o_s, jnp.zeros((M - n_tok, N_HEADS * HEAD_DIM), BF16)], 0)
            x = residual_mm(o, nsa_w_out, bl, x, ROW_TILE, "nsa_out")

        if l % 2 == 0:
            h = _rmsnorm(x, norm_ffn[l], BF16, tm=ROW_TILE)
            hh = _ws_matmul(h, [(ff_w_gate, l // 2), (ff_w_up, l // 2)], _ep_swiglu, [(ff_w_gate.shape[2], BF16)],
                            tm=ROW_TILE, tn=COL_TILE, n_cols=ff_w_gate.shape[2], name="ffn_up")[0]
            x = residual_mm(hh, ff_w_down, l // 2, x, ROW_TILE_DOWN, "ffn_down")
        else:
            x = _moe(x, norm_ffn[l], moe_router[l // 2], moe_w_gate, moe_w_up, moe_w_down, l // 2, n_tok=n_tok)

        if l == N_A_LAYERS - 1:
            hkv = _rmsnorm(x, norm_kv, BF16, tm=ROW_TILE)
            r, rb = _ws_matmul(hkv, [(kv_w.reshape(1, D, 6 * KV_SEC), 0)], _ep_kv,
                               [(6 * KV_SEC, F32), (6 * KV_SEC, BF16)], tm=ROW_TILE, tn=KV_SEC,
                               n_cols=6 * KV_SEC, extras=rope_extras, name="kv_proj")
            pages_p = r.reshape(M // PAGE_SIZE, PAGE_SIZE, 6 * KV_SEC)
            table_p = jnp.arange(n_prompt // PAGE_SIZE, dtype=I32).reshape(B, S // PAGE_SIZE)
            r_s = r[n_prompt:n_tok]
            sec_s = lambda i: r_s[:, i * KV_SEC:(i + 1) * KV_SEC]
            rows2d = lambda c: c.reshape(-1, HEAD_DIM)
            kv = dict(
                rb=rb,
                kc_p=_compress(pages_p, table_p, 0, cmp_pe_k, cmp_w1_k, cmp_w2_k),
                vc_p=_compress(pages_p, table_p, 1, cmp_pe_v, cmp_w1_v, cmp_w2_v),
                kc_s=_compress(rows2d(cache_cmp_k), page_table, 0, cmp_pe_k, cmp_w1_k, cmp_w2_k),
                vc_s=_compress(rows2d(cache_cmp_v), page_table, 0, cmp_pe_v, cmp_w1_v, cmp_w2_v),
                pool_k=rows2d(cache_sel_k), pool_v=rows2d(cache_sel_v),
                ks_new=_head_rows(sec_s(2)), vs_new=_head_rows(sec_s(3)),
                kw_new=_head_rows(sec_s(4)), vw_new=_head_rows(sec_s(5)),
                win_k=rows2d(state_win_k), win_v=rows2d(state_win_v),
            )
            r_p = r[:n_prompt]
            pg = lambda i: r_p[:, i * KV_SEC:(i + 1) * KV_SEC].reshape(B, S // PAGE_SIZE, PAGE_SIZE, KV_HEADS, HEAD_DIM)
            wkp = min(WINDOW, S)
            tail = lambda i: r_p[:, i * KV_SEC:(i + 1) * KV_SEC].reshape(B, S, KV_HEADS, HEAD_DIM)[:, S - wkp:]
            row_s = lambda i: sec_s(i).reshape(DB, DS, KV_HEADS, HEAD_DIM)
            state_p = (pg(0), pg(1), pg(2), pg(3), tail(4), tail(5))
            state_s = (row_s(0), row_s(1), row_s(2), row_s(3),
                       jnp.concatenate([state_win_k, row_s(4)], 1)[:, DS:],
                       jnp.concatenate([state_win_v, row_s(5)], 1)[:, DS:])

    y = _rmsnorm(x, norm_final, F32, tm=ROW_TILE)
    y_prompt = y[:n_prompt].reshape(B, S, D)
    y_sample = y[n_prompt:n_tok].reshape(DB, DS, D)
    return (y_prompt, y_sample) + state_p + state_s + (jnp.stack(gm_v, 0),)
```

```python
import functools

import numpy as np
import jax
import jax.numpy as jnp
from jax import lax
from jax.experimental import pallas as pl
from jax.experimental.pallas import tpu as pltpu

F32 = jnp.float32
BF16 = jnp.bfloat16
I32 = jnp.int32

D_MODEL = 2048
DEPTH = 4
PAGE_SIZE = 128
N_A_LAYERS = DEPTH // 2
GM_CHUNK = 128
GM_DIM = D_MODEL
GM_GROUPS = 16
GM_GW = GM_DIM // GM_GROUPS
N_HEADS = 16
HEAD_DIM = 128
KV_HEADS = 4
Q_PER_KV = N_HEADS // KV_HEADS
ROT_DIM = HEAD_DIM // 4
ROPE_THETA = 500000.0
CMP_LEN = 32
CMP_STRIDE = 16
CMP_R = CMP_LEN // CMP_STRIDE
SEL_BLOCK = 64
SEL_TOP = 16
WINDOW = 512
N_BRANCH = 3
KV_SEC = KV_HEADS * HEAD_DIM
ATTN_SCALE = HEAD_DIM ** -0.5
N_EXPERTS = 8
TOP_K = 2
EPS = 1e-6
NEG = -1e30
FORCE_SCORE = 1e6

LANES = 128
VMEM_LIMIT_BYTES = 56 * 1024 * 1024
ROW_TILE = 832
ROW_TILE_DOWN = 416
COL_TILE = 512
MOE_BLOCK = 256
MOE_SEG_BLOCKS = 10
MOE_FF_TILE = 256
KEY_TILE = 256
Q_TILE = 256
PAGES_PER_STEP = 8


def _cparams(*sem):
    return pltpu.CompilerParams(dimension_semantics=sem, vmem_limit_bytes=VMEM_LIMIT_BYTES)


def _dot(a, b):
    return jnp.dot(a, b, preferred_element_type=F32)


def _dot_nt(a, b):
    return lax.dot_general(a, b, (((1,), (1,)), ((), ())), preferred_element_type=F32)


def _dot3(a, b):
    ah = a.astype(BF16)
    al = (a - ah.astype(F32)).astype(BF16)
    bh = b.astype(BF16)
    bl = (b - bh.astype(F32)).astype(BF16)
    return _dot(ah, bh) + _dot(al, bh) + _dot(ah, bl)


def _ws_body(*refs, nw, n_extra, n_out, epilogue):
    x_ref = refs[0]
    w_refs = refs[1:1 + nw]
    e_refs = refs[1 + nw:1 + nw + n_extra]
    o_refs = refs[1 + nw + n_extra:1 + nw + n_extra + n_out]
    wb_refs = refs[1 + nw + n_extra + n_out:]
    j = pl.program_id(0)

    @pl.when(pl.program_id(1) == 0)
    def _():
        for w_ref, wb in zip(w_refs, wb_refs):
            wb[...] = w_ref[...].astype(BF16)

    accs = [_dot(x_ref[...], wb[...]) for wb in wb_refs]
    outs = epilogue(j, accs, e_refs)
    for o_ref, o in zip(o_refs, outs):
        o_ref[...] = o.astype(o_ref.dtype)


def _ws_matmul(x, ws, epilogue, outs, *, tm, tn, n_cols, extras=(), name):
    M, K = x.shape
    nj = n_cols // tn
    ni = M // tm
    assert nj * tn == n_cols and ni * tm == M
    in_specs = [pl.BlockSpec((tm, K), lambda j, i: (i, 0))]
    in_specs += [pl.BlockSpec((None, K, tn), lambda j, i, layer=layer: (layer, 0, j)) for _, layer in ws]
    ws = [w for w, _ in ws]
    in_specs += [pl.BlockSpec(bs, im) for _, bs, im in extras]
    out_shape = [jax.ShapeDtypeStruct((M, c), dt) for c, dt in outs]
    out_specs = [pl.BlockSpec((tm, c // nj), lambda j, i: (i, j)) for c, _ in outs]
    body = functools.partial(_ws_body, nw=len(ws), n_extra=len(extras), n_out=len(outs), epilogue=epilogue)
    res = pl.pallas_call(
        body, grid=(nj, ni), in_specs=in_specs, out_specs=out_specs, out_shape=out_shape,
        scratch_shapes=[pltpu.VMEM((K, tn), BF16) for _ in ws],
        compiler_params=_cparams("arbitrary", "arbitrary"), name=name,
    )(x, *ws, *[a for a, _, _ in extras])
    return res


def _rope_tile(acc, c_ref, s1_ref, s2_ref):
    c, s1, s2 = c_ref[...], s1_ref[...], s2_ref[...]
    parts = []
    for h in range(acc.shape[1] // HEAD_DIM):
        xh = acc[:, h * HEAD_DIM:(h + 1) * HEAD_DIM]
        parts.append(xh * c + pltpu.roll(xh, HEAD_DIM - ROT_DIM // 2, 1) * s1 + pltpu.roll(xh, ROT_DIM // 2, 1) * s2)
    return jnp.concatenate(parts, 1)


def _ep_gelu(j, accs, e):
    return [jax.nn.gelu(accs[0])]


def _ep_residual(j, accs, e):
    return [e[0][...] + accs[0]]


def _ep_swiglu(j, accs, e):
    return [jax.nn.silu(accs[0]) * accs[1]]


def _ep_sigmoid(j, accs, e):
    return [jax.nn.sigmoid(accs[0])]


def _ep_kv(j, accs, e):
    acc = accs[0]
    roped = _rope_tile(acc, *e)
    r = jnp.where((j == 2) | (j == 4), roped, acc)
    return [r, r]


def _ep_q(j, accs, e):
    acc = accs[0]
    return [acc, _rope_tile(acc, *e)]


def _rms(x, g):
    return (x * lax.rsqrt(jnp.mean(x * x, -1, keepdims=True) + EPS)) * g


def _rms_body(x_ref, g_ref, o_ref):
    o_ref[...] = _rms(x_ref[...], g_ref[...]).astype(o_ref.dtype)


def _rmsnorm(x, g, out_dtype, *, tm, first_row=0, n_rows=None):
    M, D = x.shape
    n_rows = M if n_rows is None else n_rows
    assert n_rows % tm == 0 and first_row % tm == 0
    off = first_row // tm
    return pl.pallas_call(
        _rms_body, grid=(n_rows // tm,),
        in_specs=[pl.BlockSpec((tm, D), lambda i: (i + off, 0)), pl.BlockSpec((1, D), lambda i: (0, 0))],
        out_specs=pl.BlockSpec((tm, D), lambda i: (i, 0)),
        out_shape=jax.ShapeDtypeStruct((n_rows, D), out_dtype),
        compiler_params=_cparams("arbitrary"), name="rmsnorm",
    )(x, g.reshape(1, D))


def _rms_router_body(x_ref, g_ref, r_ref, h_ref, te_ref, tg_ref):
    h = _rms(x_ref[...], g_ref[...])
    h_ref[...] = h
    lane = lax.broadcasted_iota(I32, (1, LANES), 1).astype(F32)
    logits = jnp.where(lane < N_EXPERTS, _dot3(h, r_ref[...]), -jnp.inf)
    m1 = jnp.max(logits, -1, keepdims=True)
    i1 = jnp.min(jnp.where(logits == m1, lane, float(LANES)), -1, keepdims=True)
    rest = jnp.where(lane == i1, -jnp.inf, logits)
    m2 = jnp.max(rest, -1, keepdims=True)
    i2 = jnp.min(jnp.where(rest == m2, lane, float(LANES)), -1, keepdims=True)
    e2 = jnp.exp(m2 - m1)
    den = 1.0 + e2
    te_ref[...] = jnp.where(lane == 0, i1, jnp.where(lane == 1, i2, 0.0)).astype(I32)
    tg_ref[...] = jnp.where(lane == 0, 1.0 / den, jnp.where(lane == 1, e2 / den, 0.0))


def _rmsnorm_router(x, g, router, *, tm):
    M, D = x.shape
    rp = jnp.pad(router, ((0, 0), (0, LANES - N_EXPERTS)))
    return pl.pallas_call(
        _rms_router_body, grid=(M // tm,),
        in_specs=[pl.BlockSpec((tm, D), lambda i: (i, 0)), pl.BlockSpec((1, D), lambda i: (0, 0)),
                  pl.BlockSpec((D, LANES), lambda i: (0, 0))],
        out_specs=[pl.BlockSpec((tm, D), lambda i: (i, 0)), pl.BlockSpec((tm, LANES), lambda i: (i, 0)),
                   pl.BlockSpec((tm, LANES), lambda i: (i, 0))],
        out_shape=[jax.ShapeDtypeStruct((M, D), F32), jax.ShapeDtypeStruct((M, LANES), I32),
                   jax.ShapeDtypeStruct((M, LANES), F32)],
        compiler_params=_cparams("arbitrary"), name="rmsnorm_router",
    )(x, g.reshape(1, D), rp)


def _gm_gate_body(z_ref, lng_ref, lnb_ref, ws_ref, bst_ref, w00_ref, b00_ref, o_ref, vs_ref, *,
                  n_prompt_chunks, n_sample):
    c = pl.program_id(0)
    z = z_ref[...]
    u = z[:, :GM_DIM].astype(F32)
    v = z[:, GM_DIM:].astype(F32)
    mu = jnp.mean(v, -1, keepdims=True)
    vc = v - mu
    vn = (vc * lax.rsqrt(jnp.mean(vc * vc, -1, keepdims=True) + EPS)) * lng_ref[...] + lnb_ref[...]

    @pl.when(c < n_prompt_chunks)
    def _():
        vb = vn.astype(BF16)
        tri = lax.broadcasted_iota(I32, (GM_CHUNK, GM_CHUNK), 0) >= lax.broadcasted_iota(I32, (GM_CHUNK, GM_CHUNK), 1)
        for g in range(GM_GROUPS):
            sl = slice(g * GM_GW, (g + 1) * GM_GW)
            wg = jnp.where(tri, ws_ref[g], 0.0).astype(BF16)
            s = _dot(wg, vb[:, sl]) + bst_ref[:, g:g + 1]
            o_ref[:, sl] = (u[:, sl] * s).astype(o_ref.dtype)

    @pl.when(c >= n_prompt_chunks)
    def _():
        row = lax.broadcasted_iota(I32, (GM_CHUNK, 1), 0)
        s = w00_ref[...] * vn + b00_ref[...]
        o_ref[...] = jnp.where(row < n_sample, u * s, 0.0).astype(o_ref.dtype)
        vs_ref[...] = vn[:vs_ref.shape[0]]


def _gm_gate(z, ln_g, ln_b, w_s, b_s, *, n_prompt, n_sample):
    M = z.shape[0]
    n_pc = n_prompt // GM_CHUNK
    assert n_prompt % GM_CHUNK == 0 and M == (n_pc + 1) * GM_CHUNK and n_sample <= GM_CHUNK
    n_vs = -(-n_sample // 8) * 8
    w00 = jnp.repeat(w_s[:, 0, 0], GM_GW).reshape(1, GM_DIM)
    b00 = jnp.repeat(b_s[:, 0], GM_GW).reshape(1, GM_DIM)
    body = functools.partial(_gm_gate_body, n_prompt_chunks=n_pc, n_sample=n_sample)
    full = lambda *shape: pl.BlockSpec(shape, lambda c: (0,) * len(shape))
    return pl.pallas_call(
        body, grid=(n_pc + 1,),
        in_specs=[pl.BlockSpec((GM_CHUNK, 2 * GM_DIM), lambda c: (c, 0)), full(1, GM_DIM), full(1, GM_DIM),
                  full(GM_GROUPS, GM_CHUNK, GM_CHUNK), full(GM_CHUNK, GM_GROUPS), full(1, GM_DIM), full(1, GM_DIM)],
        out_specs=[pl.BlockSpec((GM_CHUNK, GM_DIM), lambda c: (c, 0)), full(n_vs, GM_DIM)],
        out_shape=[jax.ShapeDtypeStruct((M, GM_DIM), BF16), jax.ShapeDtypeStruct((n_vs, GM_DIM), F32)],
        compiler_params=_cparams("arbitrary"), name="gmlp_gate",
    )(z, ln_g.reshape(1, GM_DIM), ln_b.reshape(1, GM_DIM), w_s, b_s.T, w00, b00)


def _moe_gather_body(rt_ref, nu_ref, h_hbm, o_ref, buf, sem):
    i = pl.program_id(0)
    n_used = nu_ref[0]

    def row_copy(tok, slot, r):
        return pltpu.make_async_copy(h_hbm.at[pl.ds(tok, 1), :], buf.at[slot, pl.ds(r, 1), :], sem.at[slot])

    def start_block(blk, slot):
        def issue(r, c):
            row_copy(rt_ref[blk * MOE_BLOCK + r], slot, r).start()
            return c

        lax.fori_loop(0, MOE_BLOCK, issue, 0, unroll=8)

    @pl.when((i == 0) & (n_used > 0))
    def _():
        start_block(0, 0)

    @pl.when(i + 1 < n_used)
    def _():
        start_block(i + 1, (i + 1) % 2)

    @pl.when(i < n_used)
    def _():
        slot = i % 2

        def wait(r, c):
            row_copy(0, slot, r).wait()
            return c

        lax.fori_loop(0, MOE_BLOCK, wait, 0, unroll=8)
        o_ref[...] = buf[slot].astype(o_ref.dtype)

    @pl.when(i >= n_used)
    def _():
        o_ref[...] = jnp.zeros(o_ref.shape, o_ref.dtype)


def _moe_gather(h, row_tok, n_used, n_blk):
    D = h.shape[1]
    return pl.pallas_call(
        _moe_gather_body,
        grid_spec=pltpu.PrefetchScalarGridSpec(
            num_scalar_prefetch=2, grid=(n_blk,),
            in_specs=[pl.BlockSpec(memory_space=pl.ANY)],
            out_specs=pl.BlockSpec((MOE_BLOCK, D), lambda i, rt, nu: (i, 0)),
            scratch_shapes=[pltpu.VMEM((2, MOE_BLOCK, D), F32), pltpu.SemaphoreType.DMA((2,))]),
        out_shape=jax.ShapeDtypeStruct((n_blk * MOE_BLOCK, D), BF16),
        compiler_params=_cparams("arbitrary"), name="moe_gather",
    )(row_tok, n_used, h)


def _moe_ffn_body(se_ref, sb_ref, sn_ref, ns_ref, nu_ref, xb_hbm, wg_ref, wu_ref, wd_ref, yb_hbm,
                  x_vmem, y_acc, wgb, wub, wdb, sem_in, sem_out, *, n_j, n_seg_max, n_blk):
    s, j = pl.program_id(0), pl.program_id(1)
    nblk = sn_ref[jnp.minimum(s, n_seg_max - 1)]
    row0 = sb_ref[jnp.minimum(s, n_seg_max - 1)] * MOE_BLOCK

    def block_copy(src, src_row, dst, dst_row, sem):
        return pltpu.make_async_copy(src.at[pl.ds(pl.multiple_of(src_row, MOE_BLOCK), MOE_BLOCK), :],
                                     dst.at[pl.ds(pl.multiple_of(dst_row, MOE_BLOCK), MOE_BLOCK), :], sem)

    def for_blocks(lo, hi, fn):
        def body(blk, c):
            fn(blk)
            return c

        lax.fori_loop(lo, hi, body, 0)

    @pl.when(s < ns_ref[0])
    def _():
        @pl.when(j == 0)
        def _():
            for_blocks(0, nblk, lambda blk: block_copy(xb_hbm, row0 + blk * MOE_BLOCK, x_vmem, blk * MOE_BLOCK,
                                                       sem_in).start())
            for_blocks(0, nblk, lambda blk: block_copy(xb_hbm, row0, x_vmem, blk * MOE_BLOCK, sem_in).wait())

        wgb[...] = wg_ref[...].astype(BF16)
        wub[...] = wu_ref[...].astype(BF16)
        wdb[...] = wd_ref[...].astype(BF16)

        def update(blk, n, first):
            rows = pl.ds(pl.multiple_of(blk * MOE_BLOCK, MOE_BLOCK), n * MOE_BLOCK)
            x = x_vmem[rows, :]
            hidden = (jax.nn.silu(_dot(x, wgb[...])) * _dot(x, wub[...])).astype(BF16)
            part = _dot(hidden, wdb[...])
            y_acc[rows, :] = part if first else y_acc[rows, :] + part

        def sweep(first):
            pairs = nblk // 2
            for_blocks(0, pairs, lambda c: update(2 * c, 2, first))

            @pl.when(nblk % 2 == 1)
            def _():
                update(2 * pairs, 1, first)

        @pl.when(j == 0)
        def _():
            sweep(True)

        @pl.when(j > 0)
        def _():
            sweep(False)

        @pl.when(j == n_j - 1)
        def _():
            for_blocks(0, nblk, lambda blk: block_copy(y_acc, blk * MOE_BLOCK, yb_hbm, row0 + blk * MOE_BLOCK,
                                                       sem_out).start())
            for_blocks(0, nblk, lambda blk: block_copy(y_acc, blk * MOE_BLOCK, yb_hbm, row0, sem_out).wait())

    @pl.when((s == n_seg_max) & (j == 0))
    def _():
        y_acc[pl.ds(0, MOE_BLOCK), :] = jnp.zeros((MOE_BLOCK, y_acc.shape[1]), F32)
        for_blocks(nu_ref[0], n_blk, lambda blk: block_copy(y_acc, 0, yb_hbm, blk * MOE_BLOCK, sem_out).start())
        for_blocks(nu_ref[0], n_blk, lambda blk: block_copy(y_acc, 0, yb_hbm, 0, sem_out).wait())


def _moe_ffn(xb, wg, wu, wd, layer, seg_e, seg_blk, seg_nblk, n_seg, n_used, *, n_seg_max):
    M, D = xb.shape
    F = wg.shape[3]
    n_j = F // MOE_FF_TILE
    assert n_j * MOE_FF_TILE == F
    n_blk = M // MOE_BLOCK

    def w_index(s, j, se, sb, sn, ns, nu):
        live = s < ns[0]
        return se[jnp.minimum(s, ns[0] - 1)], jnp.where(live, j, n_j - 1)

    def up_map(s, j, *pref):
        e, jj = w_index(s, j, *pref)
        return layer, e, 0, jj

    def down_map(s, j, *pref):
        e, jj = w_index(s, j, *pref)
        return layer, e, jj, 0

    seg_rows = MOE_SEG_BLOCKS * MOE_BLOCK
    body = functools.partial(_moe_ffn_body, n_j=n_j, n_seg_max=n_seg_max, n_blk=n_blk)
    return pl.pallas_call(
        body,
        grid_spec=pltpu.PrefetchScalarGridSpec(
            num_scalar_prefetch=5, grid=(n_seg_max + 1, n_j),
            in_specs=[pl.BlockSpec(memory_space=pl.ANY),
                      pl.BlockSpec((None, None, D, MOE_FF_TILE), up_map),
                      pl.BlockSpec((None, None, D, MOE_FF_TILE), up_map),
                      pl.BlockSpec((None, None, MOE_FF_TILE, D), down_map)],
            out_specs=pl.BlockSpec(memory_space=pl.ANY),
            scratch_shapes=[pltpu.VMEM((seg_rows, D), BF16), pltpu.VMEM((seg_rows, D), F32),
                            pltpu.VMEM((D, MOE_FF_TILE), BF16), pltpu.VMEM((D, MOE_FF_TILE), BF16),
                            pltpu.VMEM((MOE_FF_TILE, D), BF16),
                            pltpu.SemaphoreType.DMA(()), pltpu.SemaphoreType.DMA(())]),
        out_shape=jax.ShapeDtypeStruct((M, D), F32),
        compiler_params=_cparams("arbitrary", "arbitrary"), name="moe_ffn",
    )(seg_e, seg_blk, seg_nblk, n_seg, n_used, xb, wg, wu, wd)


def _moe_combine_body(d0_ref, d1_ref, x_ref, g_ref, yb_hbm, o_ref, b0, b1, sem, *, tm):
    i = pl.program_id(0)

    def row_copy(src_row, buf, slot, r):
        return pltpu.make_async_copy(yb_hbm.at[pl.ds(src_row, 1), :], buf.at[slot, pl.ds(r, 1), :], sem.at[slot])

    def start_tile(t, slot):
        def issue(r, c):
            row_copy(d0_ref[t * tm + r], b0, slot, r).start()
            row_copy(d1_ref[t * tm + r], b1, slot, r).start()
            return c

        lax.fori_loop(0, tm, issue, 0, unroll=8)

    @pl.when(i == 0)
    def _():
        start_tile(0, 0)

    @pl.when(i + 1 < pl.num_programs(0))
    def _():
        start_tile(i + 1, (i + 1) % 2)

    slot = i % 2

    def wait(r, c):
        row_copy(0, b0, slot, r).wait()
        row_copy(0, b1, slot, r).wait()
        return c

    lax.fori_loop(0, tm, wait, 0, unroll=8)
    g = g_ref[...]
    o_ref[...] = x_ref[...] + (g[:, 0:1] * b0[slot] + g[:, 1:2] * b1[slot])


def _moe_combine(x, gates, yb, d0, d1, *, tm):
    M, D = x.shape
    body = functools.partial(_moe_combine_body, tm=tm)
    return pl.pallas_call(
        body,
        grid_spec=pltpu.PrefetchScalarGridSpec(
            num_scalar_prefetch=2, grid=(M // tm,),
            in_specs=[pl.BlockSpec((tm, D), lambda i, a, b: (i, 0)), pl.BlockSpec((tm, LANES), lambda i, a, b: (i, 0)),
                      pl.BlockSpec(memory_space=pl.ANY)],
            out_specs=pl.BlockSpec((tm, D), lambda i, a, b: (i, 0)),
            scratch_shapes=[pltpu.VMEM((2, tm, D), F32), pltpu.VMEM((2, tm, D), F32), pltpu.SemaphoreType.DMA((2,))]),
        out_shape=jax.ShapeDtypeStruct((M, D), F32),
        compiler_params=_cparams("arbitrary"), name="moe_combine",
    )(d0, d1, x, gates, yb)


def _moe(x, g_norm, router, wg, wu, wd, layer, *, n_tok):
    M, D = x.shape
    h, te, tg = _rmsnorm_router(x, g_norm, router, tm=ROW_TILE)
    a = n_tok * TOP_K
    n_blk = -(-a // MOE_BLOCK) + N_EXPERTS
    e_flat = te[:n_tok, :TOP_K].reshape(a)
    onehot = (e_flat[:, None] == jnp.arange(N_EXPERTS, dtype=I32)[None, :]).astype(I32)
    csum = jnp.cumsum(onehot, 0)
    pos = jnp.take_along_axis(csum, e_flat[:, None], 1)[:, 0] - 1
    counts = csum[-1]
    padded = (counts + MOE_BLOCK - 1) // MOE_BLOCK * MOE_BLOCK
    end_pad = jnp.cumsum(padded)
    start_pad = end_pad - padded
    dest = (start_pad[e_flat] + pos).astype(I32)
    tok = jnp.arange(a, dtype=I32) // TOP_K
    row_tok = jnp.full((n_blk * MOE_BLOCK,), n_tok, I32).at[dest].set(tok)
    n_used = (end_pad[-1] // MOE_BLOCK).astype(I32).reshape(1)
    n_seg_max = n_blk // MOE_SEG_BLOCKS + N_EXPERTS + 1
    e_blocks = padded // MOE_BLOCK
    e_segs = (e_blocks + MOE_SEG_BLOCKS - 1) // MOE_SEG_BLOCKS
    seg_end = jnp.cumsum(e_segs)
    s_ids = jnp.arange(n_seg_max, dtype=I32)
    seg_e = jnp.minimum(jnp.sum((seg_end[None, :] <= s_ids[:, None]).astype(I32), 1), N_EXPERTS - 1)
    local = s_ids - (seg_end - e_segs)[seg_e]
    seg_blk = (start_pad // MOE_BLOCK)[seg_e] + local * MOE_SEG_BLOCKS
    seg_nblk = jnp.where(s_ids < seg_end[-1], jnp.clip(e_blocks[seg_e] - local * MOE_SEG_BLOCKS, 0, MOE_SEG_BLOCKS), 0)
    d = jnp.pad(dest.reshape(n_tok, TOP_K), ((0, M - n_tok), (0, 0)))
    row = jnp.arange(M, dtype=I32)[:, None]
    gates = jnp.where(row < n_tok, tg, 0.0)

    xb = _moe_gather(h, row_tok, n_used, n_blk)
    yb = _moe_ffn(xb, wg, wu, wd, layer, seg_e.astype(I32), seg_blk.astype(I32), seg_nblk.astype(I32),
                  seg_end[-1:].astype(I32), n_used, n_seg_max=n_seg_max)
    return _moe_combine(x, gates, yb, d[:, 0], d[:, 1], tm=ROW_TILE_DOWN)


def _cmp_z_body(pt_ref, *refs, heads_interleaved):
    n_in = len(refs) - 2
    page_refs = refs[:n_in]
    w_ref = refs[n_in]
    o_ref = refs[n_in + 1]
    n_ch = PAGE_SIZE // CMP_STRIDE
    cols = []
    for s in range(CMP_STRIDE):
        if heads_interleaved:
            rows = [pg[pl.ds(KV_HEADS * s + k, n_ch, stride=KV_HEADS * CMP_STRIDE), :]
                    for k in range(KV_HEADS) for pg in page_refs]
        else:
            rows = [pg[pl.ds(s, n_ch, stride=CMP_STRIDE), :] for pg in page_refs]
        cols.append(jnp.concatenate(rows, 0).astype(BF16))
    acc = _dot(jnp.concatenate(cols, 1), w_ref[...])
    m = PAGES_PER_STEP * n_ch
    for k in range(KV_HEADS):
        o_ref[k] = acc[k * m:(k + 1) * m]


def _cmp_z(pages, table, sec, w1cat):
    nb, npg = table.shape
    assert npg % PAGES_PER_STEP == 0
    n_ch = PAGE_SIZE // CMP_STRIDE
    m = PAGES_PER_STEP * n_ch
    interleaved = pages.ndim == 2
    if interleaved:
        specs = [pl.BlockSpec((PAGE_SIZE * KV_HEADS, HEAD_DIM),
                              lambda b, g, pt, q=q: (pt[b, g * PAGES_PER_STEP + q], 0)) for q in range(PAGES_PER_STEP)]
    else:
        specs = [pl.BlockSpec((None, PAGE_SIZE, HEAD_DIM),
                              lambda b, g, pt, k=k, q=q: (pt[b, g * PAGES_PER_STEP + q], 0, sec * KV_HEADS + k))
                 for k in range(KV_HEADS) for q in range(PAGES_PER_STEP)]
    return pl.pallas_call(
        functools.partial(_cmp_z_body, heads_interleaved=interleaved),
        grid_spec=pltpu.PrefetchScalarGridSpec(
            num_scalar_prefetch=1, grid=(nb, npg // PAGES_PER_STEP),
            in_specs=specs + [pl.BlockSpec((CMP_STRIDE * HEAD_DIM, CMP_R * HEAD_DIM), lambda b, g, pt: (0, 0))],
            out_specs=pl.BlockSpec((None, KV_HEADS, m, CMP_R * HEAD_DIM), lambda b, g, pt: (b, 0, g, 0))),
        out_shape=jax.ShapeDtypeStruct((nb, KV_HEADS, npg * n_ch, CMP_R * HEAD_DIM), F32),
        compiler_params=_cparams("arbitrary", "arbitrary"), name="cmp_z",
    )(table, *([pages] * len(specs)), w1cat)


def _cmp_fin_body(z_ref, pe_ref, w1_ref, w2_ref, o_ref):
    z = z_ref[...]
    n = z.shape[0]
    pe = jnp.broadcast_to(pe_ref[...], (8, pe_ref.shape[1])).astype(BF16)
    pre = _dot(pe, w1_ref[...].astype(BF16))[0:1]
    pre = pre + z[:, :HEAD_DIM]
    pre = pre + pltpu.roll(z[:, HEAD_DIM:], n - 1, 0)
    o_ref[...] = _dot(jax.nn.gelu(pre).astype(BF16), w2_ref[...].astype(BF16)).astype(o_ref.dtype)


def _cmp_fin(z, pe, w1, w2):
    nb, _, n, _ = z.shape
    return pl.pallas_call(
        _cmp_fin_body, grid=(nb, KV_HEADS),
        in_specs=[pl.BlockSpec((None, None, n, CMP_R * HEAD_DIM), lambda b, k: (b, k, 0, 0)),
                  pl.BlockSpec((1, CMP_LEN * HEAD_DIM), lambda b, k: (0, 0)),
                  pl.BlockSpec((CMP_LEN * HEAD_DIM, HEAD_DIM), lambda b, k: (0, 0)),
                  pl.BlockSpec((HEAD_DIM, HEAD_DIM), lambda b, k: (0, 0))],
        out_specs=pl.BlockSpec((None, None, n, HEAD_DIM), lambda b, k: (b, k, 0, 0)),
        out_shape=jax.ShapeDtypeStruct((nb, KV_HEADS, n, HEAD_DIM), BF16),
        compiler_params=_cparams("arbitrary", "arbitrary"), name="cmp_fin",
    )(z, pe.reshape(1, -1), w1, w2)


def _w1cat(w1):
    w = w1.reshape(CMP_R, CMP_STRIDE, HEAD_DIM, HEAD_DIM)
    return jnp.transpose(w, (1, 2, 0, 3)).reshape(CMP_STRIDE * HEAD_DIM, CMP_R * HEAD_DIM).astype(BF16)


def _compress(pages, table, sec, pe, w1, w2):
    return _cmp_fin(_cmp_z(pages, table, sec, _w1cat(w1)), pe, w1, w2)


def _overlap_matrix(nc, ns, rows, cols):
    cs = np.arange(nc) * CMP_STRIDE
    ss = np.arange(ns) * SEL_BLOCK
    ov = np.minimum(cs[:, None] + CMP_LEN, ss[None, :] + SEL_BLOCK) - np.maximum(cs[:, None], ss[None, :])
    ov = np.clip(ov, 0, None).astype(np.float32) / np.float32(CMP_LEN)
    out = np.zeros((rows, cols), np.float32)
    out[:nc, :ns] = ov
    return jnp.asarray(out)


def _attn_prompt_body(qc_ref, qr_ref, kc_ref, vc_ref, ks_ref, vs_ref, kw_ref, vw_ref, g_ref, ovt_ref, ext_ref, o_ref,
                      allow_ref, vct_ref, vst_ref, vwt_ref, q4_ref, s_ref, ocmp_ref, osel_ref, acc_ref, *,
                      seq, n_cmp, n_sel):
    qi = pl.program_id(2)
    qpos = qi * Q_TILE + lax.broadcasted_iota(I32, (1, Q_TILE), 1)
    row = lax.broadcasted_iota(I32, (LANES, 1), 0)

    @pl.when(qi == 0)
    def _():
        vct_ref[...] = vc_ref[...].astype(F32).T.astype(BF16)
        vst_ref[...] = vs_ref[...].astype(F32).T.astype(BF16)
        vwt_ref[...] = vw_ref[...].astype(F32).T.astype(BF16)

    cmask = (row * CMP_STRIDE + (CMP_LEN - 1) <= qpos) & (row < n_cmp)
    kc = kc_ref[...]
    psum = jnp.zeros((LANES, Q_TILE), F32)
    for p in range(Q_PER_KV):
        s = _dot_nt(kc, qc_ref[:, p * HEAD_DIM:(p + 1) * HEAD_DIM]) * ATTN_SCALE
        sm = jnp.where(cmask, s, NEG)
        e = jnp.exp(sm - jnp.max(sm, 0, keepdims=True))
        pr = jnp.where(cmask, e / jnp.sum(e, 0, keepdims=True), 0.0)
        psum = psum + pr
        ocmp_ref[p] = _dot(vct_ref[...], pr.astype(BF16))

    n_sel_pad = -(-n_sel // 8) * 8
    imp = _dot3(ovt_ref[...], psum)[:n_sel_pad]
    blk = row[:n_sel_pad]
    qblk = qpos >> 6
    valid = (blk * SEL_BLOCK <= qpos) & (blk < n_sel)
    forced = (blk == 0) | (blk == qblk) | (blk == qblk - 1)
    score = jnp.where(forced & valid, FORCE_SCORE, jnp.where(valid, imp, -1.0))
    rank = jnp.zeros(score.shape, F32)
    for j in range(n_sel):
        cj = score[j:j + 1, :]
        beats = (cj > score) | ((cj == score) & (blk > j))
        rank = rank + jnp.where(beats, 1.0, 0.0)
    sel = jnp.where((rank < SEL_TOP) & (score >= 0.0), 1.0, 0.0)
    sel = jnp.concatenate([sel, jnp.zeros((LANES - n_sel_pad, Q_TILE), F32)], 0).astype(BF16)
    selx = _dot(ext_ref[...], sel)
    kpos_all = lax.broadcasted_iota(I32, (seq, 1), 0)
    allow_ref[...] = jnp.where((selx > 0.5) & (kpos_all <= qpos), 1.0, 0.0)

    q4_ref[...] = jnp.concatenate([qr_ref[:, p * HEAD_DIM:(p + 1) * HEAD_DIM] for p in range(Q_PER_KV)], 0)

    def flash(k_ref, vt_ref, lo, hi, mask_fn):
        acc_ref[...] = jnp.zeros(acc_ref.shape, F32)
        last = seq // KEY_TILE - 1

        def scores(kt):
            start = pl.multiple_of(kt * KEY_TILE, KEY_TILE)
            return _dot_nt(k_ref[pl.ds(start, KEY_TILE), :], q4_ref[...])

        s_ref[lo % 2] = scores(lo)

        def body(kt, carry):
            m, l = carry
            start = pl.multiple_of(kt * KEY_TILE, KEY_TILE)
            s = s_ref[kt % 2] * ATTN_SCALE
            s_ref[(kt + 1) % 2] = scores(jnp.minimum(kt + 1, last))
            mask = mask_fn(start)
            s = jnp.where(jnp.concatenate([mask] * Q_PER_KV, 1), s, NEG)
            m_new = jnp.maximum(m, jnp.max(s, 0, keepdims=True))
            alpha = jnp.exp(m - m_new)
            e = jnp.exp(s - m_new)
            l = alpha * l + jnp.sum(e, 0, keepdims=True)
            acc_ref[...] = alpha * acc_ref[...] + _dot(vt_ref[:, pl.ds(start, KEY_TILE)], e.astype(BF16))
            return m_new, l

        init = (jnp.full((1, Q_PER_KV * Q_TILE), NEG, F32), jnp.zeros((1, Q_PER_KV * Q_TILE), F32))
        return lax.fori_loop(lo, hi, body, init)[1]

    hi = (qi * Q_TILE + Q_TILE - 1) // KEY_TILE + 1
    l = flash(ks_ref, vst_ref, 0, hi, lambda start: allow_ref[pl.ds(start, KEY_TILE), :] > 0.5)
    osel_ref[...] = acc_ref[...] / l

    def win_mask(start):
        kp = start + lax.broadcasted_iota(I32, (KEY_TILE, 1), 0)
        return (kp <= qpos) & (kp > qpos - WINDOW)

    l = flash(kw_ref, vwt_ref, jnp.maximum(qi * Q_TILE - (WINDOW - 1), 0) // KEY_TILE, hi, win_mask)
    owin = acc_ref[...] / l

    gt = g_ref[...].T
    for p in range(Q_PER_KV):
        sl = slice(p * Q_TILE, (p + 1) * Q_TILE)
        ot = (gt[p:p + 1] * ocmp_ref[p] + gt[Q_PER_KV + p:Q_PER_KV + p + 1] * osel_ref[:, sl]
              + gt[2 * Q_PER_KV + p:2 * Q_PER_KV + p + 1] * owin[:, sl])
        o_ref[:, p * HEAD_DIM:(p + 1) * HEAD_DIM] = ot.T.astype(o_ref.dtype)


def _attn_prompt(qc, qr, kc, vc, rb, gates, *, batch, seq):
    nq = seq // Q_TILE
    n_chunk = seq // CMP_STRIDE
    n_cmp = n_chunk - CMP_R + 1
    n_sel = seq // SEL_BLOCK
    assert n_chunk == LANES and n_sel <= LANES and seq % KEY_TILE == 0
    ovt = _overlap_matrix(n_cmp, n_sel, LANES, LANES).T
    ext = np.zeros((seq, LANES), np.float32)
    ext[np.arange(seq), np.arange(seq) // SEL_BLOCK] = 1.0
    ext = jnp.asarray(ext, BF16)
    qspec = pl.BlockSpec((Q_TILE, KV_SEC), lambda b, g, q: (b * nq + q, g))
    cspec = pl.BlockSpec((None, None, n_chunk, HEAD_DIM), lambda b, g, q: (b, g, 0, 0))
    kvspec = lambda sec: pl.BlockSpec((seq, HEAD_DIM), lambda b, g, q: (b, sec * KV_HEADS + g))
    body = functools.partial(_attn_prompt_body, seq=seq, n_cmp=n_cmp, n_sel=n_sel)
    return pl.pallas_call(
        body, grid=(batch, KV_HEADS, nq),
        in_specs=[qspec, qspec, cspec, cspec, kvspec(2), kvspec(3), kvspec(4), kvspec(5),
                  pl.BlockSpec((Q_TILE, LANES), lambda b, g, q: (b * nq + q, g)),
                  pl.BlockSpec((LANES, LANES), lambda b, g, q: (0, 0)),
                  pl.BlockSpec((seq, LANES), lambda b, g, q: (0, 0))],
        out_specs=qspec,
        out_shape=jax.ShapeDtypeStruct((batch * seq, N_HEADS * HEAD_DIM), BF16),
        scratch_shapes=[pltpu.VMEM((seq, Q_TILE), F32), pltpu.VMEM((HEAD_DIM, n_chunk), BF16),
                        pltpu.VMEM((HEAD_DIM, seq), BF16), pltpu.VMEM((HEAD_DIM, seq), BF16),
                        pltpu.VMEM((Q_PER_KV * Q_TILE, HEAD_DIM), BF16),
                        pltpu.VMEM((2, KEY_TILE, Q_PER_KV * Q_TILE), F32),
                        pltpu.VMEM((Q_PER_KV, HEAD_DIM, Q_TILE), F32), pltpu.VMEM((HEAD_DIM, Q_PER_KV * Q_TILE), F32),
                        pltpu.VMEM((HEAD_DIM, Q_PER_KV * Q_TILE), F32)],
        compiler_params=_cparams("arbitrary", "arbitrary", "arbitrary"), name="attn_prompt",
    )(qc, qr, kc, vc, rb, rb, rb, rb, gates, ovt, ext)


def _attn_s1_body(q_ref, kc_ref, vc_ref, ov_ref, o_ref, idx_ref, val_ref, sc_ref, *, qpos, n_cmp, n_sel):
    step = pl.program_id(0) * pl.num_programs(1) + pl.program_id(1)
    n_steps = pl.num_programs(0) * pl.num_programs(1)
    rows = q_ref.shape[0]
    n = kc_ref.shape[0]
    col = lax.broadcasted_iota(I32, (1, n), 1)
    cmask = (col * CMP_STRIDE + (CMP_LEN - 1) <= qpos) & (col < n_cmp)
    s = _dot_nt(q_ref[...], kc_ref[...]) * ATTN_SCALE
    sm = jnp.where(cmask, s, NEG)
    e = jnp.exp(sm - jnp.max(sm, -1, keepdims=True))
    pr = jnp.where(cmask, e / jnp.sum(e, -1, keepdims=True), 0.0)
    o_ref[...] = _dot(pr.astype(BF16), vc_ref[...])
    head = lax.broadcasted_iota(I32, (rows, 1), 0)
    psum = jnp.sum(jnp.where(head < Q_PER_KV, pr, 0.0), 0, keepdims=True)
    imp = _dot3(jnp.broadcast_to(psum, (rows, n)), ov_ref[...])[0:1]
    w = ov_ref.shape[1]
    blk = lax.broadcasted_iota(I32, (1, w), 1)
    qblk = qpos // SEL_BLOCK
    valid = (blk * SEL_BLOCK <= qpos) & (blk < n_sel)
    forced = (blk == 0) | (blk == qblk) | (blk == qblk - 1)
    low = -3e38
    sc_ref[pl.ds(step, 1), :] = jnp.where(
        blk < n_sel, jnp.where(forced & valid, FORCE_SCORE, jnp.where(valid, imp, -1.0)), low)

    @pl.when(step == n_steps - 1)
    def _():
        score = sc_ref[...]
        blkf = blk.astype(F32)
        lane = lax.broadcasted_iota(I32, (1, LANES), 1)
        idx_rows = jnp.zeros(idx_ref.shape, F32)
        val_rows = jnp.full(val_ref.shape, -1.0, F32)
        for t in range(SEL_TOP):
            m = jnp.max(score, -1, keepdims=True)
            it = jnp.min(jnp.where(score == m, blkf, float(w)), -1, keepdims=True)
            idx_rows = jnp.where(lane == t, it, idx_rows)
            val_rows = jnp.where(lane == t, m, val_rows)
            score = jnp.where(blkf == it, low, score)
        idx_ref[...] = idx_rows.astype(I32)
        val_ref[...] = val_rows


def _attn_s1(qc, kc, vc, *, qpos):
    nb, _, rows, _ = qc.shape
    n_chunk = kc.shape[2]
    t_all = qpos + 1
    n_cmp = t_all // CMP_STRIDE - CMP_R + 1
    n_sel = -(-t_all // SEL_BLOCK)
    assert n_cmp <= n_chunk and n_sel >= SEL_TOP
    w = -(-n_sel // LANES) * LANES
    ov = _overlap_matrix(n_cmp, n_sel, n_chunk, w)
    body = functools.partial(_attn_s1_body, qpos=qpos, n_cmp=n_cmp, n_sel=n_sel)
    hspec = pl.BlockSpec((None, None, rows, HEAD_DIM), lambda b, g: (b, g, 0, 0))
    cspec = pl.BlockSpec((None, None, n_chunk, HEAD_DIM), lambda b, g: (b, g, 0, 0))
    n_rows = nb * KV_HEADS
    assert n_rows % 8 == 0
    pick_spec = pl.BlockSpec((n_rows, LANES), lambda b, g: (0, 0))
    return pl.pallas_call(
        body, grid=(nb, KV_HEADS),
        in_specs=[hspec, cspec, cspec, pl.BlockSpec((n_chunk, w), lambda b, g: (0, 0))],
        out_specs=[hspec, pick_spec, pick_spec],
        out_shape=[jax.ShapeDtypeStruct((nb, KV_HEADS, rows, HEAD_DIM), F32),
                   jax.ShapeDtypeStruct((n_rows, LANES), I32),
                   jax.ShapeDtypeStruct((n_rows, LANES), F32)],
        scratch_shapes=[pltpu.VMEM((n_rows, w), F32)],
        compiler_params=_cparams("arbitrary", "arbitrary"), name="attn_sample_cmp",
    )(qc, kc, vc, ov)


def _attn_s2_body(phys_ref, idx_ref, ok_ref, new_ref, q_ref, *refs, qpos, n_past_blk, past_len):
    kb_refs = refs[:SEL_TOP]
    vb_refs = refs[SEL_TOP:2 * SEL_TOP]
    kn_ref, vn_ref, wk_ref, wv_ref, kwn_ref, vwn_ref, oc_ref, g_ref, o_ref = refs[2 * SEL_TOP:]
    b, g = pl.program_id(0), pl.program_id(1)
    q = q_ref[...]

    def attend(s, v, sn, vn):
        m = jnp.maximum(jnp.max(s, -1, keepdims=True), jnp.max(sn, -1, keepdims=True))
        e, en = jnp.exp(s - m), jnp.exp(sn - m)
        den = jnp.sum(e, -1, keepdims=True) + jnp.sum(en, -1, keepdims=True)
        return (_dot(e.astype(BF16), v) + _dot(en.astype(BF16), vn)) / den

    def head_rows(ref, n):
        return ref[pl.ds(g, n, stride=KV_HEADS), :].astype(BF16)

    wk = wk_ref.shape[0] // KV_HEADS
    kp = (past_len - wk) + lax.broadcasted_iota(I32, (1, wk), 1)
    s = jnp.where((kp <= qpos) & (kp > qpos - WINDOW), _dot_nt(q, head_rows(wk_ref, wk)) * ATTN_SCALE, NEG)
    sn = _dot_nt(q, kwn_ref[...].astype(BF16)) * ATTN_SCALE
    coln = lax.broadcasted_iota(I32, (1, sn.shape[1]), 1)
    o_win = attend(s, head_rows(wv_ref, wk), jnp.where(coln == 0, sn, NEG), vwn_ref[...].astype(BF16))

    kcat = jnp.concatenate([head_rows(r, SEL_BLOCK) for r in kb_refs], 0)
    vcat = jnp.concatenate([head_rows(r, SEL_BLOCK) for r in vb_refs], 0)
    lane = lax.broadcasted_iota(I32, (1, SEL_TOP * SEL_BLOCK), 1)
    limit = jnp.full(lane.shape, -1, I32)
    for kk in range(SEL_TOP):
        t = (b * KV_HEADS + g) * SEL_TOP + kk
        lim = jnp.where((ok_ref[t] == 1) & (idx_ref[t] < n_past_blk), qpos - idx_ref[t] * SEL_BLOCK, -1)
        limit = jnp.where((lane >> 6) == kk, lim, limit)
    s = jnp.where((lane & (SEL_BLOCK - 1)) <= limit, _dot_nt(q, kcat) * ATTN_SCALE, NEG)
    sn = _dot_nt(q, kn_ref[...].astype(BF16)) * ATTN_SCALE
    sn = jnp.where(coln < new_ref[b * KV_HEADS + g], sn, NEG)
    o_sel = attend(s, vcat, sn, vn_ref[...].astype(BF16))

    gt = g_ref[...]
    o_ref[...] = gt[:, 0:1] * oc_ref[...] + gt[:, 1:2] * o_sel + gt[:, 2:3] * o_win


def _attn_s2(qr, pool_k, pool_v, kn, vn, win_k, win_v, kwn, vwn, o_cmp, gates, phys, idx, ok, has_new, *,
             qpos, past_len):
    nb, _, rows, _ = qr.shape
    wk = win_k.shape[0] // (nb * KV_HEADS)
    assert SEL_BLOCK == 64
    hspec = pl.BlockSpec((None, None, rows, HEAD_DIM), lambda b, g, *_: (b, g, 0, 0))
    bspec = lambda k: pl.BlockSpec((SEL_BLOCK * KV_HEADS, HEAD_DIM),
                                   lambda b, g, ph, *_: (ph[(b * KV_HEADS + g) * SEL_TOP + k], 0))
    wspec = pl.BlockSpec((wk * KV_HEADS, HEAD_DIM), lambda b, g, *_: (b, 0))
    blocks = [bspec(k) for k in range(SEL_TOP)]
    body = functools.partial(_attn_s2_body, qpos=qpos, n_past_blk=past_len // SEL_BLOCK, past_len=past_len)
    return pl.pallas_call(
        body,
        grid_spec=pltpu.PrefetchScalarGridSpec(
            num_scalar_prefetch=4, grid=(nb, KV_HEADS),
            in_specs=[hspec] + blocks + blocks + [hspec, hspec, wspec, wspec, hspec, hspec, hspec, hspec],
            out_specs=hspec),
        out_shape=jax.ShapeDtypeStruct((nb, KV_HEADS, rows, HEAD_DIM), F32),
        compiler_params=_cparams("arbitrary", "arbitrary"), name="attn_sample_sel_win",
    )(phys, idx, ok, has_new, qr, *([pool_k] * SEL_TOP), *([pool_v] * SEL_TOP), kn, vn, win_k, win_v, kwn, vwn,
      o_cmp, gates)


def _head_rows(a, rows=8):
    nb = a.shape[0]
    p = a.shape[1] // (KV_HEADS * HEAD_DIM)
    a = a.reshape(nb, KV_HEADS, p, HEAD_DIM)
    return jnp.pad(a, ((0, 0), (0, 0), (0, rows - p), (0, 0)))


def kernel(x_prompt, x_sample, cache_cmp_k, cache_cmp_v, cache_sel_k, cache_sel_v, state_win_k, state_win_v, page_table, norm_mix, norm_ffn, norm_kv, norm_final, gm_w_in, gm_ln_g, gm_ln_b, gm_w_s, gm_b_s, gm_w_out, nsa_w_in, nsa_w_out, kv_w, cmp_pe_k, cmp_w1_k, cmp_w2_k, cmp_pe_v, cmp_w1_v, cmp_w2_v, ff_w_gate, ff_w_up, ff_w_down, moe_router, moe_w_gate, moe_w_up, moe_w_down):
    B, S, D = x_prompt.shape
    DB, DS, _ = x_sample.shape
    assert DS == 1 and D == D_MODEL and S % KEY_TILE == 0
    n_prompt, n_sample = B * S, DB * DS
    n_tok = n_prompt + n_sample
    M = n_prompt + GM_CHUNK
    assert M % ROW_TILE == 0 and M % ROW_TILE_DOWN == 0 and n_sample <= 8
    n_pool = cache_cmp_k.shape[0]
    past_len = page_table.shape[1] * PAGE_SIZE
    qpos_s = past_len
    wk = state_win_k.shape[1]
    assert wk == min(WINDOW, past_len)

    x = jnp.concatenate([x_prompt.reshape(n_prompt, D), x_sample.reshape(n_sample, D),
                         jnp.zeros((M - n_tok, D), F32)], 0)

    pos = jnp.concatenate([jnp.tile(jnp.arange(S), B), jnp.full((n_sample,), past_len), jnp.zeros((M - n_tok,), I32)])
    half = ROT_DIM // 2
    inv = ROPE_THETA ** (-jnp.arange(half, dtype=F32) / half)
    ang = pos.astype(F32)[:, None] * inv[None, :]
    cos, sin = jnp.cos(ang), jnp.sin(ang)
    zeros_h = jnp.zeros((M, half), F32)
    rest = HEAD_DIM - ROT_DIM
    rope_c = jnp.concatenate([cos, cos, jnp.ones((M, rest), F32)], 1)
    rope_s1 = jnp.concatenate([-sin, zeros_h, jnp.zeros((M, rest), F32)], 1)
    rope_s2 = jnp.concatenate([zeros_h, sin, jnp.zeros((M, rest), F32)], 1)
    rope_extras = [(t, (ROW_TILE, HEAD_DIM), lambda j, i: (i, 0)) for t in (rope_c, rope_s1, rope_s2)]

    def residual_mm(xin, w, layer, res, tm, name):
        n = w.shape[2]
        return _ws_matmul(xin, [(w, layer)], _ep_residual, [(n, F32)], tm=tm, tn=COL_TILE, n_cols=n,
                          extras=[(res, (tm, COL_TILE), lambda j, i: (i, j))], name=name)[0]

    gm_v = []
    kv = None
    for l in range(DEPTH):
        h = _rmsnorm(x, norm_mix[l], BF16, tm=ROW_TILE)
        if l < N_A_LAYERS:
            z = _ws_matmul(h, [(gm_w_in, l)], _ep_gelu, [(2 * GM_DIM, BF16)], tm=ROW_TILE, tn=COL_TILE,
                           n_cols=2 * GM_DIM, name="gmlp_in")[0]
            o, v_s = _gm_gate(z, gm_ln_g[l], gm_ln_b[l], gm_w_s[l], gm_b_s[l], n_prompt=n_prompt, n_sample=n_sample)
            gm_v.append(v_s[:n_sample].reshape(DB, DS, GM_DIM))
            x = residual_mm(o, gm_w_out, l, x, ROW_TILE, "gmlp_out")
        else:
            bl = l - N_A_LAYERS
            qc, qr = _ws_matmul(h, [(nsa_w_in, bl)], _ep_q, [(N_HEADS * HEAD_DIM, BF16)] * 2, tm=ROW_TILE,
                                tn=COL_TILE, n_cols=N_HEADS * HEAD_DIM, extras=rope_extras, name="nsa_q")
            wg = nsa_w_in[bl, :, N_HEADS * HEAD_DIM:].reshape(D, N_BRANCH, KV_HEADS, Q_PER_KV)
            wg = jnp.transpose(wg, (0, 2, 1, 3)).reshape(D, KV_HEADS, N_BRANCH * Q_PER_KV)
            wg = jnp.pad(wg, ((0, 0), (0, 0), (0, LANES - N_BRANCH * Q_PER_KV))).reshape(1, D, KV_HEADS * LANES)
            gates = _ws_matmul(h, [(wg, 0)], _ep_sigmoid, [(KV_HEADS * LANES, F32)], tm=ROW_TILE, tn=KV_HEADS * LANES,
                               n_cols=KV_HEADS * LANES, name="nsa_gates")[0]
            o = _attn_prompt(qc, qr, kv['kc_p'], kv['vc_p'], kv['rb'], gates, batch=B, seq=S)
            qc_s = _head_rows(qc[n_prompt:n_tok])
            qr_s = _head_rows(qr[n_prompt:n_tok])
            oc_s, idx_s, val_s = _attn_s1(qc_s, kv['kc_s'], kv['vc_s'], qpos=qpos_s)
            idx = idx_s[:, :SEL_TOP].reshape(DB, KV_HEADS, SEL_TOP)
            ok = val_s[:, :SEL_TOP].reshape(DB, KV_HEADS, SEL_TOP) >= 0.0
            n_past_blk = past_len // SEL_BLOCK
            ppb = PAGE_SIZE // SEL_BLOCK
            ip = jnp.minimum(idx, n_past_blk - 1)
            phys = jnp.take_along_axis(page_table, (ip // ppb).reshape(DB, -1), 1).reshape(ip.shape) * ppb + ip % ppb
            has_new = jnp.any((idx >= n_past_blk) & ok, -1)
            g_s = gates[n_prompt:n_tok].reshape(DB, KV_HEADS, LANES)[:, :, :N_BRANCH * Q_PER_KV]
            g_s = jnp.transpose(g_s.reshape(DB, KV_HEADS, N_BRANCH, Q_PER_KV), (0, 1, 3, 2))
            g_s = jnp.pad(g_s, ((0, 0), (0, 0), (0, 8 - Q_PER_KV), (0, HEAD_DIM - N_BRANCH)))
            o_s = _attn_s2(qr_s, kv['pool_k'], kv['pool_v'], kv['ks_new'], kv['vs_new'], kv['win_k'], kv['win_v'],
                           kv['kw_new'], kv['vw_new'], oc_s, g_s, phys.reshape(-1).astype(I32),
                           idx.reshape(-1).astype(I32), ok.reshape(-1).astype(I32), has_new.reshape(-1).astype(I32),
                           qpos=qpos_s, past_len=past_len)
            o_s = o_s[:, :, :Q_PER_KV].reshape(n_sample, N_HEADS * HEAD_DIM).astype(BF16)
            o = jnp.concatenate([o, o_s, jnp.zeros((M - n_tok, N_HEADS * HEAD_DIM), BF16)], 0)
            x = residual_mm(o, nsa_w_out, bl, x, ROW_TILE, "nsa_out")

        if l % 2 == 0:
            h = _rmsnorm(x, norm_ffn[l], BF16, tm=ROW_TILE)
            hh = _ws_matmul(h, [(ff_w_gate, l // 2), (ff_w_up, l // 2)], _ep_swiglu, [(ff_w_gate.shape[2], BF16)],
                            tm=ROW_TILE, tn=COL_TILE, n_cols=ff_w_gate.shape[2], name="ffn_up")[0]
            x = residual_mm(hh, ff_w_down, l // 2, x, ROW_TILE_DOWN, "ffn_down")
        else:
            x = _moe(x, norm_ffn[l], moe_router[l // 2], moe_w_gate, moe_w_up, moe_w_down, l // 2, n_tok=n_tok)

        if l == N_A_LAYERS - 1:
            hkv = _rmsnorm(x, norm_kv, BF16, tm=ROW_TILE)
            r, rb = _ws_matmul(hkv, [(kv_w.reshape(1, D, 6 * KV_SEC), 0)], _ep_kv,
                               [(6 * KV_SEC, F32), (6 * KV_SEC, BF16)], tm=ROW_TILE, tn=KV_SEC,
                               n_cols=6 * KV_SEC, extras=rope_extras, name="kv_proj")
            pages_p = r.reshape(M // PAGE_SIZE, PAGE_SIZE, 6 * KV_SEC)
            table_p = jnp.arange(n_prompt // PAGE_SIZE, dtype=I32).reshape(B, S // PAGE_SIZE)
            r_s = r[n_prompt:n_tok]
            sec_s = lambda i: r_s[:, i * KV_SEC:(i + 1) * KV_SEC]
            rows2d = lambda c: c.reshape(-1, HEAD_DIM)
            kv = dict(
                rb=rb,
                kc_p=_compress(pages_p, table_p, 0, cmp_pe_k, cmp_w1_k, cmp_w2_k),
                vc_p=_compress(pages_p, table_p, 1, cmp_pe_v, cmp_w1_v, cmp_w2_v),
                kc_s=_compress(rows2d(cache_cmp_k), page_table, 0, cmp_pe_k, cmp_w1_k, cmp_w2_k),
                vc_s=_compress(rows2d(cache_cmp_v), page_table, 0, cmp_pe_v, cmp_w1_v, cmp_w2_v),
                pool_k=rows2d(cache_sel_k), pool_v=rows2d(cache_sel_v),
                ks_new=_head_rows(sec_s(2)), vs_new=_head_rows(sec_s(3)),
                kw_new=_head_rows(sec_s(4)), vw_new=_head_rows(sec_s(5)),
                win_k=rows2d(state_win_k), win_v=rows2d(state_win_v),
            )
            r_p = r[:n_prompt]
            pg = lambda i: r_p[:, i * KV_SEC:(i + 1) * KV_SEC].reshape(B, S // PAGE_SIZE, PAGE_SIZE, KV_HEADS, HEAD_DIM)
            wkp = min(WINDOW, S)
            tail = lambda i: r_p[:, i * KV_SEC:(i + 1) * KV_SEC].reshape(B, S, KV_HEADS, HEAD_DIM)[:, S - wkp:]
            row_s = lambda i: sec_s(i).reshape(DB, DS, KV_HEADS, HEAD_DIM)
            state_p = (pg(0), pg(1), pg(2), pg(3), tail(4), tail(5))
            state_s = (row_s(0), row_s(1), row_s(2), row_s(3),
                       jnp.concatenate([state_win_k, row_s(4)], 1)[:, DS:],
                       jnp.concatenate([state_win_v, row_s(5)], 1)[:, DS:])

    y_prompt = _rmsnorm(x, norm_final, F32, tm=S // 2, n_rows=n_prompt).reshape(B, S, D)
    y_sample = _rmsnorm(x, norm_final, F32, tm=8, first_row=n_prompt, n_rows=8)[:n_sample].reshape(DB, DS, D)
    return (y_prompt, y_sample) + state_p + state_s + (jnp.stack(gm_v, 0),)
```

```python
import functools

import numpy as np
import jax
import jax.numpy as jnp
from jax import lax
from jax.experimental import pallas as pl
from jax.experimental.pallas import tpu as pltpu

F32 = jnp.float32
BF16 = jnp.bfloat16
I32 = jnp.int32

D_MODEL = 2048
DEPTH = 4
PAGE_SIZE = 128
N_A_LAYERS = DEPTH // 2
GM_CHUNK = 128
GM_DIM = D_MODEL
GM_GROUPS = 16
GM_GW = GM_DIM // GM_GROUPS
N_HEADS = 16
HEAD_DIM = 128
KV_HEADS = 4
Q_PER_KV = N_HEADS // KV_HEADS
ROT_DIM = HEAD_DIM // 4
ROPE_THETA = 500000.0
CMP_LEN = 32
CMP_STRIDE = 16
CMP_R = CMP_LEN // CMP_STRIDE
SEL_BLOCK = 64
SEL_TOP = 16
WINDOW = 512
N_BRANCH = 3
KV_SEC = KV_HEADS * HEAD_DIM
ATTN_SCALE = HEAD_DIM ** -0.5
N_EXPERTS = 8
TOP_K = 2
EPS = 1e-6
NEG = -1e30
FORCE_SCORE = 1e6

LANES = 128
VMEM_LIMIT_BYTES = 56 * 1024 * 1024
ROW_TILE = 832
ROW_TILE_DOWN = 416
COL_TILE = 512
COL_TILE_WIDE = 1024
MOE_BLOCK = 256
MOE_SEG_BLOCKS = 10
MOE_FF_TILE = 256
KEY_TILE = 256
Q_TILE = 256
PAGES_PER_STEP = 8


def _cparams(*sem):
    return pltpu.CompilerParams(dimension_semantics=sem, vmem_limit_bytes=VMEM_LIMIT_BYTES)


def _dot(a, b):
    return jnp.dot(a, b, preferred_element_type=F32)


def _dot_nt(a, b):
    return lax.dot_general(a, b, (((1,), (1,)), ((), ())), preferred_element_type=F32)


def _dot3(a, b):
    ah = a.astype(BF16)
    al = (a - ah.astype(F32)).astype(BF16)
    bh = b.astype(BF16)
    bl = (b - bh.astype(F32)).astype(BF16)
    return _dot(ah, bh) + _dot(al, bh) + _dot(ah, bl)


def _ws_body(*refs, nw, n_extra, n_out, epilogue):
    x_ref = refs[0]
    w_refs = refs[1:1 + nw]
    e_refs = refs[1 + nw:1 + nw + n_extra]
    o_refs = refs[1 + nw + n_extra:1 + nw + n_extra + n_out]
    wb_refs = refs[1 + nw + n_extra + n_out:]
    j = pl.program_id(0)

    @pl.when(pl.program_id(1) == 0)
    def _():
        for w_ref, wb in zip(w_refs, wb_refs):
            wb[...] = w_ref[...].astype(BF16)

    accs = [_dot(x_ref[...], wb[...]) for wb in wb_refs]
    outs = epilogue(j, accs, e_refs)
    for o_ref, o in zip(o_refs, outs):
        o_ref[...] = o.astype(o_ref.dtype)


def _ws_matmul(x, ws, epilogue, outs, *, tm, tn, n_cols, extras=(), name):
    M, K = x.shape
    nj = n_cols // tn
    ni = M // tm
    assert nj * tn == n_cols and ni * tm == M
    in_specs = [pl.BlockSpec((tm, K), lambda j, i: (i, 0))]
    in_specs += [pl.BlockSpec((None, K, tn), lambda j, i, layer=layer: (layer, 0, j)) for _, layer in ws]
    ws = [w for w, _ in ws]
    in_specs += [pl.BlockSpec(bs, im) for _, bs, im in extras]
    out_shape = [jax.ShapeDtypeStruct((M, c), dt) for c, dt in outs]
    out_specs = [pl.BlockSpec((tm, c // nj), lambda j, i: (i, j)) for c, _ in outs]
    body = functools.partial(_ws_body, nw=len(ws), n_extra=len(extras), n_out=len(outs), epilogue=epilogue)
    res = pl.pallas_call(
        body, grid=(nj, ni), in_specs=in_specs, out_specs=out_specs, out_shape=out_shape,
        scratch_shapes=[pltpu.VMEM((K, tn), BF16) for _ in ws],
        compiler_params=_cparams("arbitrary", "arbitrary"), name=name,
    )(x, *ws, *[a for a, _, _ in extras])
    return res


def _rope_tile(acc, c_ref, s1_ref, s2_ref):
    c, s1, s2 = c_ref[...], s1_ref[...], s2_ref[...]
    parts = []
    for h in range(acc.shape[1] // HEAD_DIM):
        xh = acc[:, h * HEAD_DIM:(h + 1) * HEAD_DIM]
        parts.append(xh * c + pltpu.roll(xh, HEAD_DIM - ROT_DIM // 2, 1) * s1 + pltpu.roll(xh, ROT_DIM // 2, 1) * s2)
    return jnp.concatenate(parts, 1)


def _ep_gelu(j, accs, e):
    return [jax.nn.gelu(accs[0])]


def _ep_residual(j, accs, e):
    return [e[0][...] + accs[0]]


def _ep_swiglu(j, accs, e):
    return [jax.nn.silu(accs[0]) * accs[1]]


def _ep_sigmoid(j, accs, e):
    return [jax.nn.sigmoid(accs[0])]


def _ep_kv(j, accs, e):
    acc = accs[0]
    per_tile = acc.shape[1] // KV_SEC
    parts = []
    for t in range(per_tile):
        a = acc[:, t * KV_SEC:(t + 1) * KV_SEC]
        sec = j * per_tile + t
        parts.append(jnp.where((sec == 2) | (sec == 4), _rope_tile(a, *e), a))
    r = parts[0] if per_tile == 1 else jnp.concatenate(parts, 1)
    return [r, r]


def _ep_q(j, accs, e):
    acc = accs[0]
    return [acc, _rope_tile(acc, *e)]


def _rms(x, g):
    return (x * lax.rsqrt(jnp.mean(x * x, -1, keepdims=True) + EPS)) * g


def _rms_body(x_ref, g_ref, o_ref):
    o_ref[...] = _rms(x_ref[...], g_ref[...]).astype(o_ref.dtype)


def _rmsnorm(x, g, out_dtype, *, tm, first_row=0, n_rows=None):
    M, D = x.shape
    n_rows = M if n_rows is None else n_rows
    assert n_rows % tm == 0 and first_row % tm == 0
    off = first_row // tm
    return pl.pallas_call(
        _rms_body, grid=(n_rows // tm,),
        in_specs=[pl.BlockSpec((tm, D), lambda i: (i + off, 0)), pl.BlockSpec((1, D), lambda i: (0, 0))],
        out_specs=pl.BlockSpec((tm, D), lambda i: (i, 0)),
        out_shape=jax.ShapeDtypeStruct((n_rows, D), out_dtype),
        compiler_params=_cparams("arbitrary"), name="rmsnorm",
    )(x, g.reshape(1, D))


def _rms_router_body(x_ref, g_ref, r_ref, h_ref, te_ref, tg_ref):
    h = _rms(x_ref[...], g_ref[...])
    h_ref[...] = h
    lane = lax.broadcasted_iota(I32, (1, LANES), 1).astype(F32)
    logits = jnp.where(lane < N_EXPERTS, _dot3(h, r_ref[...]), -jnp.inf)
    m1 = jnp.max(logits, -1, keepdims=True)
    i1 = jnp.min(jnp.where(logits == m1, lane, float(LANES)), -1, keepdims=True)
    rest = jnp.where(lane == i1, -jnp.inf, logits)
    m2 = jnp.max(rest, -1, keepdims=True)
    i2 = jnp.min(jnp.where(rest == m2, lane, float(LANES)), -1, keepdims=True)
    e2 = jnp.exp(m2 - m1)
    den = 1.0 + e2
    te_ref[...] = jnp.where(lane == 0, i1, jnp.where(lane == 1, i2, 0.0)).astype(I32)
    tg_ref[...] = jnp.where(lane == 0, 1.0 / den, jnp.where(lane == 1, e2 / den, 0.0))


def _rmsnorm_router(x, g, router, *, tm):
    M, D = x.shape
    rp = jnp.pad(router, ((0, 0), (0, LANES - N_EXPERTS)))
    return pl.pallas_call(
        _rms_router_body, grid=(M // tm,),
        in_specs=[pl.BlockSpec((tm, D), lambda i: (i, 0)), pl.BlockSpec((1, D), lambda i: (0, 0)),
                  pl.BlockSpec((D, LANES), lambda i: (0, 0))],
        out_specs=[pl.BlockSpec((tm, D), lambda i: (i, 0)), pl.BlockSpec((tm, LANES), lambda i: (i, 0)),
                   pl.BlockSpec((tm, LANES), lambda i: (i, 0))],
        out_shape=[jax.ShapeDtypeStruct((M, D), F32), jax.ShapeDtypeStruct((M, LANES), I32),
                   jax.ShapeDtypeStruct((M, LANES), F32)],
        compiler_params=_cparams("arbitrary"), name="rmsnorm_router",
    )(x, g.reshape(1, D), rp)


def _gm_gate_body(z_ref, lng_ref, lnb_ref, ws_ref, bst_ref, w00_ref, b00_ref, o_ref, vs_ref, *,
                  n_prompt_chunks, n_sample):
    c = pl.program_id(0)
    z = z_ref[...]
    u = z[:, :GM_DIM].astype(F32)
    v = z[:, GM_DIM:].astype(F32)
    mu = jnp.mean(v, -1, keepdims=True)
    vc = v - mu
    vn = (vc * lax.rsqrt(jnp.mean(vc * vc, -1, keepdims=True) + EPS)) * lng_ref[...] + lnb_ref[...]

    @pl.when(c < n_prompt_chunks)
    def _():
        vb = vn.astype(BF16)
        tri = lax.broadcasted_iota(I32, (GM_CHUNK, GM_CHUNK), 0) >= lax.broadcasted_iota(I32, (GM_CHUNK, GM_CHUNK), 1)
        for g in range(GM_GROUPS):
            sl = slice(g * GM_GW, (g + 1) * GM_GW)
            wg = jnp.where(tri, ws_ref[g], 0.0).astype(BF16)
            s = _dot(wg, vb[:, sl]) + bst_ref[:, g:g + 1]
            o_ref[:, sl] = (u[:, sl] * s).astype(o_ref.dtype)

    @pl.when(c >= n_prompt_chunks)
    def _():
        row = lax.broadcasted_iota(I32, (GM_CHUNK, 1), 0)
        s = w00_ref[...] * vn + b00_ref[...]
        o_ref[...] = jnp.where(row < n_sample, u * s, 0.0).astype(o_ref.dtype)
        vs_ref[...] = vn[:vs_ref.shape[0]]


def _gm_gate(z, ln_g, ln_b, w_s, b_s, *, n_prompt, n_sample):
    M = z.shape[0]
    n_pc = n_prompt // GM_CHUNK
    assert n_prompt % GM_CHUNK == 0 and M == (n_pc + 1) * GM_CHUNK and n_sample <= GM_CHUNK
    n_vs = -(-n_sample // 8) * 8
    w00 = jnp.repeat(w_s[:, 0, 0], GM_GW).reshape(1, GM_DIM)
    b00 = jnp.repeat(b_s[:, 0], GM_GW).reshape(1, GM_DIM)
    body = functools.partial(_gm_gate_body, n_prompt_chunks=n_pc, n_sample=n_sample)
    full = lambda *shape: pl.BlockSpec(shape, lambda c: (0,) * len(shape))
    return pl.pallas_call(
        body, grid=(n_pc + 1,),
        in_specs=[pl.BlockSpec((GM_CHUNK, 2 * GM_DIM), lambda c: (c, 0)), full(1, GM_DIM), full(1, GM_DIM),
                  full(GM_GROUPS, GM_CHUNK, GM_CHUNK), full(GM_CHUNK, GM_GROUPS), full(1, GM_DIM), full(1, GM_DIM)],
        out_specs=[pl.BlockSpec((GM_CHUNK, GM_DIM), lambda c: (c, 0)), full(n_vs, GM_DIM)],
        out_shape=[jax.ShapeDtypeStruct((M, GM_DIM), BF16), jax.ShapeDtypeStruct((n_vs, GM_DIM), F32)],
        compiler_params=_cparams("arbitrary"), name="gmlp_gate",
    )(z, ln_g.reshape(1, GM_DIM), ln_b.reshape(1, GM_DIM), w_s, b_s.T, w00, b00)


def _moe_gather_body(rt_ref, nu_ref, h_hbm, o_ref, buf, sem):
    i = pl.program_id(0)
    n_used = nu_ref[0]

    def row_copy(tok, slot, r):
        return pltpu.make_async_copy(h_hbm.at[pl.ds(tok, 1), :], buf.at[slot, pl.ds(r, 1), :], sem.at[slot])

    def start_block(blk, slot):
        def issue(r, c):
            row_copy(rt_ref[blk * MOE_BLOCK + r], slot, r).start()
            return c

        lax.fori_loop(0, MOE_BLOCK, issue, 0, unroll=8)

    @pl.when((i == 0) & (n_used > 0))
    def _():
        start_block(0, 0)

    @pl.when(i + 1 < n_used)
    def _():
        start_block(i + 1, (i + 1) % 2)

    @pl.when(i < n_used)
    def _():
        slot = i % 2

        def wait(r, c):
            row_copy(0, slot, r).wait()
            return c

        lax.fori_loop(0, MOE_BLOCK, wait, 0, unroll=8)
        o_ref[...] = buf[slot].astype(o_ref.dtype)

    @pl.when(i >= n_used)
    def _():
        o_ref[...] = jnp.zeros(o_ref.shape, o_ref.dtype)


def _moe_gather(h, row_tok, n_used, n_blk):
    D = h.shape[1]
    return pl.pallas_call(
        _moe_gather_body,
        grid_spec=pltpu.PrefetchScalarGridSpec(
            num_scalar_prefetch=2, grid=(n_blk,),
            in_specs=[pl.BlockSpec(memory_space=pl.ANY)],
            out_specs=pl.BlockSpec((MOE_BLOCK, D), lambda i, rt, nu: (i, 0)),
            scratch_shapes=[pltpu.VMEM((2, MOE_BLOCK, D), F32), pltpu.SemaphoreType.DMA((2,))]),
        out_shape=jax.ShapeDtypeStruct((n_blk * MOE_BLOCK, D), BF16),
        compiler_params=_cparams("arbitrary"), name="moe_gather",
    )(row_tok, n_used, h)


def _moe_ffn_body(se_ref, sb_ref, sn_ref, ns_ref, nu_ref, xb_hbm, wg_ref, wu_ref, wd_ref, yb_hbm,
                  x_vmem, y_acc, wgb, wub, wdb, sem_in, sem_out, *, n_j, n_seg_max, n_blk):
    s, j = pl.program_id(0), pl.program_id(1)
    nblk = sn_ref[jnp.minimum(s, n_seg_max - 1)]
    row0 = sb_ref[jnp.minimum(s, n_seg_max - 1)] * MOE_BLOCK

    def block_copy(src, src_row, dst, dst_row, sem):
        return pltpu.make_async_copy(src.at[pl.ds(pl.multiple_of(src_row, MOE_BLOCK), MOE_BLOCK), :],
                                     dst.at[pl.ds(pl.multiple_of(dst_row, MOE_BLOCK), MOE_BLOCK), :], sem)

    def for_blocks(lo, hi, fn):
        def body(blk, c):
            fn(blk)
            return c

        lax.fori_loop(lo, hi, body, 0)

    @pl.when(s < ns_ref[0])
    def _():
        @pl.when(j == 0)
        def _():
            for_blocks(0, nblk, lambda blk: block_copy(xb_hbm, row0 + blk * MOE_BLOCK, x_vmem, blk * MOE_BLOCK,
                                                       sem_in).start())
            for_blocks(0, nblk, lambda blk: block_copy(xb_hbm, row0, x_vmem, blk * MOE_BLOCK, sem_in).wait())

        wgb[...] = wg_ref[...].astype(BF16)
        wub[...] = wu_ref[...].astype(BF16)
        wdb[...] = wd_ref[...].astype(BF16)

        def update(blk, n, first):
            rows = pl.ds(pl.multiple_of(blk * MOE_BLOCK, MOE_BLOCK), n * MOE_BLOCK)
            x = x_vmem[rows, :]
            hidden = (jax.nn.silu(_dot(x, wgb[...])) * _dot(x, wub[...])).astype(BF16)
            part = _dot(hidden, wdb[...])
            y_acc[rows, :] = part if first else y_acc[rows, :] + part

        def sweep(first):
            pairs = nblk // 2
            for_blocks(0, pairs, lambda c: update(2 * c, 2, first))

            @pl.when(nblk % 2 == 1)
            def _():
                update(2 * pairs, 1, first)

        @pl.when(j == 0)
        def _():
            sweep(True)

        @pl.when(j > 0)
        def _():
            sweep(False)

        @pl.when(j == n_j - 1)
        def _():
            for_blocks(0, nblk, lambda blk: block_copy(y_acc, blk * MOE_BLOCK, yb_hbm, row0 + blk * MOE_BLOCK,
                                                       sem_out).start())
            for_blocks(0, nblk, lambda blk: block_copy(y_acc, blk * MOE_BLOCK, yb_hbm, row0, sem_out).wait())

    @pl.when((s == n_seg_max) & (j == 0))
    def _():
        y_acc[pl.ds(0, MOE_BLOCK), :] = jnp.zeros((MOE_BLOCK, y_acc.shape[1]), F32)
        for_blocks(nu_ref[0], n_blk, lambda blk: block_copy(y_acc, 0, yb_hbm, blk * MOE_BLOCK, sem_out).start())
        for_blocks(nu_ref[0], n_blk, lambda blk: block_copy(y_acc, 0, yb_hbm, 0, sem_out).wait())


def _moe_ffn(xb, wg, wu, wd, layer, seg_e, seg_blk, seg_nblk, n_seg, n_used, *, n_seg_max):
    M, D = xb.shape
    F = wg.shape[3]
    n_j = F // MOE_FF_TILE
    assert n_j * MOE_FF_TILE == F
    n_blk = M // MOE_BLOCK

    def w_index(s, j, se, sb, sn, ns, nu):
        live = s < ns[0]
        return se[jnp.minimum(s, ns[0] - 1)], jnp.where(live, j, n_j - 1)

    def up_map(s, j, *pref):
        e, jj = w_index(s, j, *pref)
        return layer, e, 0, jj

    def down_map(s, j, *pref):
        e, jj = w_index(s, j, *pref)
        return layer, e, jj, 0

    seg_rows = MOE_SEG_BLOCKS * MOE_BLOCK
    body = functools.partial(_moe_ffn_body, n_j=n_j, n_seg_max=n_seg_max, n_blk=n_blk)
    return pl.pallas_call(
        body,
        grid_spec=pltpu.PrefetchScalarGridSpec(
            num_scalar_prefetch=5, grid=(n_seg_max + 1, n_j),
            in_specs=[pl.BlockSpec(memory_space=pl.ANY),
                      pl.BlockSpec((None, None, D, MOE_FF_TILE), up_map),
                      pl.BlockSpec((None, None, D, MOE_FF_TILE), up_map),
                      pl.BlockSpec((None, None, MOE_FF_TILE, D), down_map)],
            out_specs=pl.BlockSpec(memory_space=pl.ANY),
            scratch_shapes=[pltpu.VMEM((seg_rows, D), BF16), pltpu.VMEM((seg_rows, D), F32),
                            pltpu.VMEM((D, MOE_FF_TILE), BF16), pltpu.VMEM((D, MOE_FF_TILE), BF16),
                            pltpu.VMEM((MOE_FF_TILE, D), BF16),
                            pltpu.SemaphoreType.DMA(()), pltpu.SemaphoreType.DMA(())]),
        out_shape=jax.ShapeDtypeStruct((M, D), F32),
        compiler_params=_cparams("arbitrary", "arbitrary"), name="moe_ffn",
    )(seg_e, seg_blk, seg_nblk, n_seg, n_used, xb, wg, wu, wd)


def _moe_combine_body(d0_ref, d1_ref, x_ref, g_ref, yb_hbm, o_ref, b0, b1, sem, *, tm):
    i = pl.program_id(0)

    def row_copy(src_row, buf, slot, r):
        return pltpu.make_async_copy(yb_hbm.at[pl.ds(src_row, 1), :], buf.at[slot, pl.ds(r, 1), :], sem.at[slot])

    def start_tile(t, slot):
        def issue(r, c):
            row_copy(d0_ref[t * tm + r], b0, slot, r).start()
            row_copy(d1_ref[t * tm + r], b1, slot, r).start()
            return c

        lax.fori_loop(0, tm, issue, 0, unroll=8)

    @pl.when(i == 0)
    def _():
        start_tile(0, 0)

    @pl.when(i + 1 < pl.num_programs(0))
    def _():
        start_tile(i + 1, (i + 1) % 2)

    slot = i % 2

    def wait(r, c):
        row_copy(0, b0, slot, r).wait()
        row_copy(0, b1, slot, r).wait()
        return c

    lax.fori_loop(0, tm, wait, 0, unroll=8)
    g = g_ref[...]
    o_ref[...] = x_ref[...] + (g[:, 0:1] * b0[slot] + g[:, 1:2] * b1[slot])


def _moe_combine(x, gates, yb, d0, d1, *, tm):
    M, D = x.shape
    body = functools.partial(_moe_combine_body, tm=tm)
    return pl.pallas_call(
        body,
        grid_spec=pltpu.PrefetchScalarGridSpec(
            num_scalar_prefetch=2, grid=(M // tm,),
            in_specs=[pl.BlockSpec((tm, D), lambda i, a, b: (i, 0)), pl.BlockSpec((tm, LANES), lambda i, a, b: (i, 0)),
                      pl.BlockSpec(memory_space=pl.ANY)],
            out_specs=pl.BlockSpec((tm, D), lambda i, a, b: (i, 0)),
            scratch_shapes=[pltpu.VMEM((2, tm, D), F32), pltpu.VMEM((2, tm, D), F32), pltpu.SemaphoreType.DMA((2,))]),
        out_shape=jax.ShapeDtypeStruct((M, D), F32),
        compiler_params=_cparams("arbitrary"), name="moe_combine",
    )(d0, d1, x, gates, yb)


def _moe(x, g_norm, router, wg, wu, wd, layer, *, n_tok):
    M, D = x.shape
    h, te, tg = _rmsnorm_router(x, g_norm, router, tm=ROW_TILE)
    a = n_tok * TOP_K
    n_blk = -(-a // MOE_BLOCK) + N_EXPERTS
    e_flat = te[:n_tok, :TOP_K].reshape(a)
    onehot = (e_flat[:, None] == jnp.arange(N_EXPERTS, dtype=I32)[None, :]).astype(I32)
    csum = jnp.cumsum(onehot, 0)
    pos = jnp.take_along_axis(csum, e_flat[:, None], 1)[:, 0] - 1
    counts = csum[-1]
    padded = (counts + MOE_BLOCK - 1) // MOE_BLOCK * MOE_BLOCK
    end_pad = jnp.cumsum(padded)
    start_pad = end_pad - padded
    dest = (start_pad[e_flat] + pos).astype(I32)
    tok = jnp.arange(a, dtype=I32) // TOP_K
    row_tok = jnp.full((n_blk * MOE_BLOCK,), n_tok, I32).at[dest].set(tok)
    n_used = (end_pad[-1] // MOE_BLOCK).astype(I32).reshape(1)
    n_seg_max = n_blk // MOE_SEG_BLOCKS + N_EXPERTS + 1
    e_blocks = padded // MOE_BLOCK
    e_segs = (e_blocks + MOE_SEG_BLOCKS - 1) // MOE_SEG_BLOCKS
    seg_end = jnp.cumsum(e_segs)
    s_ids = jnp.arange(n_seg_max, dtype=I32)
    seg_e = jnp.minimum(jnp.sum((seg_end[None, :] <= s_ids[:, None]).astype(I32), 1), N_EXPERTS - 1)
    local = s_ids - (seg_end - e_segs)[seg_e]
    seg_blk = (start_pad // MOE_BLOCK)[seg_e] + local * MOE_SEG_BLOCKS
    seg_nblk = jnp.where(s_ids < seg_end[-1], jnp.clip(e_blocks[seg_e] - local * MOE_SEG_BLOCKS, 0, MOE_SEG_BLOCKS), 0)
    d = jnp.pad(dest.reshape(n_tok, TOP_K), ((0, M - n_tok), (0, 0)))
    row = jnp.arange(M, dtype=I32)[:, None]
    gates = jnp.where(row < n_tok, tg, 0.0)

    xb = _moe_gather(h, row_tok, n_used, n_blk)
    yb = _moe_ffn(xb, wg, wu, wd, layer, seg_e.astype(I32), seg_blk.astype(I32), seg_nblk.astype(I32),
                  seg_end[-1:].astype(I32), n_used, n_seg_max=n_seg_max)
    return _moe_combine(x, gates, yb, d[:, 0], d[:, 1], tm=ROW_TILE_DOWN)


def _cmp_z_body(pt_ref, *refs):
    n_in = len(refs) - 2
    page_refs = refs[:n_in]
    w_ref = refs[n_in]
    o_ref = refs[n_in + 1]
    n_ch = PAGE_SIZE // CMP_STRIDE
    cols = [jnp.concatenate([pg[pl.ds(s, n_ch, stride=CMP_STRIDE), :] for pg in page_refs], 0).astype(BF16)
            for s in range(CMP_STRIDE)]
    acc = _dot(jnp.concatenate(cols, 1), w_ref[...])
    m = PAGES_PER_STEP * n_ch
    for k in range(KV_HEADS):
        o_ref[k] = acc[k * m:(k + 1) * m]


def _cmp_z(pages, table, sec, w1cat):
    nb, npg = table.shape
    assert npg % PAGES_PER_STEP == 0
    n_ch = PAGE_SIZE // CMP_STRIDE
    m = PAGES_PER_STEP * n_ch
    specs = [pl.BlockSpec((None, PAGE_SIZE, HEAD_DIM),
                          lambda b, g, pt, k=k, q=q: (pt[b, g * PAGES_PER_STEP + q], 0, sec * KV_HEADS + k))
             for k in range(KV_HEADS) for q in range(PAGES_PER_STEP)]
    return pl.pallas_call(
        _cmp_z_body,
        grid_spec=pltpu.PrefetchScalarGridSpec(
            num_scalar_prefetch=1, grid=(nb, npg // PAGES_PER_STEP),
            in_specs=specs + [pl.BlockSpec((CMP_STRIDE * HEAD_DIM, CMP_R * HEAD_DIM), lambda b, g, pt: (0, 0))],
            out_specs=pl.BlockSpec((None, KV_HEADS, m, CMP_R * HEAD_DIM), lambda b, g, pt: (b, 0, g, 0))),
        out_shape=jax.ShapeDtypeStruct((nb, KV_HEADS, npg * n_ch, CMP_R * HEAD_DIM), F32),
        compiler_params=_cparams("arbitrary", "arbitrary"), name="cmp_z",
    )(table, *([pages] * len(specs)), w1cat)


def _cmp_cache_z_body(pt_ref, *refs):
    page_refs = refs[:PAGES_PER_STEP]
    w_ref, o_ref = refs[PAGES_PER_STEP:]
    tiles_per_chunk = CMP_STRIDE * KV_HEADS // 8
    lhs = jnp.concatenate(
        [jnp.concatenate([pg[:, 8 * t:8 * t + 8, :].reshape(-1, HEAD_DIM) for t in range(tiles_per_chunk)], 1)
         for pg in page_refs], 0).astype(BF16)
    even, odd = _dot(lhs, w_ref[0]), _dot(lhs, w_ref[1])
    o_ref[...] = even + pltpu.roll(odd, even.shape[0] - KV_HEADS, 0)


def _cmp_cache_fin_body(z_ref, pe_ref, w1_ref, w2_ref, o_ref, y_ref):
    z = z_ref[...]
    n = z.shape[0]
    pe = jnp.broadcast_to(pe_ref[...], (8, pe_ref.shape[1])).astype(BF16)
    pre = _dot(pe, w1_ref[...].astype(BF16))[0:1]
    pre = pre + z[:, :HEAD_DIM]
    pre = pre + pltpu.roll(z[:, HEAD_DIM:], n - 8, 0)
    y_ref[...] = _dot(jax.nn.gelu(pre).astype(BF16), w2_ref[...].astype(BF16))
    for k in range(KV_HEADS):
        o_ref[k] = y_ref[pl.ds(k, n // 8, stride=8), :].astype(o_ref.dtype)


def _compress_cache(cache, table, pe, w1, w2):
    assert KV_HEADS == 4 and CMP_STRIDE % 2 == 0
    nb, npg = table.shape
    assert npg % PAGES_PER_STEP == 0
    n_ch = PAGE_SIZE // CMP_STRIDE
    chunk_rows = CMP_STRIDE * KV_HEADS
    pages = cache.reshape(-1, chunk_rows, HEAD_DIM)
    w = _w1cat(w1).reshape(CMP_STRIDE // 2, 2, HEAD_DIM, CMP_R * HEAD_DIM)
    w = jnp.transpose(w, (1, 0, 2, 3)).reshape(2, CMP_STRIDE // 2 * HEAD_DIM, CMP_R * HEAD_DIM)
    m = PAGES_PER_STEP * n_ch * 8
    z = pl.pallas_call(
        _cmp_cache_z_body,
        grid_spec=pltpu.PrefetchScalarGridSpec(
            num_scalar_prefetch=1, grid=(nb, npg // PAGES_PER_STEP),
            in_specs=[pl.BlockSpec((n_ch, chunk_rows, HEAD_DIM),
                                   lambda b, g, pt, q=q: (pt[b, g * PAGES_PER_STEP + q], 0, 0))
                      for q in range(PAGES_PER_STEP)]
            + [pl.BlockSpec(w.shape, lambda b, g, pt: (0, 0, 0))],
            out_specs=pl.BlockSpec((None, m, CMP_R * HEAD_DIM), lambda b, g, pt: (b, g, 0))),
        out_shape=jax.ShapeDtypeStruct((nb, npg * n_ch * 8, CMP_R * HEAD_DIM), F32),
        compiler_params=_cparams("arbitrary", "arbitrary"), name="cmp_cache_z",
    )(table, *([pages] * PAGES_PER_STEP), w)
    n = npg * n_ch
    return pl.pallas_call(
        _cmp_cache_fin_body, grid=(nb,),
        in_specs=[pl.BlockSpec((None, n * 8, CMP_R * HEAD_DIM), lambda b: (b, 0, 0)),
                  pl.BlockSpec((1, CMP_LEN * HEAD_DIM), lambda b: (0, 0)),
                  pl.BlockSpec((CMP_LEN * HEAD_DIM, HEAD_DIM), lambda b: (0, 0)),
                  pl.BlockSpec((HEAD_DIM, HEAD_DIM), lambda b: (0, 0))],
        out_specs=pl.BlockSpec((None, KV_HEADS, n, HEAD_DIM), lambda b: (b, 0, 0, 0)),
        out_shape=jax.ShapeDtypeStruct((nb, KV_HEADS, n, HEAD_DIM), BF16),
        scratch_shapes=[pltpu.VMEM((n * 8, HEAD_DIM), F32)],
        compiler_params=_cparams("arbitrary"), name="cmp_cache_fin",
    )(z, pe.reshape(1, -1), w1, w2)


def _cmp_fin_body(z_ref, pe_ref, w1_ref, w2_ref, o_ref):
    z = z_ref[...]
    n = z.shape[0]
    pe = jnp.broadcast_to(pe_ref[...], (8, pe_ref.shape[1])).astype(BF16)
    pre = _dot(pe, w1_ref[...].astype(BF16))[0:1]
    pre = pre + z[:, :HEAD_DIM]
    pre = pre + pltpu.roll(z[:, HEAD_DIM:], n - 1, 0)
    o_ref[...] = _dot(jax.nn.gelu(pre).astype(BF16), w2_ref[...].astype(BF16)).astype(o_ref.dtype)


def _cmp_fin(z, pe, w1, w2):
    nb, _, n, _ = z.shape
    return pl.pallas_call(
        _cmp_fin_body, grid=(nb, KV_HEADS),
        in_specs=[pl.BlockSpec((None, None, n, CMP_R * HEAD_DIM), lambda b, k: (b, k, 0, 0)),
                  pl.BlockSpec((1, CMP_LEN * HEAD_DIM), lambda b, k: (0, 0)),
                  pl.BlockSpec((CMP_LEN * HEAD_DIM, HEAD_DIM), lambda b, k: (0, 0)),
                  pl.BlockSpec((HEAD_DIM, HEAD_DIM), lambda b, k: (0, 0))],
        out_specs=pl.BlockSpec((None, None, n, HEAD_DIM), lambda b, k: (b, k, 0, 0)),
        out_shape=jax.ShapeDtypeStruct((nb, KV_HEADS, n, HEAD_DIM), BF16),
        compiler_params=_cparams("arbitrary", "arbitrary"), name="cmp_fin",
    )(z, pe.reshape(1, -1), w1, w2)


def _w1cat(w1):
    w = w1.reshape(CMP_R, CMP_STRIDE, HEAD_DIM, HEAD_DIM)
    return jnp.transpose(w, (1, 2, 0, 3)).reshape(CMP_STRIDE * HEAD_DIM, CMP_R * HEAD_DIM).astype(BF16)


def _compress(pages, table, sec, pe, w1, w2):
    return _cmp_fin(_cmp_z(pages, table, sec, _w1cat(w1)), pe, w1, w2)


def _overlap_matrix(nc, ns, rows, cols):
    cs = np.arange(nc) * CMP_STRIDE
    ss = np.arange(ns) * SEL_BLOCK
    ov = np.minimum(cs[:, None] + CMP_LEN, ss[None, :] + SEL_BLOCK) - np.maximum(cs[:, None], ss[None, :])
    ov = np.clip(ov, 0, None).astype(np.float32) / np.float32(CMP_LEN)
    out = np.zeros((rows, cols), np.float32)
    out[:nc, :ns] = ov
    return jnp.asarray(out)


def _attn_prompt_body(qc_ref, qr_ref, kc_ref, vc_ref, ks_ref, vs_ref, kw_ref, vw_ref, g_ref, ovt_ref, ext_ref, o_ref,
                      allow_ref, vct_ref, vst_ref, vwt_ref, q4_ref, s_ref, ocmp_ref, osel_ref, acc_ref, *,
                      seq, n_cmp, n_sel):
    qi = pl.program_id(2)
    qpos = qi * Q_TILE + lax.broadcasted_iota(I32, (1, Q_TILE), 1)
    row = lax.broadcasted_iota(I32, (LANES, 1), 0)

    @pl.when(qi == 0)
    def _():
        vct_ref[...] = vc_ref[...].astype(F32).T.astype(BF16)
        vst_ref[...] = vs_ref[...].astype(F32).T.astype(BF16)
        vwt_ref[...] = vw_ref[...].astype(F32).T.astype(BF16)

    cmask = (row * CMP_STRIDE + (CMP_LEN - 1) <= qpos) & (row < n_cmp)
    kc = kc_ref[...]
    psum = jnp.zeros((LANES, Q_TILE), F32)
    for p in range(Q_PER_KV):
        s = _dot_nt(kc, qc_ref[:, p * HEAD_DIM:(p + 1) * HEAD_DIM]) * ATTN_SCALE
        sm = jnp.where(cmask, s, NEG)
        e = jnp.exp(sm - jnp.max(sm, 0, keepdims=True))
        pr = jnp.where(cmask, e / jnp.sum(e, 0, keepdims=True), 0.0)
        psum = psum + pr
        ocmp_ref[p] = _dot(vct_ref[...], pr.astype(BF16))

    n_sel_pad = -(-n_sel // 8) * 8
    imp = _dot3(ovt_ref[...], psum)[:n_sel_pad]
    blk = row[:n_sel_pad]
    qblk = qpos >> 6
    valid = (blk * SEL_BLOCK <= qpos) & (blk < n_sel)
    forced = (blk == 0) | (blk == qblk) | (blk == qblk - 1)
    score = jnp.where(forced & valid, FORCE_SCORE, jnp.where(valid, imp, -1.0))
    rank = jnp.zeros(score.shape, F32)
    for j in range(n_sel):
        cj = score[j:j + 1, :]
        beats = (cj > score) | ((cj == score) & (blk > j))
        rank = rank + jnp.where(beats, 1.0, 0.0)
    sel = jnp.where((rank < SEL_TOP) & (score >= 0.0), 1.0, 0.0)
    sel = jnp.concatenate([sel, jnp.zeros((LANES - n_sel_pad, Q_TILE), F32)], 0).astype(BF16)
    hi = (qi * Q_TILE + Q_TILE - 1) // KEY_TILE + 1
    selx = _dot(ext_ref[...], sel)
    kpos_all = lax.broadcasted_iota(I32, (seq, 1), 0)
    allow_ref[...] = jnp.where((selx > 0.5) & (kpos_all <= qpos), 1.0, 0.0)

    q4_ref[...] = jnp.concatenate([qr_ref[:, p * HEAD_DIM:(p + 1) * HEAD_DIM] for p in range(Q_PER_KV)], 0)

    def flash(k_ref, vt_ref, lo, hi, mask_fn):
        acc_ref[...] = jnp.zeros(acc_ref.shape, F32)
        last = seq // KEY_TILE - 1

        def scores(kt):
            start = pl.multiple_of(kt * KEY_TILE, KEY_TILE)
            return _dot_nt(k_ref[pl.ds(start, KEY_TILE), :], q4_ref[...])

        s_ref[lo % 2] = scores(lo)

        def body(kt, carry):
            m, l = carry
            start = pl.multiple_of(kt * KEY_TILE, KEY_TILE)
            s = s_ref[kt % 2] * ATTN_SCALE
            s_ref[(kt + 1) % 2] = scores(jnp.minimum(kt + 1, last))
            mask = mask_fn(start)
            s = jnp.where(jnp.concatenate([mask] * Q_PER_KV, 1), s, NEG)
            m_new = jnp.maximum(m, jnp.max(s, 0, keepdims=True))
            alpha = jnp.exp(m - m_new)
            e = jnp.exp(s - m_new)
            l = alpha * l + jnp.sum(e, 0, keepdims=True)
            acc_ref[...] = alpha * acc_ref[...] + _dot(vt_ref[:, pl.ds(start, KEY_TILE)], e.astype(BF16))
            return m_new, l

        init = (jnp.full((1, Q_PER_KV * Q_TILE), NEG, F32), jnp.zeros((1, Q_PER_KV * Q_TILE), F32))
        return lax.fori_loop(lo, hi, body, init)[1]

    l = flash(ks_ref, vst_ref, 0, hi, lambda start: allow_ref[pl.ds(start, KEY_TILE), :] > 0.5)
    osel_ref[...] = acc_ref[...] / l

    def win_mask(start):
        kp = start + lax.broadcasted_iota(I32, (KEY_TILE, 1), 0)
        return (kp <= qpos) & (kp > qpos - WINDOW)

    l = flash(kw_ref, vwt_ref, jnp.maximum(qi * Q_TILE - (WINDOW - 1), 0) // KEY_TILE, hi, win_mask)
    owin = acc_ref[...] / l

    gt = g_ref[...].T
    for p in range(Q_PER_KV):
        sl = slice(p * Q_TILE, (p + 1) * Q_TILE)
        ot = (gt[p:p + 1] * ocmp_ref[p] + gt[Q_PER_KV + p:Q_PER_KV + p + 1] * osel_ref[:, sl]
              + gt[2 * Q_PER_KV + p:2 * Q_PER_KV + p + 1] * owin[:, sl])
        o_ref[:, p * HEAD_DIM:(p + 1) * HEAD_DIM] = ot.T.astype(o_ref.dtype)


def _attn_prompt(qc, qr, kc, vc, rb, gates, *, batch, seq):
    nq = seq // Q_TILE
    n_chunk = seq // CMP_STRIDE
    n_cmp = n_chunk - CMP_R + 1
    n_sel = seq // SEL_BLOCK
    assert n_chunk == LANES and n_sel <= LANES and seq % KEY_TILE == 0
    ovt = _overlap_matrix(n_cmp, n_sel, LANES, LANES).T
    ext = np.zeros((seq, LANES), np.float32)
    ext[np.arange(seq), np.arange(seq) // SEL_BLOCK] = 1.0
    ext = jnp.asarray(ext, BF16)
    qspec = pl.BlockSpec((Q_TILE, KV_SEC), lambda b, g, q: (b * nq + q, g))
    cspec = pl.BlockSpec((None, None, n_chunk, HEAD_DIM), lambda b, g, q: (b, g, 0, 0))
    kvspec = lambda sec: pl.BlockSpec((seq, HEAD_DIM), lambda b, g, q: (b, sec * KV_HEADS + g))
    body = functools.partial(_attn_prompt_body, seq=seq, n_cmp=n_cmp, n_sel=n_sel)
    return pl.pallas_call(
        body, grid=(batch, KV_HEADS, nq),
        in_specs=[qspec, qspec, cspec, cspec, kvspec(2), kvspec(3), kvspec(4), kvspec(5),
                  pl.BlockSpec((Q_TILE, LANES), lambda b, g, q: (b * nq + q, g)),
                  pl.BlockSpec((LANES, LANES), lambda b, g, q: (0, 0)),
                  pl.BlockSpec((seq, LANES), lambda b, g, q: (0, 0))],
        out_specs=qspec,
        out_shape=jax.ShapeDtypeStruct((batch * seq, N_HEADS * HEAD_DIM), BF16),
        scratch_shapes=[pltpu.VMEM((seq, Q_TILE), F32), pltpu.VMEM((HEAD_DIM, n_chunk), BF16),
                        pltpu.VMEM((HEAD_DIM, seq), BF16), pltpu.VMEM((HEAD_DIM, seq), BF16),
                        pltpu.VMEM((Q_PER_KV * Q_TILE, HEAD_DIM), BF16),
                        pltpu.VMEM((2, KEY_TILE, Q_PER_KV * Q_TILE), F32),
                        pltpu.VMEM((Q_PER_KV, HEAD_DIM, Q_TILE), F32), pltpu.VMEM((HEAD_DIM, Q_PER_KV * Q_TILE), F32),
                        pltpu.VMEM((HEAD_DIM, Q_PER_KV * Q_TILE), F32)],
        compiler_params=_cparams("arbitrary", "arbitrary", "arbitrary"), name="attn_prompt",
    )(qc, qr, kc, vc, rb, rb, rb, rb, gates, ovt, ext)


def _attn_s1_body(q_ref, kc_ref, vc_ref, ov_ref, o_ref, idx_ref, val_ref, sc_ref, *, qpos, n_cmp, n_sel):
    step = pl.program_id(0) * pl.num_programs(1) + pl.program_id(1)
    n_steps = pl.num_programs(0) * pl.num_programs(1)
    rows = q_ref.shape[0]
    n = kc_ref.shape[0]
    col = lax.broadcasted_iota(I32, (1, n), 1)
    cmask = (col * CMP_STRIDE + (CMP_LEN - 1) <= qpos) & (col < n_cmp)
    s = _dot_nt(q_ref[...], kc_ref[...]) * ATTN_SCALE
    sm = jnp.where(cmask, s, NEG)
    e = jnp.exp(sm - jnp.max(sm, -1, keepdims=True))
    pr = jnp.where(cmask, e / jnp.sum(e, -1, keepdims=True), 0.0)
    o_ref[...] = _dot(pr.astype(BF16), vc_ref[...])
    head = lax.broadcasted_iota(I32, (rows, 1), 0)
    psum = jnp.sum(jnp.where(head < Q_PER_KV, pr, 0.0), 0, keepdims=True)
    imp = _dot3(jnp.broadcast_to(psum, (rows, n)), ov_ref[...])[0:1]
    w = ov_ref.shape[1]
    blk = lax.broadcasted_iota(I32, (1, w), 1)
    qblk = qpos // SEL_BLOCK
    valid = (blk * SEL_BLOCK <= qpos) & (blk < n_sel)
    forced = (blk == 0) | (blk == qblk) | (blk == qblk - 1)
    low = -3e38
    sc_ref[pl.ds(step, 1), :] = jnp.where(
        blk < n_sel, jnp.where(forced & valid, FORCE_SCORE, jnp.where(valid, imp, -1.0)), low)

    @pl.when(step == n_steps - 1)
    def _():
        score = sc_ref[...]
        blkf = blk.astype(F32)
        lane = lax.broadcasted_iota(I32, (1, LANES), 1)
        idx_rows = jnp.zeros(idx_ref.shape, F32)
        val_rows = jnp.full(val_ref.shape, -1.0, F32)
        for t in range(SEL_TOP):
            m = jnp.max(score, -1, keepdims=True)
            it = jnp.min(jnp.where(score == m, blkf, float(w)), -1, keepdims=True)
            idx_rows = jnp.where(lane == t, it, idx_rows)
            val_rows = jnp.where(lane == t, m, val_rows)
            score = jnp.where(blkf == it, low, score)
        idx_ref[...] = idx_rows.astype(I32)
        val_ref[...] = val_rows


def _attn_s1(qc, kc, vc, *, qpos):
    nb, _, rows, _ = qc.shape
    n_chunk = kc.shape[2]
    t_all = qpos + 1
    n_cmp = t_all // CMP_STRIDE - CMP_R + 1
    n_sel = -(-t_all // SEL_BLOCK)
    assert n_cmp <= n_chunk and n_sel >= SEL_TOP
    w = -(-n_sel // LANES) * LANES
    ov = _overlap_matrix(n_cmp, n_sel, n_chunk, w)
    body = functools.partial(_attn_s1_body, qpos=qpos, n_cmp=n_cmp, n_sel=n_sel)
    hspec = pl.BlockSpec((None, None, rows, HEAD_DIM), lambda b, g: (b, g, 0, 0))
    cspec = pl.BlockSpec((None, None, n_chunk, HEAD_DIM), lambda b, g: (b, g, 0, 0))
    n_rows = nb * KV_HEADS
    assert n_rows % 8 == 0
    pick_spec = pl.BlockSpec((n_rows, LANES), lambda b, g: (0, 0))
    return pl.pallas_call(
        body, grid=(nb, KV_HEADS),
        in_specs=[hspec, cspec, cspec, pl.BlockSpec((n_chunk, w), lambda b, g: (0, 0))],
        out_specs=[hspec, pick_spec, pick_spec],
        out_shape=[jax.ShapeDtypeStruct((nb, KV_HEADS, rows, HEAD_DIM), F32),
                   jax.ShapeDtypeStruct((n_rows, LANES), I32),
                   jax.ShapeDtypeStruct((n_rows, LANES), F32)],
        scratch_shapes=[pltpu.VMEM((n_rows, w), F32)],
        compiler_params=_cparams("arbitrary", "arbitrary"), name="attn_sample_cmp",
    )(qc, kc, vc, ov)


def _attn_s2_body(phys_ref, idx_ref, ok_ref, new_ref, q_ref, *refs, qpos, n_past_blk, past_len):
    kb_refs = refs[:SEL_TOP]
    vb_refs = refs[SEL_TOP:2 * SEL_TOP]
    kn_ref, vn_ref, wk_ref, wv_ref, kwn_ref, vwn_ref, oc_ref, g_ref, o_ref = refs[2 * SEL_TOP:]
    b, g = pl.program_id(0), pl.program_id(1)
    q = q_ref[...]

    def attend(s, v, sn, vn):
        m = jnp.maximum(jnp.max(s, -1, keepdims=True), jnp.max(sn, -1, keepdims=True))
        e, en = jnp.exp(s - m), jnp.exp(sn - m)
        den = jnp.sum(e, -1, keepdims=True) + jnp.sum(en, -1, keepdims=True)
        return (_dot(e.astype(BF16), v) + _dot(en.astype(BF16), vn)) / den

    def head_rows(ref, n):
        return ref[pl.ds(g, n, stride=KV_HEADS), :].astype(BF16)

    wk = wk_ref.shape[0] // KV_HEADS
    kp = (past_len - wk) + lax.broadcasted_iota(I32, (1, wk), 1)
    s = jnp.where((kp <= qpos) & (kp > qpos - WINDOW), _dot_nt(q, head_rows(wk_ref, wk)) * ATTN_SCALE, NEG)
    sn = _dot_nt(q, kwn_ref[...].astype(BF16)) * ATTN_SCALE
    coln = lax.broadcasted_iota(I32, (1, sn.shape[1]), 1)
    o_win = attend(s, head_rows(wv_ref, wk), jnp.where(coln == 0, sn, NEG), vwn_ref[...].astype(BF16))

    kcat = jnp.concatenate([head_rows(r, SEL_BLOCK) for r in kb_refs], 0)
    vcat = jnp.concatenate([head_rows(r, SEL_BLOCK) for r in vb_refs], 0)
    lane = lax.broadcasted_iota(I32, (1, SEL_TOP * SEL_BLOCK), 1)
    limit = jnp.full(lane.shape, -1, I32)
    for kk in range(SEL_TOP):
        t = (b * KV_HEADS + g) * SEL_TOP + kk
        lim = jnp.where((ok_ref[t] == 1) & (idx_ref[t] < n_past_blk), qpos - idx_ref[t] * SEL_BLOCK, -1)
        limit = jnp.where((lane >> 6) == kk, lim, limit)
    s = jnp.where((lane & (SEL_BLOCK - 1)) <= limit, _dot_nt(q, kcat) * ATTN_SCALE, NEG)
    sn = _dot_nt(q, kn_ref[...].astype(BF16)) * ATTN_SCALE
    sn = jnp.where(coln < new_ref[b * KV_HEADS + g], sn, NEG)
    o_sel = attend(s, vcat, sn, vn_ref[...].astype(BF16))

    gt = g_ref[...]
    o_ref[...] = gt[:, 0:1] * oc_ref[...] + gt[:, 1:2] * o_sel + gt[:, 2:3] * o_win


def _attn_s2(qr, pool_k, pool_v, kn, vn, win_k, win_v, kwn, vwn, o_cmp, gates, phys, idx, ok, has_new, *,
             qpos, past_len):
    nb, _, rows, _ = qr.shape
    wk = win_k.shape[0] // (nb * KV_HEADS)
    assert SEL_BLOCK == 64
    hspec = pl.BlockSpec((None, None, rows, HEAD_DIM), lambda b, g, *_: (b, g, 0, 0))
    bspec = lambda k: pl.BlockSpec((SEL_BLOCK * KV_HEADS, HEAD_DIM),
                                   lambda b, g, ph, *_: (ph[(b * KV_HEADS + g) * SEL_TOP + k], 0))
    wspec = pl.BlockSpec((wk * KV_HEADS, HEAD_DIM), lambda b, g, *_: (b, 0))
    blocks = [bspec(k) for k in range(SEL_TOP)]
    body = functools.partial(_attn_s2_body, qpos=qpos, n_past_blk=past_len // SEL_BLOCK, past_len=past_len)
    return pl.pallas_call(
        body,
        grid_spec=pltpu.PrefetchScalarGridSpec(
            num_scalar_prefetch=4, grid=(nb, KV_HEADS),
            in_specs=[hspec] + blocks + blocks + [hspec, hspec, wspec, wspec, hspec, hspec, hspec, hspec],
            out_specs=hspec),
        out_shape=jax.ShapeDtypeStruct((nb, KV_HEADS, rows, HEAD_DIM), F32),
        compiler_params=_cparams("arbitrary", "arbitrary"), name="attn_sample_sel_win",
    )(phys, idx, ok, has_new, qr, *([pool_k] * SEL_TOP), *([pool_v] * SEL_TOP), kn, vn, win_k, win_v, kwn, vwn,
      o_cmp, gates)


def _head_rows(a, rows=8):
    nb = a.shape[0]
    p = a.shape[1] // (KV_HEADS * HEAD_DIM)
    a = a.reshape(nb, KV_HEADS, p, HEAD_DIM)
    return jnp.pad(a, ((0, 0), (0, 0), (0, rows - p), (0, 0)))


def kernel(x_prompt, x_sample, cache_cmp_k, cache_cmp_v, cache_sel_k, cache_sel_v, state_win_k, state_win_v, page_table, norm_mix, norm_ffn, norm_kv, norm_final, gm_w_in, gm_ln_g, gm_ln_b, gm_w_s, gm_b_s, gm_w_out, nsa_w_in, nsa_w_out, kv_w, cmp_pe_k, cmp_w1_k, cmp_w2_k, cmp_pe_v, cmp_w1_v, cmp_w2_v, ff_w_gate, ff_w_up, ff_w_down, moe_router, moe_w_gate, moe_w_up, moe_w_down):
    B, S, D = x_prompt.shape
    DB, DS, _ = x_sample.shape
    assert DS == 1 and D == D_MODEL and S % KEY_TILE == 0
    n_prompt, n_sample = B * S, DB * DS
    n_tok = n_prompt + n_sample
    M = n_prompt + GM_CHUNK
    assert M % ROW_TILE == 0 and M % ROW_TILE_DOWN == 0 and n_sample <= 8
    n_pool = cache_cmp_k.shape[0]
    past_len = page_table.shape[1] * PAGE_SIZE
    qpos_s = past_len
    wk = state_win_k.shape[1]
    assert wk == min(WINDOW, past_len)

    x = jnp.concatenate([x_prompt.reshape(n_prompt, D), x_sample.reshape(n_sample, D),
                         jnp.zeros((M - n_tok, D), F32)], 0)

    pos = jnp.concatenate([jnp.tile(jnp.arange(S), B), jnp.full((n_sample,), past_len), jnp.zeros((M - n_tok,), I32)])
    half = ROT_DIM // 2
    inv = ROPE_THETA ** (-jnp.arange(half, dtype=F32) / half)
    ang = pos.astype(F32)[:, None] * inv[None, :]
    cos, sin = jnp.cos(ang), jnp.sin(ang)
    zeros_h = jnp.zeros((M, half), F32)
    rest = HEAD_DIM - ROT_DIM
    rope_c = jnp.concatenate([cos, cos, jnp.ones((M, rest), F32)], 1)
    rope_s1 = jnp.concatenate([-sin, zeros_h, jnp.zeros((M, rest), F32)], 1)
    rope_s2 = jnp.concatenate([zeros_h, sin, jnp.zeros((M, rest), F32)], 1)
    rope_extras = [(t, (ROW_TILE, HEAD_DIM), lambda j, i: (i, 0)) for t in (rope_c, rope_s1, rope_s2)]

    def residual_mm(xin, w, layer, res, tm, name):
        n = w.shape[2]
        tn = COL_TILE_WIDE if xin.shape[1] == D_MODEL else COL_TILE
        return _ws_matmul(xin, [(w, layer)], _ep_residual, [(n, F32)], tm=tm, tn=tn, n_cols=n,
                          extras=[(res, (tm, tn), lambda j, i: (i, j))], name=name)[0]

    gm_v = []
    kv = None
    for l in range(DEPTH):
        h = _rmsnorm(x, norm_mix[l], BF16, tm=ROW_TILE)
        if l < N_A_LAYERS:
            z = _ws_matmul(h, [(gm_w_in, l)], _ep_gelu, [(2 * GM_DIM, BF16)], tm=ROW_TILE, tn=COL_TILE_WIDE,
                           n_cols=2 * GM_DIM, name="gmlp_in")[0]
            o, v_s = _gm_gate(z, gm_ln_g[l], gm_ln_b[l], gm_w_s[l], gm_b_s[l], n_prompt=n_prompt, n_sample=n_sample)
            gm_v.append(v_s[:n_sample].reshape(DB, DS, GM_DIM))
            x = residual_mm(o, gm_w_out, l, x, ROW_TILE, "gmlp_out")
        else:
            bl = l - N_A_LAYERS
            qc, qr = _ws_matmul(h, [(nsa_w_in, bl)], _ep_q, [(N_HEADS * HEAD_DIM, BF16)] * 2, tm=ROW_TILE,
                                tn=COL_TILE_WIDE, n_cols=N_HEADS * HEAD_DIM, extras=rope_extras, name="nsa_q")
            wg = nsa_w_in[bl, :, N_HEADS * HEAD_DIM:].reshape(D, N_BRANCH, KV_HEADS, Q_PER_KV)
            wg = jnp.transpose(wg, (0, 2, 1, 3)).reshape(D, KV_HEADS, N_BRANCH * Q_PER_KV)
            wg = jnp.pad(wg, ((0, 0), (0, 0), (0, LANES - N_BRANCH * Q_PER_KV))).reshape(1, D, KV_HEADS * LANES)
            gates = _ws_matmul(h, [(wg, 0)], _ep_sigmoid, [(KV_HEADS * LANES, F32)], tm=ROW_TILE, tn=KV_HEADS * LANES,
                               n_cols=KV_HEADS * LANES, name="nsa_gates")[0]
            o = _attn_prompt(qc, qr, kv['kc_p'], kv['vc_p'], kv['rb'], gates, batch=B, seq=S)
            qc_s = _head_rows(qc[n_prompt:n_tok])
            qr_s = _head_rows(qr[n_prompt:n_tok])
            oc_s, idx_s, val_s = _attn_s1(qc_s, kv['kc_s'], kv['vc_s'], qpos=qpos_s)
            idx = idx_s[:, :SEL_TOP].reshape(DB, KV_HEADS, SEL_TOP)
            ok = val_s[:, :SEL_TOP].reshape(DB, KV_HEADS, SEL_TOP) >= 0.0
            n_past_blk = past_len // SEL_BLOCK
            ppb = PAGE_SIZE // SEL_BLOCK
            ip = jnp.minimum(idx, n_past_blk - 1)
            phys = jnp.take_along_axis(page_table, (ip // ppb).reshape(DB, -1), 1).reshape(ip.shape) * ppb + ip % ppb
            has_new = jnp.any((idx >= n_past_blk) & ok, -1)
            g_s = gates[n_prompt:n_tok].reshape(DB, KV_HEADS, LANES)[:, :, :N_BRANCH * Q_PER_KV]
            g_s = jnp.transpose(g_s.reshape(DB, KV_HEADS, N_BRANCH, Q_PER_KV), (0, 1, 3, 2))
            g_s = jnp.pad(g_s, ((0, 0), (0, 0), (0, 8 - Q_PER_KV), (0, HEAD_DIM - N_BRANCH)))
            o_s = _attn_s2(qr_s, kv['pool_k'], kv['pool_v'], kv['ks_new'], kv['vs_new'], kv['win_k'], kv['win_v'],
                           kv['kw_new'], kv['vw_new'], oc_s, g_s, phys.reshape(-1).astype(I32),
                           idx.reshape(-1).astype(I32), ok.reshape(-1).astype(I32), has_new.reshape(-1).astype(I32),
                           qpos=qpos_s, past_len=past_len)
            o_s = o_s[:, :, :Q_PER_KV].reshape(n_sample, N_HEADS * HEAD_DIM).astype(BF16)
            o = jnp.concatenate([o, o_s, jnp.zeros((M - n_tok, N_HEADS * HEAD_DIM), BF16)], 0)
            x = residual_mm(o, nsa_w_out, bl, x, ROW_TILE, "nsa_out")

        if l % 2 == 0:
            h = _rmsnorm(x, norm_ffn[l], BF16, tm=ROW_TILE)
            hh = _ws_matmul(h, [(ff_w_gate, l // 2), (ff_w_up, l // 2)], _ep_swiglu, [(ff_w_gate.shape[2], BF16)],
                            tm=ROW_TILE, tn=COL_TILE, n_cols=ff_w_gate.shape[2], name="ffn_up")[0]
            x = residual_mm(hh, ff_w_down, l // 2, x, ROW_TILE_DOWN, "ffn_down")
        else:
            x = _moe(x, norm_ffn[l], moe_router[l // 2], moe_w_gate, moe_w_up, moe_w_down, l // 2, n_tok=n_tok)

        if l == N_A_LAYERS - 1:
            hkv = _rmsnorm(x, norm_kv, BF16, tm=ROW_TILE)
            r, rb = _ws_matmul(hkv, [(kv_w.reshape(1, D, 6 * KV_SEC), 0)], _ep_kv,
                               [(6 * KV_SEC, F32), (6 * KV_SEC, BF16)], tm=ROW_TILE, tn=COL_TILE_WIDE,
                               n_cols=6 * KV_SEC, extras=rope_extras, name="kv_proj")
            pages_p = r.reshape(M // PAGE_SIZE, PAGE_SIZE, 6 * KV_SEC)
            table_p = jnp.arange(n_prompt // PAGE_SIZE, dtype=I32).reshape(B, S // PAGE_SIZE)
            r_s = r[n_prompt:n_tok]
            sec_s = lambda i: r_s[:, i * KV_SEC:(i + 1) * KV_SEC]
            rows2d = lambda c: c.reshape(-1, HEAD_DIM)
            kv = dict(
                rb=rb,
                kc_p=_compress(pages_p, table_p, 0, cmp_pe_k, cmp_w1_k, cmp_w2_k),
                vc_p=_compress(pages_p, table_p, 1, cmp_pe_v, cmp_w1_v, cmp_w2_v),
                kc_s=_compress_cache(cache_cmp_k, page_table, cmp_pe_k, cmp_w1_k, cmp_w2_k),
                vc_s=_compress_cache(cache_cmp_v, page_table, cmp_pe_v, cmp_w1_v, cmp_w2_v),
                pool_k=rows2d(cache_sel_k), pool_v=rows2d(cache_sel_v),
                ks_new=_head_rows(sec_s(2)), vs_new=_head_rows(sec_s(3)),
                kw_new=_head_rows(sec_s(4)), vw_new=_head_rows(sec_s(5)),
                win_k=rows2d(state_win_k), win_v=rows2d(state_win_v),
            )
            r_p = r[:n_prompt]
            pg = lambda i: r_p[:, i * KV_SEC:(i + 1) * KV_SEC].reshape(B, S // PAGE_SIZE, PAGE_SIZE, KV_HEADS, HEAD_DIM)
            wkp = min(WINDOW, S)
            tail = lambda i: r_p[:, i * KV_SEC:(i + 1) * KV_SEC].reshape(B, S, KV_HEADS, HEAD_DIM)[:, S - wkp:]
            row_s = lambda i: sec_s(i).reshape(DB, DS, KV_HEADS, HEAD_DIM)
            state_p = (pg(0), pg(1), pg(2), pg(3), tail(4), tail(5))
            state_s = (row_s(0), row_s(1), row_s(2), row_s(3),
                       jnp.concatenate([state_win_k, row_s(4)], 1)[:, DS:],
                       jnp.concatenate([state_win_v, row_s(5)], 1)[:, DS:])

    y_prompt = _rmsnorm(x, norm_final, F32, tm=S // 2, n_rows=n_prompt).reshape(B, S, D)
    y_sample = _rmsnorm(x, norm_final, F32, tm=8, first_row=n_prompt, n_rows=8)[:n_sample].reshape(DB, DS, D)
    return (y_prompt, y_sample) + state_p + state_s + (jnp.stack(gm_v, 0),)
```

```python
import functools

import numpy as np
import jax
import jax.numpy as jnp
from jax import lax
from jax.experimental import pallas as pl
from jax.experimental.pallas import tpu as pltpu

F32 = jnp.float32
BF16 = jnp.bfloat16
I32 = jnp.int32

D_MODEL = 2048
DEPTH = 4
PAGE_SIZE = 128
N_A_LAYERS = DEPTH // 2
GM_CHUNK = 128
GM_DIM = D_MODEL
GM_GROUPS = 16
GM_GW = GM_DIM // GM_GROUPS
N_HEADS = 16
HEAD_DIM = 128
KV_HEADS = 4
Q_PER_KV = N_HEADS // KV_HEADS
ROT_DIM = HEAD_DIM // 4
ROPE_THETA = 500000.0
CMP_LEN = 32
CMP_STRIDE = 16
CMP_R = CMP_LEN // CMP_STRIDE
SEL_BLOCK = 64
SEL_TOP = 16
WINDOW = 512
N_BRANCH = 3
KV_SEC = KV_HEADS * HEAD_DIM
ATTN_SCALE = HEAD_DIM ** -0.5
N_EXPERTS = 8
TOP_K = 2
EPS = 1e-6
NEG = -1e30
FORCE_SCORE = 1e6

LANES = 128
VMEM_LIMIT_BYTES = 56 * 1024 * 1024
ROW_TILE = 832
ROW_TILE_DOWN = 416
COL_TILE = 512
COL_TILE_WIDE = 1024
MOE_BLOCK = 256
MOE_SEG_BLOCKS = 10
MOE_FF_TILE = 256
KEY_TILE = 256
Q_TILE = 256
PAGES_PER_STEP = 16


def _cparams(*sem):
    return pltpu.CompilerParams(dimension_semantics=sem, vmem_limit_bytes=VMEM_LIMIT_BYTES)


def _dot(a, b):
    return jnp.dot(a, b, preferred_element_type=F32)


def _dot_nt(a, b):
    return lax.dot_general(a, b, (((1,), (1,)), ((), ())), preferred_element_type=F32)


def _dot3(a, b):
    ah = a.astype(BF16)
    al = (a - ah.astype(F32)).astype(BF16)
    bh = b.astype(BF16)
    bl = (b - bh.astype(F32)).astype(BF16)
    return _dot(ah, bh) + _dot(al, bh) + _dot(ah, bl)


def _ws_body(*refs, nw, n_extra, n_out, epilogue):
    x_ref = refs[0]
    w_refs = refs[1:1 + nw]
    e_refs = refs[1 + nw:1 + nw + n_extra]
    o_refs = refs[1 + nw + n_extra:1 + nw + n_extra + n_out]
    wb_refs = refs[1 + nw + n_extra + n_out:]
    j = pl.program_id(0)

    @pl.when(pl.program_id(1) == 0)
    def _():
        for w_ref, wb in zip(w_refs, wb_refs):
            wb[...] = w_ref[...].astype(BF16)

    accs = [_dot(x_ref[...], wb[...]) for wb in wb_refs]
    outs = epilogue(j, accs, e_refs)
    for o_ref, o in zip(o_refs, outs):
        o_ref[...] = o.astype(o_ref.dtype)


def _ws_matmul(x, ws, epilogue, outs, *, tm, tn, n_cols, extras=(), name):
    M, K = x.shape
    nj = n_cols // tn
    ni = M // tm
    assert nj * tn == n_cols and ni * tm == M
    in_specs = [pl.BlockSpec((tm, K), lambda j, i: (i, 0))]
    in_specs += [pl.BlockSpec((None, K, tn), lambda j, i, layer=layer: (layer, 0, j)) for _, layer in ws]
    ws = [w for w, _ in ws]
    in_specs += [pl.BlockSpec(bs, im) for _, bs, im in extras]
    out_shape = [jax.ShapeDtypeStruct((M, c), dt) for c, dt in outs]
    out_specs = [pl.BlockSpec((tm, c // nj), lambda j, i: (i, j)) for c, _ in outs]
    body = functools.partial(_ws_body, nw=len(ws), n_extra=len(extras), n_out=len(outs), epilogue=epilogue)
    res = pl.pallas_call(
        body, grid=(nj, ni), in_specs=in_specs, out_specs=out_specs, out_shape=out_shape,
        scratch_shapes=[pltpu.VMEM((K, tn), BF16) for _ in ws],
        compiler_params=_cparams("arbitrary", "arbitrary"), name=name,
    )(x, *ws, *[a for a, _, _ in extras])
    return res


def _rope_tile(acc, c_ref, s1_ref, s2_ref):
    c, s1, s2 = c_ref[...], s1_ref[...], s2_ref[...]
    parts = []
    for h in range(acc.shape[1] // HEAD_DIM):
        xh = acc[:, h * HEAD_DIM:(h + 1) * HEAD_DIM]
        parts.append(xh * c + pltpu.roll(xh, HEAD_DIM - ROT_DIM // 2, 1) * s1 + pltpu.roll(xh, ROT_DIM // 2, 1) * s2)
    return jnp.concatenate(parts, 1)


def _ep_gelu(j, accs, e):
    return [jax.nn.gelu(accs[0])]


def _ep_residual(j, accs, e):
    return [e[0][...] + accs[0]]


def _ep_swiglu(j, accs, e):
    return [jax.nn.silu(accs[0]) * accs[1]]


def _ep_sigmoid(j, accs, e):
    return [jax.nn.sigmoid(accs[0])]


def _ep_kv(j, accs, e):
    acc = accs[0]
    per_tile = acc.shape[1] // KV_SEC
    parts = []
    for t in range(per_tile):
        a = acc[:, t * KV_SEC:(t + 1) * KV_SEC]
        sec = j * per_tile + t
        parts.append(jnp.where((sec == 2) | (sec == 4), _rope_tile(a, *e), a))
    r = parts[0] if per_tile == 1 else jnp.concatenate(parts, 1)
    return [r, r]


def _ep_q(j, accs, e):
    acc = accs[0]
    return [acc, _rope_tile(acc, *e)]


def _rms(x, g):
    return (x * lax.rsqrt(jnp.mean(x * x, -1, keepdims=True) + EPS)) * g


def _rms_body(x_ref, g_ref, o_ref):
    o_ref[...] = _rms(x_ref[...], g_ref[...]).astype(o_ref.dtype)


def _rmsnorm(x, g, out_dtype, *, tm, first_row=0, n_rows=None):
    M, D = x.shape
    n_rows = M if n_rows is None else n_rows
    assert n_rows % tm == 0 and first_row % tm == 0
    off = first_row // tm
    return pl.pallas_call(
        _rms_body, grid=(n_rows // tm,),
        in_specs=[pl.BlockSpec((tm, D), lambda i: (i + off, 0)), pl.BlockSpec((1, D), lambda i: (0, 0))],
        out_specs=pl.BlockSpec((tm, D), lambda i: (i, 0)),
        out_shape=jax.ShapeDtypeStruct((n_rows, D), out_dtype),
        compiler_params=_cparams("arbitrary"), name="rmsnorm",
    )(x, g.reshape(1, D))


def _rms_router_body(x_ref, g_ref, r_ref, h_ref, te_ref, tg_ref):
    h = _rms(x_ref[...], g_ref[...])
    h_ref[...] = h
    lane = lax.broadcasted_iota(I32, (1, LANES), 1).astype(F32)
    logits = jnp.where(lane < N_EXPERTS, _dot3(h, r_ref[...]), -jnp.inf)
    m1 = jnp.max(logits, -1, keepdims=True)
    i1 = jnp.min(jnp.where(logits == m1, lane, float(LANES)), -1, keepdims=True)
    rest = jnp.where(lane == i1, -jnp.inf, logits)
    m2 = jnp.max(rest, -1, keepdims=True)
    i2 = jnp.min(jnp.where(rest == m2, lane, float(LANES)), -1, keepdims=True)
    e2 = jnp.exp(m2 - m1)
    den = 1.0 + e2
    te_ref[...] = jnp.where(lane == 0, i1, jnp.where(lane == 1, i2, 0.0)).astype(I32)
    tg_ref[...] = jnp.where(lane == 0, 1.0 / den, jnp.where(lane == 1, e2 / den, 0.0))


def _rmsnorm_router(x, g, router, *, tm):
    M, D = x.shape
    rp = jnp.pad(router, ((0, 0), (0, LANES - N_EXPERTS)))
    return pl.pallas_call(
        _rms_router_body, grid=(M // tm,),
        in_specs=[pl.BlockSpec((tm, D), lambda i: (i, 0)), pl.BlockSpec((1, D), lambda i: (0, 0)),
                  pl.BlockSpec((D, LANES), lambda i: (0, 0))],
        out_specs=[pl.BlockSpec((tm, D), lambda i: (i, 0)), pl.BlockSpec((tm, LANES), lambda i: (i, 0)),
                   pl.BlockSpec((tm, LANES), lambda i: (i, 0))],
        out_shape=[jax.ShapeDtypeStruct((M, D), F32), jax.ShapeDtypeStruct((M, LANES), I32),
                   jax.ShapeDtypeStruct((M, LANES), F32)],
        compiler_params=_cparams("arbitrary"), name="rmsnorm_router",
    )(x, g.reshape(1, D), rp)


def _gm_gate_body(z_ref, lng_ref, lnb_ref, ws_ref, bst_ref, w00_ref, b00_ref, o_ref, vs_ref, *,
                  n_prompt_chunks, n_sample):
    c = pl.program_id(0)
    z = z_ref[...]
    u = z[:, :GM_DIM].astype(F32)
    v = z[:, GM_DIM:].astype(F32)
    mu = jnp.mean(v, -1, keepdims=True)
    vc = v - mu
    vn = (vc * lax.rsqrt(jnp.mean(vc * vc, -1, keepdims=True) + EPS)) * lng_ref[...] + lnb_ref[...]

    @pl.when(c < n_prompt_chunks)
    def _():
        vb = vn.astype(BF16)
        tri = lax.broadcasted_iota(I32, (GM_CHUNK, GM_CHUNK), 0) >= lax.broadcasted_iota(I32, (GM_CHUNK, GM_CHUNK), 1)
        for g in range(GM_GROUPS):
            sl = slice(g * GM_GW, (g + 1) * GM_GW)
            wg = jnp.where(tri, ws_ref[g], 0.0).astype(BF16)
            s = _dot(wg, vb[:, sl]) + bst_ref[:, g:g + 1]
            o_ref[:, sl] = (u[:, sl] * s).astype(o_ref.dtype)

    @pl.when(c >= n_prompt_chunks)
    def _():
        row = lax.broadcasted_iota(I32, (GM_CHUNK, 1), 0)
        s = w00_ref[...] * vn + b00_ref[...]
        o_ref[...] = jnp.where(row < n_sample, u * s, 0.0).astype(o_ref.dtype)
        vs_ref[...] = vn[:vs_ref.shape[0]]


def _gm_gate(z, ln_g, ln_b, w_s, b_s, *, n_prompt, n_sample):
    M = z.shape[0]
    n_pc = n_prompt // GM_CHUNK
    assert n_prompt % GM_CHUNK == 0 and M == (n_pc + 1) * GM_CHUNK and n_sample <= GM_CHUNK
    n_vs = -(-n_sample // 8) * 8
    w00 = jnp.repeat(w_s[:, 0, 0], GM_GW).reshape(1, GM_DIM)
    b00 = jnp.repeat(b_s[:, 0], GM_GW).reshape(1, GM_DIM)
    body = functools.partial(_gm_gate_body, n_prompt_chunks=n_pc, n_sample=n_sample)
    full = lambda *shape: pl.BlockSpec(shape, lambda c: (0,) * len(shape))
    return pl.pallas_call(
        body, grid=(n_pc + 1,),
        in_specs=[pl.BlockSpec((GM_CHUNK, 2 * GM_DIM), lambda c: (c, 0)), full(1, GM_DIM), full(1, GM_DIM),
                  full(GM_GROUPS, GM_CHUNK, GM_CHUNK), full(GM_CHUNK, GM_GROUPS), full(1, GM_DIM), full(1, GM_DIM)],
        out_specs=[pl.BlockSpec((GM_CHUNK, GM_DIM), lambda c: (c, 0)), full(n_vs, GM_DIM)],
        out_shape=[jax.ShapeDtypeStruct((M, GM_DIM), BF16), jax.ShapeDtypeStruct((n_vs, GM_DIM), F32)],
        compiler_params=_cparams("arbitrary"), name="gmlp_gate",
    )(z, ln_g.reshape(1, GM_DIM), ln_b.reshape(1, GM_DIM), w_s, b_s.T, w00, b00)


def _moe_gather_body(rt_ref, nu_ref, h_hbm, o_ref, buf, sem):
    i = pl.program_id(0)
    n_used = nu_ref[0]

    def row_copy(tok, slot, r):
        return pltpu.make_async_copy(h_hbm.at[pl.ds(tok, 1), :], buf.at[slot, pl.ds(r, 1), :], sem.at[slot])

    def start_block(blk, slot):
        def issue(r, c):
            row_copy(rt_ref[blk * MOE_BLOCK + r], slot, r).start()
            return c

        lax.fori_loop(0, MOE_BLOCK, issue, 0, unroll=8)

    @pl.when((i == 0) & (n_used > 0))
    def _():
        start_block(0, 0)

    @pl.when(i + 1 < n_used)
    def _():
        start_block(i + 1, (i + 1) % 2)

    @pl.when(i < n_used)
    def _():
        slot = i % 2

        def wait(r, c):
            row_copy(0, slot, r).wait()
            return c

        lax.fori_loop(0, MOE_BLOCK, wait, 0, unroll=8)
        o_ref[...] = buf[slot].astype(o_ref.dtype)

    @pl.when(i >= n_used)
    def _():
        o_ref[...] = jnp.zeros(o_ref.shape, o_ref.dtype)


def _moe_gather(h, row_tok, n_used, n_blk):
    D = h.shape[1]
    return pl.pallas_call(
        _moe_gather_body,
        grid_spec=pltpu.PrefetchScalarGridSpec(
            num_scalar_prefetch=2, grid=(n_blk,),
            in_specs=[pl.BlockSpec(memory_space=pl.ANY)],
            out_specs=pl.BlockSpec((MOE_BLOCK, D), lambda i, rt, nu: (i, 0)),
            scratch_shapes=[pltpu.VMEM((2, MOE_BLOCK, D), F32), pltpu.SemaphoreType.DMA((2,))]),
        out_shape=jax.ShapeDtypeStruct((n_blk * MOE_BLOCK, D), BF16),
        compiler_params=_cparams("arbitrary"), name="moe_gather",
    )(row_tok, n_used, h)


def _moe_ffn_body(se_ref, sb_ref, sn_ref, ns_ref, nu_ref, xb_hbm, wg_ref, wu_ref, wd_ref, yb_hbm,
                  x_vmem, y_acc, wgb, wub, wdb, sem_in, sem_out, *, n_j, n_seg_max, n_blk):
    s, j = pl.program_id(0), pl.program_id(1)
    nblk = sn_ref[jnp.minimum(s, n_seg_max - 1)]
    row0 = sb_ref[jnp.minimum(s, n_seg_max - 1)] * MOE_BLOCK

    def block_copy(src, src_row, dst, dst_row, sem):
        return pltpu.make_async_copy(src.at[pl.ds(pl.multiple_of(src_row, MOE_BLOCK), MOE_BLOCK), :],
                                     dst.at[pl.ds(pl.multiple_of(dst_row, MOE_BLOCK), MOE_BLOCK), :], sem)

    def for_blocks(lo, hi, fn):
        def body(blk, c):
            fn(blk)
            return c

        lax.fori_loop(lo, hi, body, 0)

    @pl.when(s < ns_ref[0])
    def _():
        @pl.when(j == 0)
        def _():
            for_blocks(0, nblk, lambda blk: block_copy(xb_hbm, row0 + blk * MOE_BLOCK, x_vmem, blk * MOE_BLOCK,
                                                       sem_in).start())
            for_blocks(0, nblk, lambda blk: block_copy(xb_hbm, row0, x_vmem, blk * MOE_BLOCK, sem_in).wait())

        wgb[...] = wg_ref[...].astype(BF16)
        wub[...] = wu_ref[...].astype(BF16)
        wdb[...] = wd_ref[...].astype(BF16)

        def update(blk, n, first):
            rows = pl.ds(pl.multiple_of(blk * MOE_BLOCK, MOE_BLOCK), n * MOE_BLOCK)
            x = x_vmem[rows, :]
            hidden = (jax.nn.silu(_dot(x, wgb[...])) * _dot(x, wub[...])).astype(BF16)
            part = _dot(hidden, wdb[...])
            y_acc[rows, :] = part if first else y_acc[rows, :] + part

        def sweep(first):
            pairs = nblk // 2
            for_blocks(0, pairs, lambda c: update(2 * c, 2, first))

            @pl.when(nblk % 2 == 1)
            def _():
                update(2 * pairs, 1, first)

        @pl.when(j == 0)
        def _():
            sweep(True)

        @pl.when(j > 0)
        def _():
            sweep(False)

        @pl.when(j == n_j - 1)
        def _():
            for_blocks(0, nblk, lambda blk: block_copy(y_acc, blk * MOE_BLOCK, yb_hbm, row0 + blk * MOE_BLOCK,
                                                       sem_out).start())
            for_blocks(0, nblk, lambda blk: block_copy(y_acc, blk * MOE_BLOCK, yb_hbm, row0, sem_out).wait())

    @pl.when((s == n_seg_max) & (j == 0))
    def _():
        y_acc[pl.ds(0, MOE_BLOCK), :] = jnp.zeros((MOE_BLOCK, y_acc.shape[1]), F32)
        for_blocks(nu_ref[0], n_blk, lambda blk: block_copy(y_acc, 0, yb_hbm, blk * MOE_BLOCK, sem_out).start())
        for_blocks(nu_ref[0], n_blk, lambda blk: block_copy(y_acc, 0, yb_hbm, 0, sem_out).wait())


def _moe_ffn(xb, wg, wu, wd, layer, seg_e, seg_blk, seg_nblk, n_seg, n_used, *, n_seg_max):
    M, D = xb.shape
    F = wg.shape[3]
    n_j = F // MOE_FF_TILE
    assert n_j * MOE_FF_TILE == F
    n_blk = M // MOE_BLOCK

    def w_index(s, j, se, sb, sn, ns, nu):
        live = s < ns[0]
        return se[jnp.minimum(s, ns[0] - 1)], jnp.where(live, j, n_j - 1)

    def up_map(s, j, *pref):
        e, jj = w_index(s, j, *pref)
        return layer, e, 0, jj

    def down_map(s, j, *pref):
        e, jj = w_index(s, j, *pref)
        return layer, e, jj, 0

    seg_rows = MOE_SEG_BLOCKS * MOE_BLOCK
    body = functools.partial(_moe_ffn_body, n_j=n_j, n_seg_max=n_seg_max, n_blk=n_blk)
    return pl.pallas_call(
        body,
        grid_spec=pltpu.PrefetchScalarGridSpec(
            num_scalar_prefetch=5, grid=(n_seg_max + 1, n_j),
            in_specs=[pl.BlockSpec(memory_space=pl.ANY),
                      pl.BlockSpec((None, None, D, MOE_FF_TILE), up_map),
                      pl.BlockSpec((None, None, D, MOE_FF_TILE), up_map),
                      pl.BlockSpec((None, None, MOE_FF_TILE, D), down_map)],
            out_specs=pl.BlockSpec(memory_space=pl.ANY),
            scratch_shapes=[pltpu.VMEM((seg_rows, D), BF16), pltpu.VMEM((seg_rows, D), F32),
                            pltpu.VMEM((D, MOE_FF_TILE), BF16), pltpu.VMEM((D, MOE_FF_TILE), BF16),
                            pltpu.VMEM((MOE_FF_TILE, D), BF16),
                            pltpu.SemaphoreType.DMA(()), pltpu.SemaphoreType.DMA(())]),
        out_shape=jax.ShapeDtypeStruct((M, D), F32),
        compiler_params=_cparams("arbitrary", "arbitrary"), name="moe_ffn",
    )(seg_e, seg_blk, seg_nblk, n_seg, n_used, xb, wg, wu, wd)


def _moe_combine_body(d0_ref, d1_ref, x_ref, g_ref, yb_hbm, o_ref, b0, b1, sem, *, tm):
    i = pl.program_id(0)

    def row_copy(src_row, buf, slot, r):
        return pltpu.make_async_copy(yb_hbm.at[pl.ds(src_row, 1), :], buf.at[slot, pl.ds(r, 1), :], sem.at[slot])

    def start_tile(t, slot):
        def issue(r, c):
            row_copy(d0_ref[t * tm + r], b0, slot, r).start()
            row_copy(d1_ref[t * tm + r], b1, slot, r).start()
            return c

        lax.fori_loop(0, tm, issue, 0, unroll=8)

    @pl.when(i == 0)
    def _():
        start_tile(0, 0)

    @pl.when(i + 1 < pl.num_programs(0))
    def _():
        start_tile(i + 1, (i + 1) % 2)

    slot = i % 2

    def wait(r, c):
        row_copy(0, b0, slot, r).wait()
        row_copy(0, b1, slot, r).wait()
        return c

    lax.fori_loop(0, tm, wait, 0, unroll=8)
    g = g_ref[...]
    o_ref[...] = x_ref[...] + (g[:, 0:1] * b0[slot] + g[:, 1:2] * b1[slot])


def _moe_combine(x, gates, yb, d0, d1, *, tm):
    M, D = x.shape
    body = functools.partial(_moe_combine_body, tm=tm)
    return pl.pallas_call(
        body,
        grid_spec=pltpu.PrefetchScalarGridSpec(
            num_scalar_prefetch=2, grid=(M // tm,),
            in_specs=[pl.BlockSpec((tm, D), lambda i, a, b: (i, 0)), pl.BlockSpec((tm, LANES), lambda i, a, b: (i, 0)),
                      pl.BlockSpec(memory_space=pl.ANY)],
            out_specs=pl.BlockSpec((tm, D), lambda i, a, b: (i, 0)),
            scratch_shapes=[pltpu.VMEM((2, tm, D), F32), pltpu.VMEM((2, tm, D), F32), pltpu.SemaphoreType.DMA((2,))]),
        out_shape=jax.ShapeDtypeStruct((M, D), F32),
        compiler_params=_cparams("arbitrary"), name="moe_combine",
    )(d0, d1, x, gates, yb)


def _moe(x, g_norm, router, wg, wu, wd, layer, *, n_tok):
    M, D = x.shape
    h, te, tg = _rmsnorm_router(x, g_norm, router, tm=ROW_TILE)
    a = n_tok * TOP_K
    n_blk = -(-a // MOE_BLOCK) + N_EXPERTS
    e_flat = te[:n_tok, :TOP_K].reshape(a)
    onehot = (e_flat[:, None] == jnp.arange(N_EXPERTS, dtype=I32)[None, :]).astype(I32)
    csum = jnp.cumsum(onehot, 0)
    pos = jnp.take_along_axis(csum, e_flat[:, None], 1)[:, 0] - 1
    counts = csum[-1]
    padded = (counts + MOE_BLOCK - 1) // MOE_BLOCK * MOE_BLOCK
    end_pad = jnp.cumsum(padded)
    start_pad = end_pad - padded
    dest = (start_pad[e_flat] + pos).astype(I32)
    tok = jnp.arange(a, dtype=I32) // TOP_K
    row_tok = jnp.full((n_blk * MOE_BLOCK,), n_tok, I32).at[dest].set(tok)
    n_used = (end_pad[-1] // MOE_BLOCK).astype(I32).reshape(1)
    n_seg_max = n_blk // MOE_SEG_BLOCKS + N_EXPERTS + 1
    e_blocks = padded // MOE_BLOCK
    e_segs = (e_blocks + MOE_SEG_BLOCKS - 1) // MOE_SEG_BLOCKS
    seg_end = jnp.cumsum(e_segs)
    s_ids = jnp.arange(n_seg_max, dtype=I32)
    seg_e = jnp.minimum(jnp.sum((seg_end[None, :] <= s_ids[:, None]).astype(I32), 1), N_EXPERTS - 1)
    local = s_ids - (seg_end - e_segs)[seg_e]
    seg_blk = (start_pad // MOE_BLOCK)[seg_e] + local * MOE_SEG_BLOCKS
    seg_nblk = jnp.where(s_ids < seg_end[-1], jnp.clip(e_blocks[seg_e] - local * MOE_SEG_BLOCKS, 0, MOE_SEG_BLOCKS), 0)
    d = jnp.pad(dest.reshape(n_tok, TOP_K), ((0, M - n_tok), (0, 0)))
    row = jnp.arange(M, dtype=I32)[:, None]
    gates = jnp.where(row < n_tok, tg, 0.0)

    xb = _moe_gather(h, row_tok, n_used, n_blk)
    yb = _moe_ffn(xb, wg, wu, wd, layer, seg_e.astype(I32), seg_blk.astype(I32), seg_nblk.astype(I32),
                  seg_end[-1:].astype(I32), n_used, n_seg_max=n_seg_max)
    return _moe_combine(x, gates, yb, d[:, 0], d[:, 1], tm=ROW_TILE_DOWN)


def _cmp_z_body(pt_ref, *refs):
    n_in = len(refs) - 2
    page_refs = refs[:n_in]
    w_ref = refs[n_in]
    o_ref = refs[n_in + 1]
    n_ch = PAGE_SIZE // CMP_STRIDE
    cols = [jnp.concatenate([pg[pl.ds(s, n_ch, stride=CMP_STRIDE), :] for pg in page_refs], 0).astype(BF16)
            for s in range(CMP_STRIDE)]
    acc = _dot(jnp.concatenate(cols, 1), w_ref[...])
    m = PAGES_PER_STEP * n_ch
    for k in range(KV_HEADS):
        o_ref[k] = acc[k * m:(k + 1) * m]


def _cmp_z(pages, table, sec, w1cat):
    nb, npg = table.shape
    assert npg % PAGES_PER_STEP == 0
    n_ch = PAGE_SIZE // CMP_STRIDE
    m = PAGES_PER_STEP * n_ch
    specs = [pl.BlockSpec((None, PAGE_SIZE, HEAD_DIM),
                          lambda b, g, pt, k=k, q=q: (pt[b, g * PAGES_PER_STEP + q], 0, sec * KV_HEADS + k))
             for k in range(KV_HEADS) for q in range(PAGES_PER_STEP)]
    return pl.pallas_call(
        _cmp_z_body,
        grid_spec=pltpu.PrefetchScalarGridSpec(
            num_scalar_prefetch=1, grid=(nb, npg // PAGES_PER_STEP),
            in_specs=specs + [pl.BlockSpec((CMP_STRIDE * HEAD_DIM, CMP_R * HEAD_DIM), lambda b, g, pt: (0, 0))],
            out_specs=pl.BlockSpec((None, KV_HEADS, m, CMP_R * HEAD_DIM), lambda b, g, pt: (b, 0, g, 0))),
        out_shape=jax.ShapeDtypeStruct((nb, KV_HEADS, npg * n_ch, CMP_R * HEAD_DIM), F32),
        compiler_params=_cparams("arbitrary", "arbitrary"), name="cmp_z",
    )(table, *([pages] * len(specs)), w1cat)


def _cmp_cache_z_body(pt_ref, *refs):
    page_refs = refs[:PAGES_PER_STEP]
    w_ref, o_ref = refs[PAGES_PER_STEP:]
    tiles_per_chunk = CMP_STRIDE * KV_HEADS // 8
    lhs = jnp.concatenate(
        [jnp.concatenate([pg[:, 8 * t:8 * t + 8, :].reshape(-1, HEAD_DIM) for t in range(tiles_per_chunk)], 1)
         for pg in page_refs], 0).astype(BF16)
    even, odd = _dot(lhs, w_ref[0]), _dot(lhs, w_ref[1])
    o_ref[...] = even + pltpu.roll(odd, even.shape[0] - KV_HEADS, 0)


def _cmp_cache_fin_body(z_ref, pe_ref, w1_ref, w2_ref, o_ref, y_ref):
    z = z_ref[...]
    n = z.shape[0]
    pe = jnp.broadcast_to(pe_ref[...], (8, pe_ref.shape[1])).astype(BF16)
    pre = _dot(pe, w1_ref[...].astype(BF16))[0:1]
    pre = pre + z[:, :HEAD_DIM]
    pre = pre + pltpu.roll(z[:, HEAD_DIM:], n - 8, 0)
    y_ref[...] = _dot(jax.nn.gelu(pre).astype(BF16), w2_ref[...].astype(BF16))
    for k in range(KV_HEADS):
        o_ref[k] = y_ref[pl.ds(k, n // 8, stride=8), :].astype(o_ref.dtype)


def _compress_cache(cache, table, pe, w1, w2):
    assert KV_HEADS == 4 and CMP_STRIDE % 2 == 0
    nb, npg = table.shape
    assert npg % PAGES_PER_STEP == 0
    n_ch = PAGE_SIZE // CMP_STRIDE
    chunk_rows = CMP_STRIDE * KV_HEADS
    pages = cache.reshape(-1, chunk_rows, HEAD_DIM)
    w = _w1cat(w1).reshape(CMP_STRIDE // 2, 2, HEAD_DIM, CMP_R * HEAD_DIM)
    w = jnp.transpose(w, (1, 0, 2, 3)).reshape(2, CMP_STRIDE // 2 * HEAD_DIM, CMP_R * HEAD_DIM)
    m = PAGES_PER_STEP * n_ch * 8
    z = pl.pallas_call(
        _cmp_cache_z_body,
        grid_spec=pltpu.PrefetchScalarGridSpec(
            num_scalar_prefetch=1, grid=(nb, npg // PAGES_PER_STEP),
            in_specs=[pl.BlockSpec((n_ch, chunk_rows, HEAD_DIM),
                                   lambda b, g, pt, q=q: (pt[b, g * PAGES_PER_STEP + q], 0, 0))
                      for q in range(PAGES_PER_STEP)]
            + [pl.BlockSpec(w.shape, lambda b, g, pt: (0, 0, 0))],
            out_specs=pl.BlockSpec((None, m, CMP_R * HEAD_DIM), lambda b, g, pt: (b, g, 0))),
        out_shape=jax.ShapeDtypeStruct((nb, npg * n_ch * 8, CMP_R * HEAD_DIM), F32),
        compiler_params=_cparams("arbitrary", "arbitrary"), name="cmp_cache_z",
    )(table, *([pages] * PAGES_PER_STEP), w)
    n = npg * n_ch
    return pl.pallas_call(
        _cmp_cache_fin_body, grid=(nb,),
        in_specs=[pl.BlockSpec((None, n * 8, CMP_R * HEAD_DIM), lambda b: (b, 0, 0)),
                  pl.BlockSpec((1, CMP_LEN * HEAD_DIM), lambda b: (0, 0)),
                  pl.BlockSpec((CMP_LEN * HEAD_DIM, HEAD_DIM), lambda b: (0, 0)),
                  pl.BlockSpec((HEAD_DIM, HEAD_DIM), lambda b: (0, 0))],
        out_specs=pl.BlockSpec((None, KV_HEADS, n, HEAD_DIM), lambda b: (b, 0, 0, 0)),
        out_shape=jax.ShapeDtypeStruct((nb, KV_HEADS, n, HEAD_DIM), BF16),
        scratch_shapes=[pltpu.VMEM((n * 8, HEAD_DIM), F32)],
        compiler_params=_cparams("arbitrary"), name="cmp_cache_fin",
    )(z, pe.reshape(1, -1), w1, w2)


def _cmp_fin_body(z_ref, pe_ref, w1_ref, w2_ref, o_ref):
    z = z_ref[...]
    n = z.shape[0]
    pe = jnp.broadcast_to(pe_ref[...], (8, pe_ref.shape[1])).astype(BF16)
    pre = _dot(pe, w1_ref[...].astype(BF16))[0:1]
    pre = pre + z[:, :HEAD_DIM]
    pre = pre + pltpu.roll(z[:, HEAD_DIM:], n - 1, 0)
    o_ref[...] = _dot(jax.nn.gelu(pre).astype(BF16), w2_ref[...].astype(BF16)).astype(o_ref.dtype)


def _cmp_fin(z, pe, w1, w2):
    nb, _, n, _ = z.shape
    return pl.pallas_call(
        _cmp_fin_body, grid=(nb, KV_HEADS),
        in_specs=[pl.BlockSpec((None, None, n, CMP_R * HEAD_DIM), lambda b, k: (b, k, 0, 0)),
                  pl.BlockSpec((1, CMP_LEN * HEAD_DIM), lambda b, k: (0, 0)),
                  pl.BlockSpec((CMP_LEN * HEAD_DIM, HEAD_DIM), lambda b, k: (0, 0)),
                  pl.BlockSpec((HEAD_DIM, HEAD_DIM), lambda b, k: (0, 0))],
        out_specs=pl.BlockSpec((None, None, n, HEAD_DIM), lambda b, k: (b, k, 0, 0)),
        out_shape=jax.ShapeDtypeStruct((nb, KV_HEADS, n, HEAD_DIM), BF16),
        compiler_params=_cparams("arbitrary", "arbitrary"), name="cmp_fin",
    )(z, pe.reshape(1, -1), w1, w2)


def _w1cat(w1):
    w = w1.reshape(CMP_R, CMP_STRIDE, HEAD_DIM, HEAD_DIM)
    return jnp.transpose(w, (1, 2, 0, 3)).reshape(CMP_STRIDE * HEAD_DIM, CMP_R * HEAD_DIM).astype(BF16)


def _compress(pages, table, sec, pe, w1, w2):
    return _cmp_fin(_cmp_z(pages, table, sec, _w1cat(w1)), pe, w1, w2)


def _overlap_matrix(nc, ns, rows, cols):
    cs = np.arange(nc) * CMP_STRIDE
    ss = np.arange(ns) * SEL_BLOCK
    ov = np.minimum(cs[:, None] + CMP_LEN, ss[None, :] + SEL_BLOCK) - np.maximum(cs[:, None], ss[None, :])
    ov = np.clip(ov, 0, None).astype(np.float32) / np.float32(CMP_LEN)
    out = np.zeros((rows, cols), np.float32)
    out[:nc, :ns] = ov
    return jnp.asarray(out)


def _attn_prompt_body(qc_ref, qr_ref, kc_ref, vc_ref, ks_ref, vs_ref, kw_ref, vw_ref, g_ref, ovt_ref, ext_ref, o_ref,
                      allow_ref, vct_ref, vst_ref, vwt_ref, q4_ref, s_ref, ocmp_ref, osel_ref, acc_ref, *,
                      seq, n_cmp, n_sel):
    qi = pl.program_id(2)
    qpos = qi * Q_TILE + lax.broadcasted_iota(I32, (1, Q_TILE), 1)
    row = lax.broadcasted_iota(I32, (LANES, 1), 0)

    @pl.when(qi == 0)
    def _():
        vct_ref[...] = vc_ref[...].astype(F32).T.astype(BF16)
        vst_ref[...] = vs_ref[...].astype(F32).T.astype(BF16)
        vwt_ref[...] = vw_ref[...].astype(F32).T.astype(BF16)

    cmask = (row * CMP_STRIDE + (CMP_LEN - 1) <= qpos) & (row < n_cmp)
    kc = kc_ref[...]
    psum = jnp.zeros((LANES, Q_TILE), F32)
    for p in range(Q_PER_KV):
        s = _dot_nt(kc, qc_ref[:, p * HEAD_DIM:(p + 1) * HEAD_DIM]) * ATTN_SCALE
        sm = jnp.where(cmask, s, NEG)
        e = jnp.exp(sm - jnp.max(sm, 0, keepdims=True))
        pr = jnp.where(cmask, e / jnp.sum(e, 0, keepdims=True), 0.0)
        psum = psum + pr
        ocmp_ref[p] = _dot(vct_ref[...], pr.astype(BF16))

    n_sel_pad = -(-n_sel // 8) * 8
    imp = _dot3(ovt_ref[...], psum)[:n_sel_pad]
    blk = row[:n_sel_pad]
    qblk = qpos >> 6
    valid = (blk * SEL_BLOCK <= qpos) & (blk < n_sel)
    forced = (blk == 0) | (blk == qblk) | (blk == qblk - 1)
    score = jnp.where(forced & valid, FORCE_SCORE, jnp.where(valid, imp, -1.0))
    rank = jnp.zeros(score.shape, F32)
    for j in range(n_sel):
        cj = score[j:j + 1, :]
        beats = (cj > score) | ((cj == score) & (blk > j))
        rank = rank + jnp.where(beats, 1.0, 0.0)
    sel = jnp.where((rank < SEL_TOP) & (score >= 0.0), 1.0, 0.0)
    sel = jnp.concatenate([sel, jnp.zeros((LANES - n_sel_pad, Q_TILE), F32)], 0).astype(BF16)
    hi = (qi * Q_TILE + Q_TILE - 1) // KEY_TILE + 1
    selx = _dot(ext_ref[...], sel)
    kpos_all = lax.broadcasted_iota(I32, (seq, 1), 0)
    allow_ref[...] = jnp.where((selx > 0.5) & (kpos_all <= qpos), 1.0, 0.0)

    q4_ref[...] = jnp.concatenate([qr_ref[:, p * HEAD_DIM:(p + 1) * HEAD_DIM] for p in range(Q_PER_KV)], 0)

    def flash(k_ref, vt_ref, lo, hi, mask_fn):
        acc_ref[...] = jnp.zeros(acc_ref.shape, F32)
        last = seq // KEY_TILE - 1

        def scores(kt):
            start = pl.multiple_of(kt * KEY_TILE, KEY_TILE)
            return _dot_nt(k_ref[pl.ds(start, KEY_TILE), :], q4_ref[...])

        s_ref[lo % 2] = scores(lo)

        def body(kt, carry):
            m, l = carry
            start = pl.multiple_of(kt * KEY_TILE, KEY_TILE)
            s = s_ref[kt % 2] * ATTN_SCALE
            s_ref[(kt + 1) % 2] = scores(jnp.minimum(kt + 1, last))
            mask = mask_fn(start)
            s = jnp.where(jnp.concatenate([mask] * Q_PER_KV, 1), s, NEG)
            m_new = jnp.maximum(m, jnp.max(s, 0, keepdims=True))
            alpha = jnp.exp(m - m_new)
            e = jnp.exp(s - m_new)
            l = alpha * l + jnp.sum(e, 0, keepdims=True)
            acc_ref[...] = alpha * acc_ref[...] + _dot(vt_ref[:, pl.ds(start, KEY_TILE)], e.astype(BF16))
            return m_new, l

        init = (jnp.full((1, Q_PER_KV * Q_TILE), NEG, F32), jnp.zeros((1, Q_PER_KV * Q_TILE), F32))
        return lax.fori_loop(lo, hi, body, init)[1]

    l = flash(ks_ref, vst_ref, 0, hi, lambda start: allow_ref[pl.ds(start, KEY_TILE), :] > 0.5)
    osel_ref[...] = acc_ref[...] / l

    def win_mask(start):
        kp = start + lax.broadcasted_iota(I32, (KEY_TILE, 1), 0)
        return (kp <= qpos) & (kp > qpos - WINDOW)

    l = flash(kw_ref, vwt_ref, jnp.maximum(qi * Q_TILE - (WINDOW - 1), 0) // KEY_TILE, hi, win_mask)
    owin = acc_ref[...] / l

    gt = g_ref[...].T
    for p in range(Q_PER_KV):
        sl = slice(p * Q_TILE, (p + 1) * Q_TILE)
        ot = (gt[p:p + 1] * ocmp_ref[p] + gt[Q_PER_KV + p:Q_PER_KV + p + 1] * osel_ref[:, sl]
              + gt[2 * Q_PER_KV + p:2 * Q_PER_KV + p + 1] * owin[:, sl])
        o_ref[:, p * HEAD_DIM:(p + 1) * HEAD_DIM] = ot.T.astype(o_ref.dtype)


def _attn_prompt(qc, qr, kc, vc, rb, gates, *, batch, seq):
    nq = seq // Q_TILE
    n_chunk = seq // CMP_STRIDE
    n_cmp = n_chunk - CMP_R + 1
    n_sel = seq // SEL_BLOCK
    assert n_chunk == LANES and n_sel <= LANES and seq % KEY_TILE == 0
    ovt = _overlap_matrix(n_cmp, n_sel, LANES, LANES).T
    ext = np.zeros((seq, LANES), np.float32)
    ext[np.arange(seq), np.arange(seq) // SEL_BLOCK] = 1.0
    ext = jnp.asarray(ext, BF16)
    qspec = pl.BlockSpec((Q_TILE, KV_SEC), lambda b, g, q: (b * nq + q, g))
    cspec = pl.BlockSpec((None, None, n_chunk, HEAD_DIM), lambda b, g, q: (b, g, 0, 0))
    kvspec = lambda sec: pl.BlockSpec((seq, HEAD_DIM), lambda b, g, q: (b, sec * KV_HEADS + g))
    body = functools.partial(_attn_prompt_body, seq=seq, n_cmp=n_cmp, n_sel=n_sel)
    return pl.pallas_call(
        body, grid=(batch, KV_HEADS, nq),
        in_specs=[qspec, qspec, cspec, cspec, kvspec(2), kvspec(3), kvspec(4), kvspec(5),
                  pl.BlockSpec((Q_TILE, LANES), lambda b, g, q: (b * nq + q, g)),
                  pl.BlockSpec((LANES, LANES), lambda b, g, q: (0, 0)),
                  pl.BlockSpec((seq, LANES), lambda b, g, q: (0, 0))],
        out_specs=qspec,
        out_shape=jax.ShapeDtypeStruct((batch * seq, N_HEADS * HEAD_DIM), BF16),
        scratch_shapes=[pltpu.VMEM((seq, Q_TILE), F32), pltpu.VMEM((HEAD_DIM, n_chunk), BF16),
                        pltpu.VMEM((HEAD_DIM, seq), BF16), pltpu.VMEM((HEAD_DIM, seq), BF16),
                        pltpu.VMEM((Q_PER_KV * Q_TILE, HEAD_DIM), BF16),
                        pltpu.VMEM((2, KEY_TILE, Q_PER_KV * Q_TILE), F32),
                        pltpu.VMEM((Q_PER_KV, HEAD_DIM, Q_TILE), F32), pltpu.VMEM((HEAD_DIM, Q_PER_KV * Q_TILE), F32),
                        pltpu.VMEM((HEAD_DIM, Q_PER_KV * Q_TILE), F32)],
        compiler_params=_cparams("arbitrary", "arbitrary", "arbitrary"), name="attn_prompt",
    )(qc, qr, kc, vc, rb, rb, rb, rb, gates, ovt, ext)


def _attn_s1_body(q_ref, kc_ref, vc_ref, ov_ref, o_ref, idx_ref, val_ref, sc_ref, *, qpos, n_cmp, n_sel):
    step = pl.program_id(0) * pl.num_programs(1) + pl.program_id(1)
    n_steps = pl.num_programs(0) * pl.num_programs(1)
    rows = q_ref.shape[0]
    n = kc_ref.shape[0]
    col = lax.broadcasted_iota(I32, (1, n), 1)
    cmask = (col * CMP_STRIDE + (CMP_LEN - 1) <= qpos) & (col < n_cmp)
    s = _dot_nt(q_ref[...], kc_ref[...]) * ATTN_SCALE
    sm = jnp.where(cmask, s, NEG)
    e = jnp.exp(sm - jnp.max(sm, -1, keepdims=True))
    pr = jnp.where(cmask, e / jnp.sum(e, -1, keepdims=True), 0.0)
    o_ref[...] = _dot(pr.astype(BF16), vc_ref[...])
    head = lax.broadcasted_iota(I32, (rows, 1), 0)
    psum = jnp.sum(jnp.where(head < Q_PER_KV, pr, 0.0), 0, keepdims=True)
    imp = _dot3(jnp.broadcast_to(psum, (rows, n)), ov_ref[...])[0:1]
    w = ov_ref.shape[1]
    blk = lax.broadcasted_iota(I32, (1, w), 1)
    qblk = qpos // SEL_BLOCK
    valid = (blk * SEL_BLOCK <= qpos) & (blk < n_sel)
    forced = (blk == 0) | (blk == qblk) | (blk == qblk - 1)
    low = -3e38
    sc_ref[pl.ds(step, 1), :] = jnp.where(
        blk < n_sel, jnp.where(forced & valid, FORCE_SCORE, jnp.where(valid, imp, -1.0)), low)

    @pl.when(step == n_steps - 1)
    def _():
        score = sc_ref[...]
        blkf = blk.astype(F32)
        lane = lax.broadcasted_iota(I32, (1, LANES), 1)
        idx_rows = jnp.zeros(idx_ref.shape, F32)
        val_rows = jnp.full(val_ref.shape, -1.0, F32)
        for t in range(SEL_TOP):
            m = jnp.max(score, -1, keepdims=True)
            it = jnp.min(jnp.where(score == m, blkf, float(w)), -1, keepdims=True)
            idx_rows = jnp.where(lane == t, it, idx_rows)
            val_rows = jnp.where(lane == t, m, val_rows)
            score = jnp.where(blkf == it, low, score)
        idx_ref[...] = idx_rows.astype(I32)
        val_ref[...] = val_rows


def _attn_s1(qc, kc, vc, *, qpos):
    nb, _, rows, _ = qc.shape
    n_chunk = kc.shape[2]
    t_all = qpos + 1
    n_cmp = t_all // CMP_STRIDE - CMP_R + 1
    n_sel = -(-t_all // SEL_BLOCK)
    assert n_cmp <= n_chunk and n_sel >= SEL_TOP
    w = -(-n_sel // LANES) * LANES
    ov = _overlap_matrix(n_cmp, n_sel, n_chunk, w)
    body = functools.partial(_attn_s1_body, qpos=qpos, n_cmp=n_cmp, n_sel=n_sel)
    hspec = pl.BlockSpec((None, None, rows, HEAD_DIM), lambda b, g: (b, g, 0, 0))
    cspec = pl.BlockSpec((None, None, n_chunk, HEAD_DIM), lambda b, g: (b, g, 0, 0))
    n_rows = nb * KV_HEADS
    assert n_rows % 8 == 0
    pick_spec = pl.BlockSpec((n_rows, LANES), lambda b, g: (0, 0))
    return pl.pallas_call(
        body, grid=(nb, KV_HEADS),
        in_specs=[hspec, cspec, cspec, pl.BlockSpec((n_chunk, w), lambda b, g: (0, 0))],
        out_specs=[hspec, pick_spec, pick_spec],
        out_shape=[jax.ShapeDtypeStruct((nb, KV_HEADS, rows, HEAD_DIM), F32),
                   jax.ShapeDtypeStruct((n_rows, LANES), I32),
                   jax.ShapeDtypeStruct((n_rows, LANES), F32)],
        scratch_shapes=[pltpu.VMEM((n_rows, w), F32)],
        compiler_params=_cparams("arbitrary", "arbitrary"), name="attn_sample_cmp",
    )(qc, kc, vc, ov)


def _attn_s2_body(phys_ref, idx_ref, ok_ref, new_ref, q_ref, *refs, qpos, n_past_blk, past_len):
    kb_refs = refs[:SEL_TOP]
    vb_refs = refs[SEL_TOP:2 * SEL_TOP]
    kn_ref, vn_ref, wk_ref, wv_ref, kwn_ref, vwn_ref, oc_ref, g_ref, o_ref = refs[2 * SEL_TOP:]
    b, g = pl.program_id(0), pl.program_id(1)
    q = q_ref[...]

    def attend(s, v, sn, vn):
        m = jnp.maximum(jnp.max(s, -1, keepdims=True), jnp.max(sn, -1, keepdims=True))
        e, en = jnp.exp(s - m), jnp.exp(sn - m)
        den = jnp.sum(e, -1, keepdims=True) + jnp.sum(en, -1, keepdims=True)
        return (_dot(e.astype(BF16), v) + _dot(en.astype(BF16), vn)) / den

    def head_rows(ref, n):
        return ref[pl.ds(g, n, stride=KV_HEADS), :].astype(BF16)

    wk = wk_ref.shape[0] // KV_HEADS
    kp = (past_len - wk) + lax.broadcasted_iota(I32, (1, wk), 1)
    s = jnp.where((kp <= qpos) & (kp > qpos - WINDOW), _dot_nt(q, head_rows(wk_ref, wk)) * ATTN_SCALE, NEG)
    sn = _dot_nt(q, kwn_ref[...].astype(BF16)) * ATTN_SCALE
    coln = lax.broadcasted_iota(I32, (1, sn.shape[1]), 1)
    o_win = attend(s, head_rows(wv_ref, wk), jnp.where(coln == 0, sn, NEG), vwn_ref[...].astype(BF16))

    kcat = jnp.concatenate([head_rows(r, SEL_BLOCK) for r in kb_refs], 0)
    vcat = jnp.concatenate([head_rows(r, SEL_BLOCK) for r in vb_refs], 0)
    lane = lax.broadcasted_iota(I32, (1, SEL_TOP * SEL_BLOCK), 1)
    limit = jnp.full(lane.shape, -1, I32)
    for kk in range(SEL_TOP):
        t = (b * KV_HEADS + g) * SEL_TOP + kk
        lim = jnp.where((ok_ref[t] == 1) & (idx_ref[t] < n_past_blk), qpos - idx_ref[t] * SEL_BLOCK, -1)
        limit = jnp.where((lane >> 6) == kk, lim, limit)
    s = jnp.where((lane & (SEL_BLOCK - 1)) <= limit, _dot_nt(q, kcat) * ATTN_SCALE, NEG)
    sn = _dot_nt(q, kn_ref[...].astype(BF16)) * ATTN_SCALE
    sn = jnp.where(coln < new_ref[b * KV_HEADS + g], sn, NEG)
    o_sel = attend(s, vcat, sn, vn_ref[...].astype(BF16))

    gt = g_ref[...]
    o_ref[...] = gt[:, 0:1] * oc_ref[...] + gt[:, 1:2] * o_sel + gt[:, 2:3] * o_win


def _attn_s2(qr, pool_k, pool_v, kn, vn, win_k, win_v, kwn, vwn, o_cmp, gates, phys, idx, ok, has_new, *,
             qpos, past_len):
    nb, _, rows, _ = qr.shape
    wk = win_k.shape[0] // (nb * KV_HEADS)
    assert SEL_BLOCK == 64
    hspec = pl.BlockSpec((None, None, rows, HEAD_DIM), lambda b, g, *_: (b, g, 0, 0))
    bspec = lambda k: pl.BlockSpec((SEL_BLOCK * KV_HEADS, HEAD_DIM),
                                   lambda b, g, ph, *_: (ph[(b * KV_HEADS + g) * SEL_TOP + k], 0))
    wspec = pl.BlockSpec((wk * KV_HEADS, HEAD_DIM), lambda b, g, *_: (b, 0))
    blocks = [bspec(k) for k in range(SEL_TOP)]
    body = functools.partial(_attn_s2_body, qpos=qpos, n_past_blk=past_len // SEL_BLOCK, past_len=past_len)
    return pl.pallas_call(
        body,
        grid_spec=pltpu.PrefetchScalarGridSpec(
            num_scalar_prefetch=4, grid=(nb, KV_HEADS),
            in_specs=[hspec] + blocks + blocks + [hspec, hspec, wspec, wspec, hspec, hspec, hspec, hspec],
            out_specs=hspec),
        out_shape=jax.ShapeDtypeStruct((nb, KV_HEADS, rows, HEAD_DIM), F32),
        compiler_params=_cparams("arbitrary", "arbitrary"), name="attn_sample_sel_win",
    )(phys, idx, ok, has_new, qr, *([pool_k] * SEL_TOP), *([pool_v] * SEL_TOP), kn, vn, win_k, win_v, kwn, vwn,
      o_cmp, gates)


def _head_rows(a, rows=8):
    nb = a.shape[0]
    p = a.shape[1] // (KV_HEADS * HEAD_DIM)
    a = a.reshape(nb, KV_HEADS, p, HEAD_DIM)
    return jnp.pad(a, ((0, 0), (0, 0), (0, rows - p), (0, 0)))


def kernel(x_prompt, x_sample, cache_cmp_k, cache_cmp_v, cache_sel_k, cache_sel_v, state_win_k, state_win_v, page_table, norm_mix, norm_ffn, norm_kv, norm_final, gm_w_in, gm_ln_g, gm_ln_b, gm_w_s, gm_b_s, gm_w_out, nsa_w_in, nsa_w_out, kv_w, cmp_pe_k, cmp_w1_k, cmp_w2_k, cmp_pe_v, cmp_w1_v, cmp_w2_v, ff_w_gate, ff_w_up, ff_w_down, moe_router, moe_w_gate, moe_w_up, moe_w_down):
    B, S, D = x_prompt.shape
    DB, DS, _ = x_sample.shape
    assert DS == 1 and D == D_MODEL and S % KEY_TILE == 0
    n_prompt, n_sample = B * S, DB * DS
    n_tok = n_prompt + n_sample
    M = n_prompt + GM_CHUNK
    assert M % ROW_TILE == 0 and M % ROW_TILE_DOWN == 0 and n_sample <= 8
    past_len = page_table.shape[1] * PAGE_SIZE
    qpos_s = past_len
    wk = state_win_k.shape[1]
    assert wk == min(WINDOW, past_len)

    x = jnp.concatenate([x_prompt.reshape(n_prompt, D), x_sample.reshape(n_sample, D),
                         jnp.zeros((M - n_tok, D), F32)], 0)

    pos = jnp.concatenate([jnp.tile(jnp.arange(S), B), jnp.full((n_sample,), past_len), jnp.zeros((M - n_tok,), I32)])
    half = ROT_DIM // 2
    inv = ROPE_THETA ** (-jnp.arange(half, dtype=F32) / half)
    ang = pos.astype(F32)[:, None] * inv[None, :]
    cos, sin = jnp.cos(ang), jnp.sin(ang)
    zeros_h = jnp.zeros((M, half), F32)
    rest = HEAD_DIM - ROT_DIM
    rope_c = jnp.concatenate([cos, cos, jnp.ones((M, rest), F32)], 1)
    rope_s1 = jnp.concatenate([-sin, zeros_h, jnp.zeros((M, rest), F32)], 1)
    rope_s2 = jnp.concatenate([zeros_h, sin, jnp.zeros((M, rest), F32)], 1)
    rope_extras = [(t, (ROW_TILE, HEAD_DIM), lambda j, i: (i, 0)) for t in (rope_c, rope_s1, rope_s2)]

    def residual_mm(xin, w, layer, res, tm, name):
        n = w.shape[2]
        tn = COL_TILE_WIDE if xin.shape[1] == D_MODEL else COL_TILE
        return _ws_matmul(xin, [(w, layer)], _ep_residual, [(n, F32)], tm=tm, tn=tn, n_cols=n,
                          extras=[(res, (tm, tn), lambda j, i: (i, j))], name=name)[0]

    gm_v = []
    kv = None
    for l in range(DEPTH):
        h = _rmsnorm(x, norm_mix[l], BF16, tm=ROW_TILE)
        if l < N_A_LAYERS:
            z = _ws_matmul(h, [(gm_w_in, l)], _ep_gelu, [(2 * GM_DIM, BF16)], tm=ROW_TILE, tn=COL_TILE_WIDE,
                           n_cols=2 * GM_DIM, name="gmlp_in")[0]
            o, v_s = _gm_gate(z, gm_ln_g[l], gm_ln_b[l], gm_w_s[l], gm_b_s[l], n_prompt=n_prompt, n_sample=n_sample)
            gm_v.append(v_s[:n_sample].reshape(DB, DS, GM_DIM))
            x = residual_mm(o, gm_w_out, l, x, ROW_TILE, "gmlp_out")
        else:
            bl = l - N_A_LAYERS
            qc, qr = _ws_matmul(h, [(nsa_w_in, bl)], _ep_q, [(N_HEADS * HEAD_DIM, BF16)] * 2, tm=ROW_TILE,
                                tn=COL_TILE_WIDE, n_cols=N_HEADS * HEAD_DIM, extras=rope_extras, name="nsa_q")
            wg = nsa_w_in[bl, :, N_HEADS * HEAD_DIM:].reshape(D, N_BRANCH, KV_HEADS, Q_PER_KV)
            wg = jnp.transpose(wg, (0, 2, 1, 3)).reshape(D, KV_HEADS, N_BRANCH * Q_PER_KV)
            wg = jnp.pad(wg, ((0, 0), (0, 0), (0, LANES - N_BRANCH * Q_PER_KV))).reshape(1, D, KV_HEADS * LANES)
            gates = _ws_matmul(h, [(wg, 0)], _ep_sigmoid, [(KV_HEADS * LANES, F32)], tm=ROW_TILE, tn=KV_HEADS * LANES,
                               n_cols=KV_HEADS * LANES, name="nsa_gates")[0]
            o = _attn_prompt(qc, qr, kv['kc_p'], kv['vc_p'], kv['rb'], gates, batch=B, seq=S)
            qc_s = _head_rows(qc[n_prompt:n_tok])
            qr_s = _head_rows(qr[n_prompt:n_tok])
            oc_s, idx_s, val_s = _attn_s1(qc_s, kv['kc_s'], kv['vc_s'], qpos=qpos_s)
            idx = idx_s[:, :SEL_TOP].reshape(DB, KV_HEADS, SEL_TOP)
            ok = val_s[:, :SEL_TOP].reshape(DB, KV_HEADS, SEL_TOP) >= 0.0
            n_past_blk = past_len // SEL_BLOCK
            ppb = PAGE_SIZE // SEL_BLOCK
            ip = jnp.minimum(idx, n_past_blk - 1)
            phys = jnp.take_along_axis(page_table, (ip // ppb).reshape(DB, -1), 1).reshape(ip.shape) * ppb + ip % ppb
            has_new = jnp.any((idx >= n_past_blk) & ok, -1)
            g_s = gates[n_prompt:n_tok].reshape(DB, KV_HEADS, LANES)[:, :, :N_BRANCH * Q_PER_KV]
            g_s = jnp.transpose(g_s.reshape(DB, KV_HEADS, N_BRANCH, Q_PER_KV), (0, 1, 3, 2))
            g_s = jnp.pad(g_s, ((0, 0), (0, 0), (0, 8 - Q_PER_KV), (0, HEAD_DIM - N_BRANCH)))
            o_s = _attn_s2(qr_s, kv['pool_k'], kv['pool_v'], kv['ks_new'], kv['vs_new'], kv['win_k'], kv['win_v'],
                           kv['kw_new'], kv['vw_new'], oc_s, g_s, phys.reshape(-1).astype(I32),
                           idx.reshape(-1).astype(I32), ok.reshape(-1).astype(I32), has_new.reshape(-1).astype(I32),
                           qpos=qpos_s, past_len=past_len)
            o_s = o_s[:, :, :Q_PER_KV].reshape(n_sample, N_HEADS * HEAD_DIM).astype(BF16)
            o = jnp.concatenate([o, o_s, jnp.zeros((M - n_tok, N_HEADS * HEAD_DIM), BF16)], 0)
            x = residual_mm(o, nsa_w_out, bl, x, ROW_TILE, "nsa_out")

        if l % 2 == 0:
            h = _rmsnorm(x, norm_ffn[l], BF16, tm=ROW_TILE)
            hh = _ws_matmul(h, [(ff_w_gate, l // 2), (ff_w_up, l // 2)], _ep_swiglu, [(ff_w_gate.shape[2], BF16)],
                            tm=ROW_TILE, tn=COL_TILE, n_cols=ff_w_gate.shape[2], name="ffn_up")[0]
            x = residual_mm(hh, ff_w_down, l // 2, x, ROW_TILE_DOWN, "ffn_down")
        else:
            x = _moe(x, norm_ffn[l], moe_router[l // 2], moe_w_gate, moe_w_up, moe_w_down, l // 2, n_tok=n_tok)

        if l == N_A_LAYERS - 1:
            hkv = _rmsnorm(x, norm_kv, BF16, tm=ROW_TILE)
            r, rb = _ws_matmul(hkv, [(kv_w.reshape(1, D, 6 * KV_SEC), 0)], _ep_kv,
                               [(6 * KV_SEC, F32), (6 * KV_SEC, BF16)], tm=ROW_TILE, tn=COL_TILE_WIDE,
                               n_cols=6 * KV_SEC, extras=rope_extras, name="kv_proj")
            pages_p = r.reshape(M // PAGE_SIZE, PAGE_SIZE, 6 * KV_SEC)
            table_p = jnp.arange(n_prompt // PAGE_SIZE, dtype=I32).reshape(B, S // PAGE_SIZE)
            r_s = r[n_prompt:n_tok]
            sec_s = lambda i: r_s[:, i * KV_SEC:(i + 1) * KV_SEC]
            rows2d = lambda c: c.reshape(-1, HEAD_DIM)
            kv = dict(
                rb=rb,
                kc_p=_compress(pages_p, table_p, 0, cmp_pe_k, cmp_w1_k, cmp_w2_k),
                vc_p=_compress(pages_p, table_p, 1, cmp_pe_v, cmp_w1_v, cmp_w2_v),
                kc_s=_compress_cache(cache_cmp_k, page_table, cmp_pe_k, cmp_w1_k, cmp_w2_k),
                vc_s=_compress_cache(cache_cmp_v, page_table, cmp_pe_v, cmp_w1_v, cmp_w2_v),
                pool_k=rows2d(cache_sel_k), pool_v=rows2d(cache_sel_v),
                ks_new=_head_rows(sec_s(2)), vs_new=_head_rows(sec_s(3)),
                kw_new=_head_rows(sec_s(4)), vw_new=_head_rows(sec_s(5)),
                win_k=rows2d(state_win_k), win_v=rows2d(state_win_v),
            )
            r_p = r[:n_prompt]
            pg = lambda i: r_p[:, i * KV_SEC:(i + 1) * KV_SEC].reshape(B, S // PAGE_SIZE, PAGE_SIZE, KV_HEADS, HEAD_DIM)
            wkp = min(WINDOW, S)
            tail = lambda i: r_p[:, i * KV_SEC:(i + 1) * KV_SEC].reshape(B, S, KV_HEADS, HEAD_DIM)[:, S - wkp:]
            row_s = lambda i: sec_s(i).reshape(DB, DS, KV_HEADS, HEAD_DIM)
            state_p = (pg(0), pg(1), pg(2), pg(3), tail(4), tail(5))
            state_s = (row_s(0), row_s(1), row_s(2), row_s(3),
                       jnp.concatenate([state_win_k, row_s(4)], 1)[:, DS:],
                       jnp.concatenate([state_win_v, row_s(5)], 1)[:, DS:])

    y_prompt = _rmsnorm(x, norm_final, F32, tm=S // 2, n_rows=n_prompt).reshape(B, S, D)
    y_sample = _rmsnorm(x, norm_final, F32, tm=8, first_row=n_prompt, n_rows=8)[:n_sample].reshape(DB, DS, D)
    return (y_prompt, y_sample) + state_p + state_s + (jnp.stack(gm_v, 0),)
```

```python
import functools

import numpy as np
import jax
import jax.numpy as jnp
from jax import lax
from jax.experimental import pallas as pl
from jax.experimental.pallas import tpu as pltpu

F32 = jnp.float32
BF16 = jnp.bfloat16
I32 = jnp.int32

D_MODEL = 2048
DEPTH = 4
PAGE_SIZE = 128
N_A_LAYERS = DEPTH // 2
GM_CHUNK = 128
GM_DIM = D_MODEL
GM_GROUPS = 16
GM_GW = GM_DIM // GM_GROUPS
N_HEADS = 16
HEAD_DIM = 128
KV_HEADS = 4
Q_PER_KV = N_HEADS // KV_HEADS
ROT_DIM = HEAD_DIM // 4
ROPE_THETA = 500000.0
CMP_LEN = 32
CMP_STRIDE = 16
CMP_R = CMP_LEN // CMP_STRIDE
SEL_BLOCK = 64
SEL_TOP = 16
WINDOW = 512
N_BRANCH = 3
KV_SEC = KV_HEADS * HEAD_DIM
ATTN_SCALE = HEAD_DIM ** -0.5
N_EXPERTS = 8
TOP_K = 2
EPS = 1e-6
NEG = -1e30
FORCE_SCORE = 1e6

LANES = 128
VMEM_LIMIT_BYTES = 56 * 1024 * 1024
ROW_TILE = 832
ROW_TILE_DOWN = 416
COL_TILE = 512
COL_TILE_WIDE = 1024
MOE_BLOCK = 256
MOE_SEG_BLOCKS = 10
MOE_FF_TILE = 256
KEY_TILE = 256
Q_TILE = 256
PAGES_PER_STEP = 16


def _cparams(*sem):
    return pltpu.CompilerParams(dimension_semantics=sem, vmem_limit_bytes=VMEM_LIMIT_BYTES)


def _dot(a, b):
    return jnp.dot(a, b, preferred_element_type=F32)


def _dot_nt(a, b):
    return lax.dot_general(a, b, (((1,), (1,)), ((), ())), preferred_element_type=F32)


def _dot3(a, b):
    ah = a.astype(BF16)
    al = (a - ah.astype(F32)).astype(BF16)
    bh = b.astype(BF16)
    bl = (b - bh.astype(F32)).astype(BF16)
    return _dot(ah, bh) + _dot(al, bh) + _dot(ah, bl)


def _ws_body(*refs, nw, n_extra, n_out, epilogue):
    x_ref = refs[0]
    w_refs = refs[1:1 + nw]
    e_refs = refs[1 + nw:1 + nw + n_extra]
    o_refs = refs[1 + nw + n_extra:1 + nw + n_extra + n_out]
    wb_refs = refs[1 + nw + n_extra + n_out:]
    j = pl.program_id(0)

    @pl.when(pl.program_id(1) == 0)
    def _():
        for w_ref, wb in zip(w_refs, wb_refs):
            wb[...] = w_ref[...].astype(BF16)

    accs = [_dot(x_ref[...], wb[...]) for wb in wb_refs]
    outs = epilogue(j, accs, e_refs)
    for o_ref, o in zip(o_refs, outs):
        o_ref[...] = o.astype(o_ref.dtype)


def _ws_matmul(x, ws, epilogue, outs, *, tm, tn, n_cols, extras=(), name):
    M, K = x.shape
    nj = n_cols // tn
    ni = M // tm
    assert nj * tn == n_cols and ni * tm == M
    in_specs = [pl.BlockSpec((tm, K), lambda j, i: (i, 0))]
    in_specs += [pl.BlockSpec((None, K, tn), lambda j, i, layer=layer: (layer, 0, j)) for _, layer in ws]
    ws = [w for w, _ in ws]
    in_specs += [pl.BlockSpec(bs, im) for _, bs, im in extras]
    out_shape = [jax.ShapeDtypeStruct((M, c), dt) for c, dt in outs]
    out_specs = [pl.BlockSpec((tm, c // nj), lambda j, i: (i, j)) for c, _ in outs]
    body = functools.partial(_ws_body, nw=len(ws), n_extra=len(extras), n_out=len(outs), epilogue=epilogue)
    res = pl.pallas_call(
        body, grid=(nj, ni), in_specs=in_specs, out_specs=out_specs, out_shape=out_shape,
        scratch_shapes=[pltpu.VMEM((K, tn), BF16) for _ in ws],
        compiler_params=_cparams("arbitrary", "arbitrary"), name=name,
    )(x, *ws, *[a for a, _, _ in extras])
    return res


def _rope_tile(acc, c_ref, s1_ref, s2_ref):
    c, s1, s2 = c_ref[...], s1_ref[...], s2_ref[...]
    parts = []
    for h in range(acc.shape[1] // HEAD_DIM):
        xh = acc[:, h * HEAD_DIM:(h + 1) * HEAD_DIM]
        parts.append(xh * c + pltpu.roll(xh, HEAD_DIM - ROT_DIM // 2, 1) * s1 + pltpu.roll(xh, ROT_DIM // 2, 1) * s2)
    return jnp.concatenate(parts, 1)


def _ep_gelu(j, accs, e):
    return [jax.nn.gelu(accs[0])]


def _ep_residual(j, accs, e):
    return [e[0][...] + accs[0]]


def _ep_swiglu(j, accs, e):
    return [jax.nn.silu(accs[0]) * accs[1]]


def _ep_sigmoid(j, accs, e):
    return [jax.nn.sigmoid(accs[0])]


def _ep_kv(j, accs, e):
    acc = accs[0]
    per_tile = acc.shape[1] // KV_SEC
    parts = []
    for t in range(per_tile):
        a = acc[:, t * KV_SEC:(t + 1) * KV_SEC]
        sec = j * per_tile + t
        parts.append(jnp.where((sec == 2) | (sec == 4), _rope_tile(a, *e), a))
    r = parts[0] if per_tile == 1 else jnp.concatenate(parts, 1)
    return [r, r]


def _ep_q(j, accs, e):
    acc = accs[0]
    return [acc, _rope_tile(acc, *e)]


def _rms(x, g):
    return (x * lax.rsqrt(jnp.mean(x * x, -1, keepdims=True) + EPS)) * g


def _rms_body(x_ref, g_ref, o_ref):
    o_ref[...] = _rms(x_ref[...], g_ref[...]).astype(o_ref.dtype)


def _rmsnorm(x, g, out_dtype, *, tm, first_row=0, n_rows=None):
    M, D = x.shape
    n_rows = M if n_rows is None else n_rows
    assert n_rows % tm == 0 and first_row % tm == 0
    off = first_row // tm
    return pl.pallas_call(
        _rms_body, grid=(n_rows // tm,),
        in_specs=[pl.BlockSpec((tm, D), lambda i: (i + off, 0)), pl.BlockSpec((1, D), lambda i: (0, 0))],
        out_specs=pl.BlockSpec((tm, D), lambda i: (i, 0)),
        out_shape=jax.ShapeDtypeStruct((n_rows, D), out_dtype),
        compiler_params=_cparams("arbitrary"), name="rmsnorm",
    )(x, g.reshape(1, D))


def _rms_router_body(x_ref, g_ref, r_ref, h_ref, te_ref, tg_ref):
    h = _rms(x_ref[...], g_ref[...])
    h_ref[...] = h
    lane = lax.broadcasted_iota(I32, (1, LANES), 1).astype(F32)
    logits = jnp.where(lane < N_EXPERTS, _dot3(h, r_ref[...]), -jnp.inf)
    m1 = jnp.max(logits, -1, keepdims=True)
    i1 = jnp.min(jnp.where(logits == m1, lane, float(LANES)), -1, keepdims=True)
    rest = jnp.where(lane == i1, -jnp.inf, logits)
    m2 = jnp.max(rest, -1, keepdims=True)
    i2 = jnp.min(jnp.where(rest == m2, lane, float(LANES)), -1, keepdims=True)
    e2 = jnp.exp(m2 - m1)
    den = 1.0 + e2
    te_ref[...] = jnp.where(lane == 0, i1, jnp.where(lane == 1, i2, 0.0)).astype(I32)
    tg_ref[...] = jnp.where(lane == 0, 1.0 / den, jnp.where(lane == 1, e2 / den, 0.0))


def _rmsnorm_router(x, g, router, *, tm):
    M, D = x.shape
    rp = jnp.pad(router, ((0, 0), (0, LANES - N_EXPERTS)))
    return pl.pallas_call(
        _rms_router_body, grid=(M // tm,),
        in_specs=[pl.BlockSpec((tm, D), lambda i: (i, 0)), pl.BlockSpec((1, D), lambda i: (0, 0)),
                  pl.BlockSpec((D, LANES), lambda i: (0, 0))],
        out_specs=[pl.BlockSpec((tm, D), lambda i: (i, 0)), pl.BlockSpec((tm, LANES), lambda i: (i, 0)),
                   pl.BlockSpec((tm, LANES), lambda i: (i, 0))],
        out_shape=[jax.ShapeDtypeStruct((M, D), F32), jax.ShapeDtypeStruct((M, LANES), I32),
                   jax.ShapeDtypeStruct((M, LANES), F32)],
        compiler_params=_cparams("arbitrary"), name="rmsnorm_router",
    )(x, g.reshape(1, D), rp)


def _gm_gate_body(z_ref, lng_ref, lnb_ref, ws_ref, bst_ref, w00_ref, b00_ref, o_ref, vs_ref, *,
                  n_prompt_chunks, n_sample):
    c = pl.program_id(0)
    z = z_ref[...]
    u = z[:, :GM_DIM].astype(F32)
    v = z[:, GM_DIM:].astype(F32)
    mu = jnp.mean(v, -1, keepdims=True)
    vc = v - mu
    vn = (vc * lax.rsqrt(jnp.mean(vc * vc, -1, keepdims=True) + EPS)) * lng_ref[...] + lnb_ref[...]

    @pl.when(c < n_prompt_chunks)
    def _():
        vb = vn.astype(BF16)
        tri = lax.broadcasted_iota(I32, (GM_CHUNK, GM_CHUNK), 0) >= lax.broadcasted_iota(I32, (GM_CHUNK, GM_CHUNK), 1)
        for g in range(GM_GROUPS):
            sl = slice(g * GM_GW, (g + 1) * GM_GW)
            wg = jnp.where(tri, ws_ref[g], 0.0).astype(BF16)
            s = _dot(wg, vb[:, sl]) + bst_ref[:, g:g + 1]
            o_ref[:, sl] = (u[:, sl] * s).astype(o_ref.dtype)

    @pl.when(c >= n_prompt_chunks)
    def _():
        row = lax.broadcasted_iota(I32, (GM_CHUNK, 1), 0)
        s = w00_ref[...] * vn + b00_ref[...]
        o_ref[...] = jnp.where(row < n_sample, u * s, 0.0).astype(o_ref.dtype)
        vs_ref[...] = vn[:vs_ref.shape[0]]


def _gm_gate(z, ln_g, ln_b, w_s, b_s, *, n_prompt, n_sample):
    M = z.shape[0]
    n_pc = n_prompt // GM_CHUNK
    assert n_prompt % GM_CHUNK == 0 and M == (n_pc + 1) * GM_CHUNK and n_sample <= GM_CHUNK
    n_vs = -(-n_sample // 8) * 8
    w00 = jnp.repeat(w_s[:, 0, 0], GM_GW).reshape(1, GM_DIM)
    b00 = jnp.repeat(b_s[:, 0], GM_GW).reshape(1, GM_DIM)
    body = functools.partial(_gm_gate_body, n_prompt_chunks=n_pc, n_sample=n_sample)
    full = lambda *shape: pl.BlockSpec(shape, lambda c: (0,) * len(shape))
    return pl.pallas_call(
        body, grid=(n_pc + 1,),
        in_specs=[pl.BlockSpec((GM_CHUNK, 2 * GM_DIM), lambda c: (c, 0)), full(1, GM_DIM), full(1, GM_DIM),
                  full(GM_GROUPS, GM_CHUNK, GM_CHUNK), full(GM_CHUNK, GM_GROUPS), full(1, GM_DIM), full(1, GM_DIM)],
        out_specs=[pl.BlockSpec((GM_CHUNK, GM_DIM), lambda c: (c, 0)), full(n_vs, GM_DIM)],
        out_shape=[jax.ShapeDtypeStruct((M, GM_DIM), BF16), jax.ShapeDtypeStruct((n_vs, GM_DIM), F32)],
        compiler_params=_cparams("arbitrary"), name="gmlp_gate",
    )(z, ln_g.reshape(1, GM_DIM), ln_b.reshape(1, GM_DIM), w_s, b_s.T, w00, b00)


def _moe_gather_body(rt_ref, nu_ref, h_hbm, o_ref, buf, sem):
    i = pl.program_id(0)
    n_used = nu_ref[0]

    def row_copy(tok, slot, r):
        return pltpu.make_async_copy(h_hbm.at[pl.ds(tok, 1), :], buf.at[slot, pl.ds(r, 1), :], sem.at[slot])

    def start_block(blk, slot):
        def issue(r, c):
            row_copy(rt_ref[blk * MOE_BLOCK + r], slot, r).start()
            return c

        lax.fori_loop(0, MOE_BLOCK, issue, 0, unroll=8)

    @pl.when((i == 0) & (n_used > 0))
    def _():
        start_block(0, 0)

    @pl.when(i + 1 < n_used)
    def _():
        start_block(i + 1, (i + 1) % 2)

    @pl.when(i < n_used)
    def _():
        slot = i % 2

        def wait(r, c):
            row_copy(0, slot, r).wait()
            return c

        lax.fori_loop(0, MOE_BLOCK, wait, 0, unroll=8)
        o_ref[...] = buf[slot].astype(o_ref.dtype)

    @pl.when(i >= n_used)
    def _():
        o_ref[...] = jnp.zeros(o_ref.shape, o_ref.dtype)


def _moe_gather(h, row_tok, n_used, n_blk):
    D = h.shape[1]
    return pl.pallas_call(
        _moe_gather_body,
        grid_spec=pltpu.PrefetchScalarGridSpec(
            num_scalar_prefetch=2, grid=(n_blk,),
            in_specs=[pl.BlockSpec(memory_space=pl.ANY)],
            out_specs=pl.BlockSpec((MOE_BLOCK, D), lambda i, rt, nu: (i, 0)),
            scratch_shapes=[pltpu.VMEM((2, MOE_BLOCK, D), F32), pltpu.SemaphoreType.DMA((2,))]),
        out_shape=jax.ShapeDtypeStruct((n_blk * MOE_BLOCK, D), BF16),
        compiler_params=_cparams("arbitrary"), name="moe_gather",
    )(row_tok, n_used, h)


def _moe_ffn_body(se_ref, sb_ref, sn_ref, ns_ref, nu_ref, xb_hbm, wg_ref, wu_ref, wd_ref, yb_hbm,
                  x_vmem, y_acc, wgb, wub, wdb, sem_in, sem_out, *, n_j, n_seg_max, n_blk):
    s, j = pl.program_id(0), pl.program_id(1)
    nblk = sn_ref[jnp.minimum(s, n_seg_max - 1)]
    row0 = sb_ref[jnp.minimum(s, n_seg_max - 1)] * MOE_BLOCK

    def block_copy(src, src_row, dst, dst_row, sem):
        return pltpu.make_async_copy(src.at[pl.ds(pl.multiple_of(src_row, MOE_BLOCK), MOE_BLOCK), :],
                                     dst.at[pl.ds(pl.multiple_of(dst_row, MOE_BLOCK), MOE_BLOCK), :], sem)

    def for_blocks(lo, hi, fn):
        def body(blk, c):
            fn(blk)
            return c

        lax.fori_loop(lo, hi, body, 0)

    @pl.when(s < ns_ref[0])
    def _():
        @pl.when(j == 0)
        def _():
            for_blocks(0, nblk, lambda blk: block_copy(xb_hbm, row0 + blk * MOE_BLOCK, x_vmem, blk * MOE_BLOCK,
                                                       sem_in).start())
            for_blocks(0, nblk, lambda blk: block_copy(xb_hbm, row0, x_vmem, blk * MOE_BLOCK, sem_in).wait())

        wgb[...] = wg_ref[...].astype(BF16)
        wub[...] = wu_ref[...].astype(BF16)
        wdb[...] = wd_ref[...].astype(BF16)

        def update(blk, n, first):
            rows = pl.ds(pl.multiple_of(blk * MOE_BLOCK, MOE_BLOCK), n * MOE_BLOCK)
            x = x_vmem[rows, :]
            hidden = (jax.nn.silu(_dot(x, wgb[...])) * _dot(x, wub[...])).astype(BF16)
            part = _dot(hidden, wdb[...])
            y_acc[rows, :] = part if first else y_acc[rows, :] + part

        def sweep(first):
            triples = nblk // 3
            for_blocks(0, triples, lambda c: update(3 * c, 3, first))

            @pl.when(nblk % 3 == 2)
            def _():
                update(3 * triples, 2, first)

            @pl.when(nblk % 3 == 1)
            def _():
                update(3 * triples, 1, first)

        @pl.when(j == 0)
        def _():
            sweep(True)

        @pl.when(j > 0)
        def _():
            sweep(False)

        @pl.when(j == n_j - 1)
        def _():
            for_blocks(0, nblk, lambda blk: block_copy(y_acc, blk * MOE_BLOCK, yb_hbm, row0 + blk * MOE_BLOCK,
                                                       sem_out).start())
            for_blocks(0, nblk, lambda blk: block_copy(y_acc, blk * MOE_BLOCK, yb_hbm, row0, sem_out).wait())

    @pl.when((s == n_seg_max) & (j == 0))
    def _():
        y_acc[pl.ds(0, MOE_BLOCK), :] = jnp.zeros((MOE_BLOCK, y_acc.shape[1]), F32)
        for_blocks(nu_ref[0], n_blk, lambda blk: block_copy(y_acc, 0, yb_hbm, blk * MOE_BLOCK, sem_out).start())
        for_blocks(nu_ref[0], n_blk, lambda blk: block_copy(y_acc, 0, yb_hbm, 0, sem_out).wait())


def _moe_ffn(xb, wg, wu, wd, layer, seg_e, seg_blk, seg_nblk, n_seg, n_used, *, n_seg_max):
    M, D = xb.shape
    F = wg.shape[3]
    n_j = F // MOE_FF_TILE
    assert n_j * MOE_FF_TILE == F
    n_blk = M // MOE_BLOCK

    def w_index(s, j, se, sb, sn, ns, nu):
        live = s < ns[0]
        return se[jnp.minimum(s, ns[0] - 1)], jnp.where(live, j, n_j - 1)

    def up_map(s, j, *pref):
        e, jj = w_index(s, j, *pref)
        return layer, e, 0, jj

    def down_map(s, j, *pref):
        e, jj = w_index(s, j, *pref)
        return layer, e, jj, 0

    seg_rows = MOE_SEG_BLOCKS * MOE_BLOCK
    body = functools.partial(_moe_ffn_body, n_j=n_j, n_seg_max=n_seg_max, n_blk=n_blk)
    return pl.pallas_call(
        body,
        grid_spec=pltpu.PrefetchScalarGridSpec(
            num_scalar_prefetch=5, grid=(n_seg_max + 1, n_j),
            in_specs=[pl.BlockSpec(memory_space=pl.ANY),
                      pl.BlockSpec((None, None, D, MOE_FF_TILE), up_map),
                      pl.BlockSpec((None, None, D, MOE_FF_TILE), up_map),
                      pl.BlockSpec((None, None, MOE_FF_TILE, D), down_map)],
            out_specs=pl.BlockSpec(memory_space=pl.ANY),
            scratch_shapes=[pltpu.VMEM((seg_rows, D), BF16), pltpu.VMEM((seg_rows, D), F32),
                            pltpu.VMEM((D, MOE_FF_TILE), BF16), pltpu.VMEM((D, MOE_FF_TILE), BF16),
                            pltpu.VMEM((MOE_FF_TILE, D), BF16),
                            pltpu.SemaphoreType.DMA(()), pltpu.SemaphoreType.DMA(())]),
        out_shape=jax.ShapeDtypeStruct((M, D), F32),
        compiler_params=_cparams("arbitrary", "arbitrary"), name="moe_ffn",
    )(seg_e, seg_blk, seg_nblk, n_seg, n_used, xb, wg, wu, wd)


def _moe_combine_body(d0_ref, d1_ref, x_ref, g_ref, yb_hbm, o_ref, b0, b1, sem, *, tm):
    i = pl.program_id(0)

    def row_copy(src_row, buf, slot, r):
        return pltpu.make_async_copy(yb_hbm.at[pl.ds(src_row, 1), :], buf.at[slot, pl.ds(r, 1), :], sem.at[slot])

    def start_tile(t, slot):
        def issue(r, c):
            row_copy(d0_ref[t * tm + r], b0, slot, r).start()
            row_copy(d1_ref[t * tm + r], b1, slot, r).start()
            return c

        lax.fori_loop(0, tm, issue, 0, unroll=8)

    @pl.when(i == 0)
    def _():
        start_tile(0, 0)

    @pl.when(i + 1 < pl.num_programs(0))
    def _():
        start_tile(i + 1, (i + 1) % 2)

    slot = i % 2

    def wait(r, c):
        row_copy(0, b0, slot, r).wait()
        row_copy(0, b1, slot, r).wait()
        return c

    lax.fori_loop(0, tm, wait, 0, unroll=8)
    g = g_ref[...]
    o_ref[...] = x_ref[...] + (g[:, 0:1] * b0[slot] + g[:, 1:2] * b1[slot])


def _moe_combine(x, gates, yb, d0, d1, *, tm):
    M, D = x.shape
    body = functools.partial(_moe_combine_body, tm=tm)
    return pl.pallas_call(
        body,
        grid_spec=pltpu.PrefetchScalarGridSpec(
            num_scalar_prefetch=2, grid=(M // tm,),
            in_specs=[pl.BlockSpec((tm, D), lambda i, a, b: (i, 0)), pl.BlockSpec((tm, LANES), lambda i, a, b: (i, 0)),
                      pl.BlockSpec(memory_space=pl.ANY)],
            out_specs=pl.BlockSpec((tm, D), lambda i, a, b: (i, 0)),
            scratch_shapes=[pltpu.VMEM((2, tm, D), F32), pltpu.VMEM((2, tm, D), F32), pltpu.SemaphoreType.DMA((2,))]),
        out_shape=jax.ShapeDtypeStruct((M, D), F32),
        compiler_params=_cparams("arbitrary"), name="moe_combine",
    )(d0, d1, x, gates, yb)


def _moe(x, g_norm, router, wg, wu, wd, layer, *, n_tok):
    M, D = x.shape
    h, te, tg = _rmsnorm_router(x, g_norm, router, tm=ROW_TILE)
    a = n_tok * TOP_K
    n_blk = -(-a // MOE_BLOCK) + N_EXPERTS
    e_flat = te[:n_tok, :TOP_K].reshape(a)
    onehot = (e_flat[:, None] == jnp.arange(N_EXPERTS, dtype=I32)[None, :]).astype(I32)
    csum = jnp.cumsum(onehot, 0)
    pos = jnp.take_along_axis(csum, e_flat[:, None], 1)[:, 0] - 1
    counts = csum[-1]
    padded = (counts + MOE_BLOCK - 1) // MOE_BLOCK * MOE_BLOCK
    end_pad = jnp.cumsum(padded)
    start_pad = end_pad - padded
    dest = (start_pad[e_flat] + pos).astype(I32)
    tok = jnp.arange(a, dtype=I32) // TOP_K
    row_tok = jnp.full((n_blk * MOE_BLOCK,), n_tok, I32).at[dest].set(tok)
    n_used = (end_pad[-1] // MOE_BLOCK).astype(I32).reshape(1)
    n_seg_max = n_blk // MOE_SEG_BLOCKS + N_EXPERTS + 1
    e_blocks = padded // MOE_BLOCK
    e_segs = (e_blocks + MOE_SEG_BLOCKS - 1) // MOE_SEG_BLOCKS
    seg_end = jnp.cumsum(e_segs)
    s_ids = jnp.arange(n_seg_max, dtype=I32)
    seg_e = jnp.minimum(jnp.sum((seg_end[None, :] <= s_ids[:, None]).astype(I32), 1), N_EXPERTS - 1)
    local = s_ids - (seg_end - e_segs)[seg_e]
    seg_blk = (start_pad // MOE_BLOCK)[seg_e] + local * MOE_SEG_BLOCKS
    seg_nblk = jnp.where(s_ids < seg_end[-1], jnp.clip(e_blocks[seg_e] - local * MOE_SEG_BLOCKS, 0, MOE_SEG_BLOCKS), 0)
    d = jnp.pad(dest.reshape(n_tok, TOP_K), ((0, M - n_tok), (0, 0)))
    row = jnp.arange(M, dtype=I32)[:, None]
    gates = jnp.where(row < n_tok, tg, 0.0)

    xb = _moe_gather(h, row_tok, n_used, n_blk)
    yb = _moe_ffn(xb, wg, wu, wd, layer, seg_e.astype(I32), seg_blk.astype(I32), seg_nblk.astype(I32),
                  seg_end[-1:].astype(I32), n_used, n_seg_max=n_seg_max)
    return _moe_combine(x, gates, yb, d[:, 0], d[:, 1], tm=ROW_TILE_DOWN)


def _cmp_z_body(pt_ref, *refs):
    n_in = len(refs) - 2
    page_refs = refs[:n_in]
    w_ref = refs[n_in]
    o_ref = refs[n_in + 1]
    n_ch = PAGE_SIZE // CMP_STRIDE
    cols = [jnp.concatenate([pg[pl.ds(s, n_ch, stride=CMP_STRIDE), :] for pg in page_refs], 0).astype(BF16)
            for s in range(CMP_STRIDE)]
    acc = _dot(jnp.concatenate(cols, 1), w_ref[...])
    m = PAGES_PER_STEP * n_ch
    for k in range(KV_HEADS):
        o_ref[k] = acc[k * m:(k + 1) * m]


def _cmp_z(pages, table, sec, w1cat):
    nb, npg = table.shape
    assert npg % PAGES_PER_STEP == 0
    n_ch = PAGE_SIZE // CMP_STRIDE
    m = PAGES_PER_STEP * n_ch
    specs = [pl.BlockSpec((None, PAGE_SIZE, HEAD_DIM),
                          lambda b, g, pt, k=k, q=q: (pt[b, g * PAGES_PER_STEP + q], 0, sec * KV_HEADS + k))
             for k in range(KV_HEADS) for q in range(PAGES_PER_STEP)]
    return pl.pallas_call(
        _cmp_z_body,
        grid_spec=pltpu.PrefetchScalarGridSpec(
            num_scalar_prefetch=1, grid=(nb, npg // PAGES_PER_STEP),
            in_specs=specs + [pl.BlockSpec((CMP_STRIDE * HEAD_DIM, CMP_R * HEAD_DIM), lambda b, g, pt: (0, 0))],
            out_specs=pl.BlockSpec((None, KV_HEADS, m, CMP_R * HEAD_DIM), lambda b, g, pt: (b, 0, g, 0))),
        out_shape=jax.ShapeDtypeStruct((nb, KV_HEADS, npg * n_ch, CMP_R * HEAD_DIM), F32),
        compiler_params=_cparams("arbitrary", "arbitrary"), name="cmp_z",
    )(table, *([pages] * len(specs)), w1cat)


def _cmp_cache_z_body(pt_ref, *refs):
    page_refs = refs[:PAGES_PER_STEP]
    w_ref, o_ref = refs[PAGES_PER_STEP:]
    tiles_per_chunk = CMP_STRIDE * KV_HEADS // 8
    lhs = jnp.concatenate(
        [jnp.concatenate([pg[:, 8 * t:8 * t + 8, :].reshape(-1, HEAD_DIM) for t in range(tiles_per_chunk)], 1)
         for pg in page_refs], 0).astype(BF16)
    even, odd = _dot(lhs, w_ref[0]), _dot(lhs, w_ref[1])
    o_ref[...] = even + pltpu.roll(odd, even.shape[0] - KV_HEADS, 0)


def _cmp_cache_fin_body(z_ref, pe_ref, w1_ref, w2_ref, o_ref, y_ref):
    z = z_ref[...]
    n = z.shape[0]
    pe = jnp.broadcast_to(pe_ref[...], (8, pe_ref.shape[1])).astype(BF16)
    pre = _dot(pe, w1_ref[...].astype(BF16))[0:1]
    pre = pre + z[:, :HEAD_DIM]
    pre = pre + pltpu.roll(z[:, HEAD_DIM:], n - 8, 0)
    y_ref[...] = _dot(jax.nn.gelu(pre).astype(BF16), w2_ref[...].astype(BF16))
    for k in range(KV_HEADS):
        o_ref[k] = y_ref[pl.ds(k, n // 8, stride=8), :].astype(o_ref.dtype)


def _compress_cache(cache, table, pe, w1, w2):
    assert KV_HEADS == 4 and CMP_STRIDE % 2 == 0
    nb, npg = table.shape
    assert npg % PAGES_PER_STEP == 0
    n_ch = PAGE_SIZE // CMP_STRIDE
    chunk_rows = CMP_STRIDE * KV_HEADS
    pages = cache.reshape(-1, chunk_rows, HEAD_DIM)
    w = _w1cat(w1).reshape(CMP_STRIDE // 2, 2, HEAD_DIM, CMP_R * HEAD_DIM)
    w = jnp.transpose(w, (1, 0, 2, 3)).reshape(2, CMP_STRIDE // 2 * HEAD_DIM, CMP_R * HEAD_DIM)
    m = PAGES_PER_STEP * n_ch * 8
    z = pl.pallas_call(
        _cmp_cache_z_body,
        grid_spec=pltpu.PrefetchScalarGridSpec(
            num_scalar_prefetch=1, grid=(nb, npg // PAGES_PER_STEP),
            in_specs=[pl.BlockSpec((n_ch, chunk_rows, HEAD_DIM),
                                   lambda b, g, pt, q=q: (pt[b, g * PAGES_PER_STEP + q], 0, 0))
                      for q in range(PAGES_PER_STEP)]
            + [pl.BlockSpec(w.shape, lambda b, g, pt: (0, 0, 0))],
            out_specs=pl.BlockSpec((None, m, CMP_R * HEAD_DIM), lambda b, g, pt: (b, g, 0))),
        out_shape=jax.ShapeDtypeStruct((nb, npg * n_ch * 8, CMP_R * HEAD_DIM), F32),
        compiler_params=_cparams("arbitrary", "arbitrary"), name="cmp_cache_z",
    )(table, *([pages] * PAGES_PER_STEP), w)
    n = npg * n_ch
    return pl.pallas_call(
        _cmp_cache_fin_body, grid=(nb,),
        in_specs=[pl.BlockSpec((None, n * 8, CMP_R * HEAD_DIM), lambda b: (b, 0, 0)),
                  pl.BlockSpec((1, CMP_LEN * HEAD_DIM), lambda b: (0, 0)),
                  pl.BlockSpec((CMP_LEN * HEAD_DIM, HEAD_DIM), lambda b: (0, 0)),
                  pl.BlockSpec((HEAD_DIM, HEAD_DIM), lambda b: (0, 0))],
        out_specs=pl.BlockSpec((None, KV_HEADS, n, HEAD_DIM), lambda b: (b, 0, 0, 0)),
        out_shape=jax.ShapeDtypeStruct((nb, KV_HEADS, n, HEAD_DIM), BF16),
        scratch_shapes=[pltpu.VMEM((n * 8, HEAD_DIM), F32)],
        compiler_params=_cparams("arbitrary"), name="cmp_cache_fin",
    )(z, pe.reshape(1, -1), w1, w2)


def _cmp_fin_body(z_ref, pe_ref, w1_ref, w2_ref, o_ref):
    z = z_ref[...]
    n = z.shape[0]
    pe = jnp.broadcast_to(pe_ref[...], (8, pe_ref.shape[1])).astype(BF16)
    pre = _dot(pe, w1_ref[...].astype(BF16))[0:1]
    pre = pre + z[:, :HEAD_DIM]
    pre = pre + pltpu.roll(z[:, HEAD_DIM:], n - 1, 0)
    o_ref[...] = _dot(jax.nn.gelu(pre).astype(BF16), w2_ref[...].astype(BF16)).astype(o_ref.dtype)


def _cmp_fin(z, pe, w1, w2):
    nb, _, n, _ = z.shape
    return pl.pallas_call(
        _cmp_fin_body, grid=(nb, KV_HEADS),
        in_specs=[pl.BlockSpec((None, None, n, CMP_R * HEAD_DIM), lambda b, k: (b, k, 0, 0)),
                  pl.BlockSpec((1, CMP_LEN * HEAD_DIM), lambda b, k: (0, 0)),
                  pl.BlockSpec((CMP_LEN * HEAD_DIM, HEAD_DIM), lambda b, k: (0, 0)),
                  pl.BlockSpec((HEAD_DIM, HEAD_DIM), lambda b, k: (0, 0))],
        out_specs=pl.BlockSpec((None, None, n, HEAD_DIM), lambda b, k: (b, k, 0, 0)),
        out_shape=jax.ShapeDtypeStruct((nb, KV_HEADS, n, HEAD_DIM), BF16),
        compiler_params=_cparams("arbitrary", "arbitrary"), name="cmp_fin",
    )(z, pe.reshape(1, -1), w1, w2)


def _w1cat(w1):
    w = w1.reshape(CMP_R, CMP_STRIDE, HEAD_DIM, HEAD_DIM)
    return jnp.transpose(w, (1, 2, 0, 3)).reshape(CMP_STRIDE * HEAD_DIM, CMP_R * HEAD_DIM).astype(BF16)


def _compress(pages, table, sec, pe, w1, w2):
    return _cmp_fin(_cmp_z(pages, table, sec, _w1cat(w1)), pe, w1, w2)


def _overlap_matrix(nc, ns, rows, cols):
    cs = np.arange(nc) * CMP_STRIDE
    ss = np.arange(ns) * SEL_BLOCK
    ov = np.minimum(cs[:, None] + CMP_LEN, ss[None, :] + SEL_BLOCK) - np.maximum(cs[:, None], ss[None, :])
    ov = np.clip(ov, 0, None).astype(np.float32) / np.float32(CMP_LEN)
    out = np.zeros((rows, cols), np.float32)
    out[:nc, :ns] = ov
    return jnp.asarray(out)


def _attn_prompt_body(qc_ref, qr_ref, kc_ref, vc_ref, ks_ref, vs_ref, kw_ref, vw_ref, g_ref, ovt_ref, ext_ref, o_ref,
                      allow_ref, vct_ref, vst_ref, vwt_ref, q4_ref, s_ref, ocmp_ref, osel_ref, acc_ref, *,
                      seq, n_cmp, n_sel):
    qi = pl.program_id(2)
    qpos = qi * Q_TILE + lax.broadcasted_iota(I32, (1, Q_TILE), 1)
    row = lax.broadcasted_iota(I32, (LANES, 1), 0)

    @pl.when(qi == 0)
    def _():
        vct_ref[...] = vc_ref[...].astype(F32).T.astype(BF16)
        vst_ref[...] = vs_ref[...].astype(F32).T.astype(BF16)
        vwt_ref[...] = vw_ref[...].astype(F32).T.astype(BF16)

    cmask = (row * CMP_STRIDE + (CMP_LEN - 1) <= qpos) & (row < n_cmp)
    kc = kc_ref[...]
    psum = jnp.zeros((LANES, Q_TILE), F32)
    for p in range(Q_PER_KV):
        s = _dot_nt(kc, qc_ref[:, p * HEAD_DIM:(p + 1) * HEAD_DIM]) * ATTN_SCALE
        sm = jnp.where(cmask, s, NEG)
        e = jnp.exp(sm - jnp.max(sm, 0, keepdims=True))
        pr = jnp.where(cmask, e / jnp.sum(e, 0, keepdims=True), 0.0)
        psum = psum + pr
        ocmp_ref[p] = _dot(vct_ref[...], pr.astype(BF16))

    n_sel_pad = -(-n_sel // 8) * 8
    imp = _dot3(ovt_ref[...], psum)[:n_sel_pad]
    blk = row[:n_sel_pad]
    qblk = qpos >> 6
    valid = (blk * SEL_BLOCK <= qpos) & (blk < n_sel)
    forced = (blk == 0) | (blk == qblk) | (blk == qblk - 1)
    score = jnp.where(forced & valid, FORCE_SCORE, jnp.where(valid, imp, -1.0))
    rank = jnp.zeros(score.shape, F32)
    for j in range(n_sel):
        cj = score[j:j + 1, :]
        beats = (cj > score) | ((cj == score) & (blk > j))
        rank = rank + jnp.where(beats, 1.0, 0.0)
    sel = jnp.where((rank < SEL_TOP) & (score >= 0.0), 1.0, 0.0)
    sel = jnp.concatenate([sel, jnp.zeros((LANES - n_sel_pad, Q_TILE), F32)], 0).astype(BF16)
    hi = (qi * Q_TILE + Q_TILE - 1) // KEY_TILE + 1
    selx = _dot(ext_ref[...], sel)
    kpos_all = lax.broadcasted_iota(I32, (seq, 1), 0)
    allow_ref[...] = jnp.where((selx > 0.5) & (kpos_all <= qpos), 1.0, 0.0)

    q4_ref[...] = jnp.concatenate([qr_ref[:, p * HEAD_DIM:(p + 1) * HEAD_DIM] for p in range(Q_PER_KV)], 0)

    def flash(k_ref, vt_ref, lo, hi, mask_fn):
        acc_ref[...] = jnp.zeros(acc_ref.shape, F32)
        last = seq // KEY_TILE - 1

        def scores(kt):
            start = pl.multiple_of(kt * KEY_TILE, KEY_TILE)
            return _dot_nt(k_ref[pl.ds(start, KEY_TILE), :], q4_ref[...])

        s_ref[lo % 2] = scores(lo)

        def body(kt, carry):
            m, l = carry
            start = pl.multiple_of(kt * KEY_TILE, KEY_TILE)
            s = s_ref[kt % 2] * ATTN_SCALE
            s_ref[(kt + 1) % 2] = scores(jnp.minimum(kt + 1, last))
            mask = mask_fn(start)
            s = jnp.where(jnp.concatenate([mask] * Q_PER_KV, 1), s, NEG)
            m_new = jnp.maximum(m, jnp.max(s, 0, keepdims=True))
            alpha = jnp.exp(m - m_new)
            e = jnp.exp(s - m_new)
            l = alpha * l + jnp.sum(e, 0, keepdims=True)
            acc_ref[...] = alpha * acc_ref[...] + _dot(vt_ref[:, pl.ds(start, KEY_TILE)], e.astype(BF16))
            return m_new, l

        init = (jnp.full((1, Q_PER_KV * Q_TILE), NEG, F32), jnp.zeros((1, Q_PER_KV * Q_TILE), F32))
        return lax.fori_loop(lo, hi, body, init)[1]

    l = flash(ks_ref, vst_ref, 0, hi, lambda start: allow_ref[pl.ds(start, KEY_TILE), :] > 0.5)
    osel_ref[...] = acc_ref[...] / l

    def win_mask(start):
        kp = start + lax.broadcasted_iota(I32, (KEY_TILE, 1), 0)
        return (kp <= qpos) & (kp > qpos - WINDOW)

    l = flash(kw_ref, vwt_ref, jnp.maximum(qi * Q_TILE - (WINDOW - 1), 0) // KEY_TILE, hi, win_mask)
    owin = acc_ref[...] / l

    gt = g_ref[...].T
    for p in range(Q_PER_KV):
        sl = slice(p * Q_TILE, (p + 1) * Q_TILE)
        ot = (gt[p:p + 1] * ocmp_ref[p] + gt[Q_PER_KV + p:Q_PER_KV + p + 1] * osel_ref[:, sl]
              + gt[2 * Q_PER_KV + p:2 * Q_PER_KV + p + 1] * owin[:, sl])
        o_ref[:, p * HEAD_DIM:(p + 1) * HEAD_DIM] = ot.T.astype(o_ref.dtype)


def _attn_prompt(qc, qr, kc, vc, rb, gates, *, batch, seq):
    nq = seq // Q_TILE
    n_chunk = seq // CMP_STRIDE
    n_cmp = n_chunk - CMP_R + 1
    n_sel = seq // SEL_BLOCK
    assert n_chunk == LANES and n_sel <= LANES and seq % KEY_TILE == 0
    ovt = _overlap_matrix(n_cmp, n_sel, LANES, LANES).T
    ext = np.zeros((seq, LANES), np.float32)
    ext[np.arange(seq), np.arange(seq) // SEL_BLOCK] = 1.0
    ext = jnp.asarray(ext, BF16)
    qspec = pl.BlockSpec((Q_TILE, KV_SEC), lambda b, g, q: (b * nq + q, g))
    cspec = pl.BlockSpec((None, None, n_chunk, HEAD_DIM), lambda b, g, q: (b, g, 0, 0))
    kvspec = lambda sec: pl.BlockSpec((seq, HEAD_DIM), lambda b, g, q: (b, sec * KV_HEADS + g))
    body = functools.partial(_attn_prompt_body, seq=seq, n_cmp=n_cmp, n_sel=n_sel)
    return pl.pallas_call(
        body, grid=(batch, KV_HEADS, nq),
        in_specs=[qspec, qspec, cspec, cspec, kvspec(2), kvspec(3), kvspec(4), kvspec(5),
                  pl.BlockSpec((Q_TILE, LANES), lambda b, g, q: (b * nq + q, g)),
                  pl.BlockSpec((LANES, LANES), lambda b, g, q: (0, 0)),
                  pl.BlockSpec((seq, LANES), lambda b, g, q: (0, 0))],
        out_specs=qspec,
        out_shape=jax.ShapeDtypeStruct((batch * seq, N_HEADS * HEAD_DIM), BF16),
        scratch_shapes=[pltpu.VMEM((seq, Q_TILE), F32), pltpu.VMEM((HEAD_DIM, n_chunk), BF16),
                        pltpu.VMEM((HEAD_DIM, seq), BF16), pltpu.VMEM((HEAD_DIM, seq), BF16),
                        pltpu.VMEM((Q_PER_KV * Q_TILE, HEAD_DIM), BF16),
                        pltpu.VMEM((2, KEY_TILE, Q_PER_KV * Q_TILE), F32),
                        pltpu.VMEM((Q_PER_KV, HEAD_DIM, Q_TILE), F32), pltpu.VMEM((HEAD_DIM, Q_PER_KV * Q_TILE), F32),
                        pltpu.VMEM((HEAD_DIM, Q_PER_KV * Q_TILE), F32)],
        compiler_params=_cparams("arbitrary", "arbitrary", "arbitrary"), name="attn_prompt",
    )(qc, qr, kc, vc, rb, rb, rb, rb, gates, ovt, ext)


def _attn_s1_body(q_ref, kc_ref, vc_ref, ov_ref, o_ref, idx_ref, val_ref, sc_ref, *, qpos, n_cmp, n_sel):
    step = pl.program_id(0) * pl.num_programs(1) + pl.program_id(1)
    n_steps = pl.num_programs(0) * pl.num_programs(1)
    rows = q_ref.shape[0]
    n = kc_ref.shape[0]
    col = lax.broadcasted_iota(I32, (1, n), 1)
    cmask = (col * CMP_STRIDE + (CMP_LEN - 1) <= qpos) & (col < n_cmp)
    s = _dot_nt(q_ref[...], kc_ref[...]) * ATTN_SCALE
    sm = jnp.where(cmask, s, NEG)
    e = jnp.exp(sm - jnp.max(sm, -1, keepdims=True))
    pr = jnp.where(cmask, e / jnp.sum(e, -1, keepdims=True), 0.0)
    o_ref[...] = _dot(pr.astype(BF16), vc_ref[...])
    head = lax.broadcasted_iota(I32, (rows, 1), 0)
    psum = jnp.sum(jnp.where(head < Q_PER_KV, pr, 0.0), 0, keepdims=True)
    imp = _dot3(jnp.broadcast_to(psum, (rows, n)), ov_ref[...])[0:1]
    w = ov_ref.shape[1]
    blk = lax.broadcasted_iota(I32, (1, w), 1)
    qblk = qpos // SEL_BLOCK
    valid = (blk * SEL_BLOCK <= qpos) & (blk < n_sel)
    forced = (blk == 0) | (blk == qblk) | (blk == qblk - 1)
    low = -3e38
    sc_ref[pl.ds(step, 1), :] = jnp.where(
        blk < n_sel, jnp.where(forced & valid, FORCE_SCORE, jnp.where(valid, imp, -1.0)), low)

    @pl.when(step == n_steps - 1)
    def _():
        score = sc_ref[...]
        blkf = blk.astype(F32)
        lane = lax.broadcasted_iota(I32, (1, LANES), 1)
        idx_rows = jnp.zeros(idx_ref.shape, F32)
        val_rows = jnp.full(val_ref.shape, -1.0, F32)
        for t in range(SEL_TOP):
            m = jnp.max(score, -1, keepdims=True)
            it = jnp.min(jnp.where(score == m, blkf, float(w)), -1, keepdims=True)
            idx_rows = jnp.where(lane == t, it, idx_rows)
            val_rows = jnp.where(lane == t, m, val_rows)
            score = jnp.where(blkf == it, low, score)
        idx_ref[...] = idx_rows.astype(I32)
        val_ref[...] = val_rows


def _attn_s1(qc, kc, vc, *, qpos):
    nb, _, rows, _ = qc.shape
    n_chunk = kc.shape[2]
    t_all = qpos + 1
    n_cmp = t_all // CMP_STRIDE - CMP_R + 1
    n_sel = -(-t_all // SEL_BLOCK)
    assert n_cmp <= n_chunk and n_sel >= SEL_TOP
    w = -(-n_sel // LANES) * LANES
    ov = _overlap_matrix(n_cmp, n_sel, n_chunk, w)
    body = functools.partial(_attn_s1_body, qpos=qpos, n_cmp=n_cmp, n_sel=n_sel)
    hspec = pl.BlockSpec((None, None, rows, HEAD_DIM), lambda b, g: (b, g, 0, 0))
    cspec = pl.BlockSpec((None, None, n_chunk, HEAD_DIM), lambda b, g: (b, g, 0, 0))
    n_rows = nb * KV_HEADS
    assert n_rows % 8 == 0
    pick_spec = pl.BlockSpec((n_rows, LANES), lambda b, g: (0, 0))
    return pl.pallas_call(
        body, grid=(nb, KV_HEADS),
        in_specs=[hspec, cspec, cspec, pl.BlockSpec((n_chunk, w), lambda b, g: (0, 0))],
        out_specs=[hspec, pick_spec, pick_spec],
        out_shape=[jax.ShapeDtypeStruct((nb, KV_HEADS, rows, HEAD_DIM), F32),
                   jax.ShapeDtypeStruct((n_rows, LANES), I32),
                   jax.ShapeDtypeStruct((n_rows, LANES), F32)],
        scratch_shapes=[pltpu.VMEM((n_rows, w), F32)],
        compiler_params=_cparams("arbitrary", "arbitrary"), name="attn_sample_cmp",
    )(qc, kc, vc, ov)


def _attn_s2_body(phys_ref, idx_ref, ok_ref, new_ref, q_ref, *refs, qpos, n_past_blk, past_len):
    kb_refs = refs[:SEL_TOP]
    vb_refs = refs[SEL_TOP:2 * SEL_TOP]
    kn_ref, vn_ref, wk_ref, wv_ref, kwn_ref, vwn_ref, oc_ref, g_ref, o_ref = refs[2 * SEL_TOP:]
    b, g = pl.program_id(0), pl.program_id(1)
    q = q_ref[...]

    def attend(s, v, sn, vn):
        m = jnp.maximum(jnp.max(s, -1, keepdims=True), jnp.max(sn, -1, keepdims=True))
        e, en = jnp.exp(s - m), jnp.exp(sn - m)
        den = jnp.sum(e, -1, keepdims=True) + jnp.sum(en, -1, keepdims=True)
        return (_dot(e.astype(BF16), v) + _dot(en.astype(BF16), vn)) / den

    def head_rows(ref, n):
        return ref[pl.ds(g, n, stride=KV_HEADS), :].astype(BF16)

    wk = wk_ref.shape[0] // KV_HEADS
    kp = (past_len - wk) + lax.broadcasted_iota(I32, (1, wk), 1)
    s = jnp.where((kp <= qpos) & (kp > qpos - WINDOW), _dot_nt(q, head_rows(wk_ref, wk)) * ATTN_SCALE, NEG)
    sn = _dot_nt(q, kwn_ref[...].astype(BF16)) * ATTN_SCALE
    coln = lax.broadcasted_iota(I32, (1, sn.shape[1]), 1)
    o_win = attend(s, head_rows(wv_ref, wk), jnp.where(coln == 0, sn, NEG), vwn_ref[...].astype(BF16))

    kcat = jnp.concatenate([head_rows(r, SEL_BLOCK) for r in kb_refs], 0)
    vcat = jnp.concatenate([head_rows(r, SEL_BLOCK) for r in vb_refs], 0)
    lane = lax.broadcasted_iota(I32, (1, SEL_TOP * SEL_BLOCK), 1)
    limit = jnp.full(lane.shape, -1, I32)
    for kk in range(SEL_TOP):
        t = (b * KV_HEADS + g) * SEL_TOP + kk
        lim = jnp.where((ok_ref[t] == 1) & (idx_ref[t] < n_past_blk), qpos - idx_ref[t] * SEL_BLOCK, -1)
        limit = jnp.where((lane >> 6) == kk, lim, limit)
    s = jnp.where((lane & (SEL_BLOCK - 1)) <= limit, _dot_nt(q, kcat) * ATTN_SCALE, NEG)
    sn = _dot_nt(q, kn_ref[...].astype(BF16)) * ATTN_SCALE
    sn = jnp.where(coln < new_ref[b * KV_HEADS + g], sn, NEG)
    o_sel = attend(s, vcat, sn, vn_ref[...].astype(BF16))

    gt = g_ref[...]
    o_ref[...] = gt[:, 0:1] * oc_ref[...] + gt[:, 1:2] * o_sel + gt[:, 2:3] * o_win


def _attn_s2(qr, pool_k, pool_v, kn, vn, win_k, win_v, kwn, vwn, o_cmp, gates, phys, idx, ok, has_new, *,
             qpos, past_len):
    nb, _, rows, _ = qr.shape
    wk = win_k.shape[0] // (nb * KV_HEADS)
    assert SEL_BLOCK == 64
    hspec = pl.BlockSpec((None, None, rows, HEAD_DIM), lambda b, g, *_: (b, g, 0, 0))
    bspec = lambda k: pl.BlockSpec((SEL_BLOCK * KV_HEADS, HEAD_DIM),
                                   lambda b, g, ph, *_: (ph[(b * KV_HEADS + g) * SEL_TOP + k], 0))
    wspec = pl.BlockSpec((wk * KV_HEADS, HEAD_DIM), lambda b, g, *_: (b, 0))
    blocks = [bspec(k) for k in range(SEL_TOP)]
    body = functools.partial(_attn_s2_body, qpos=qpos, n_past_blk=past_len // SEL_BLOCK, past_len=past_len)
    return pl.pallas_call(
        body,
        grid_spec=pltpu.PrefetchScalarGridSpec(
            num_scalar_prefetch=4, grid=(nb, KV_HEADS),
            in_specs=[hspec] + blocks + blocks + [hspec, hspec, wspec, wspec, hspec, hspec, hspec, hspec],
            out_specs=hspec),
        out_shape=jax.ShapeDtypeStruct((nb, KV_HEADS, rows, HEAD_DIM), F32),
        compiler_params=_cparams("arbitrary", "arbitrary"), name="attn_sample_sel_win",
    )(phys, idx, ok, has_new, qr, *([pool_k] * SEL_TOP), *([pool_v] * SEL_TOP), kn, vn, win_k, win_v, kwn, vwn,
      o_cmp, gates)


def _head_rows(a, rows=8):
    nb = a.shape[0]
    p = a.shape[1] // (KV_HEADS * HEAD_DIM)
    a = a.reshape(nb, KV_HEADS, p, HEAD_DIM)
    return jnp.pad(a, ((0, 0), (0, 0), (0, rows - p), (0, 0)))


def kernel(x_prompt, x_sample, cache_cmp_k, cache_cmp_v, cache_sel_k, cache_sel_v, state_win_k, state_win_v, page_table, norm_mix, norm_ffn, norm_kv, norm_final, gm_w_in, gm_ln_g, gm_ln_b, gm_w_s, gm_b_s, gm_w_out, nsa_w_in, nsa_w_out, kv_w, cmp_pe_k, cmp_w1_k, cmp_w2_k, cmp_pe_v, cmp_w1_v, cmp_w2_v, ff_w_gate, ff_w_up, ff_w_down, moe_router, moe_w_gate, moe_w_up, moe_w_down):
    B, S, D = x_prompt.shape
    DB, DS, _ = x_sample.shape
    assert DS == 1 and D == D_MODEL and S % KEY_TILE == 0
    n_prompt, n_sample = B * S, DB * DS
    n_tok = n_prompt + n_sample
    M = n_prompt + GM_CHUNK
    assert M % ROW_TILE == 0 and M % ROW_TILE_DOWN == 0 and n_sample <= 8
    past_len = page_table.shape[1] * PAGE_SIZE
    qpos_s = past_len
    wk = state_win_k.shape[1]
    assert wk == min(WINDOW, past_len)

    x = jnp.concatenate([x_prompt.reshape(n_prompt, D), x_sample.reshape(n_sample, D),
                         jnp.zeros((M - n_tok, D), F32)], 0)

    pos = jnp.concatenate([jnp.tile(jnp.arange(S), B), jnp.full((n_sample,), past_len), jnp.zeros((M - n_tok,), I32)])
    half = ROT_DIM // 2
    inv = ROPE_THETA ** (-jnp.arange(half, dtype=F32) / half)
    ang = pos.astype(F32)[:, None] * inv[None, :]
    cos, sin = jnp.cos(ang), jnp.sin(ang)
    zeros_h = jnp.zeros((M, half), F32)
    rest = HEAD_DIM - ROT_DIM
    rope_c = jnp.concatenate([cos, cos, jnp.ones((M, rest), F32)], 1)
    rope_s1 = jnp.concatenate([-sin, zeros_h, jnp.zeros((M, rest), F32)], 1)
    rope_s2 = jnp.concatenate([zeros_h, sin, jnp.zeros((M, rest), F32)], 1)
    rope_extras = [(t, (ROW_TILE, HEAD_DIM), lambda j, i: (i, 0)) for t in (rope_c, rope_s1, rope_s2)]

    def residual_mm(xin, w, layer, res, tm, name):
        n = w.shape[2]
        tn = COL_TILE_WIDE if xin.shape[1] == D_MODEL else COL_TILE
        return _ws_matmul(xin, [(w, layer)], _ep_residual, [(n, F32)], tm=tm, tn=tn, n_cols=n,
                          extras=[(res, (tm, tn), lambda j, i: (i, j))], name=name)[0]

    gm_v = []
    kv = None
    for l in range(DEPTH):
        h = _rmsnorm(x, norm_mix[l], BF16, tm=ROW_TILE)
        if l < N_A_LAYERS:
            z = _ws_matmul(h, [(gm_w_in, l)], _ep_gelu, [(2 * GM_DIM, BF16)], tm=ROW_TILE, tn=COL_TILE_WIDE,
                           n_cols=2 * GM_DIM, name="gmlp_in")[0]
            o, v_s = _gm_gate(z, gm_ln_g[l], gm_ln_b[l], gm_w_s[l], gm_b_s[l], n_prompt=n_prompt, n_sample=n_sample)
            gm_v.append(v_s[:n_sample].reshape(DB, DS, GM_DIM))
            x = residual_mm(o, gm_w_out, l, x, ROW_TILE, "gmlp_out")
        else:
            bl = l - N_A_LAYERS
            qc, qr = _ws_matmul(h, [(nsa_w_in, bl)], _ep_q, [(N_HEADS * HEAD_DIM, BF16)] * 2, tm=ROW_TILE,
                                tn=COL_TILE_WIDE, n_cols=N_HEADS * HEAD_DIM, extras=rope_extras, name="nsa_q")
            wg = nsa_w_in[bl, :, N_HEADS * HEAD_DIM:].reshape(D, N_BRANCH, KV_HEADS, Q_PER_KV)
            wg = jnp.transpose(wg, (0, 2, 1, 3)).reshape(D, KV_HEADS, N_BRANCH * Q_PER_KV)
            wg = jnp.pad(wg, ((0, 0), (0, 0), (0, LANES - N_BRANCH * Q_PER_KV))).reshape(1, D, KV_HEADS * LANES)
            gates = _ws_matmul(h, [(wg, 0)], _ep_sigmoid, [(KV_HEADS * LANES, F32)], tm=ROW_TILE, tn=KV_HEADS * LANES,
                               n_cols=KV_HEADS * LANES, name="nsa_gates")[0]
            o = _attn_prompt(qc, qr, kv['kc_p'], kv['vc_p'], kv['rb'], gates, batch=B, seq=S)
            qc_s = _head_rows(qc[n_prompt:n_tok])
            qr_s = _head_rows(qr[n_prompt:n_tok])
            oc_s, idx_s, val_s = _attn_s1(qc_s, kv['kc_s'], kv['vc_s'], qpos=qpos_s)
            idx = idx_s[:, :SEL_TOP].reshape(DB, KV_HEADS, SEL_TOP)
            ok = val_s[:, :SEL_TOP].reshape(DB, KV_HEADS, SEL_TOP) >= 0.0
            n_past_blk = past_len // SEL_BLOCK
            ppb = PAGE_SIZE // SEL_BLOCK
            ip = jnp.minimum(idx, n_past_blk - 1)
            phys = jnp.take_along_axis(page_table, (ip // ppb).reshape(DB, -1), 1).reshape(ip.shape) * ppb + ip % ppb
            has_new = jnp.any((idx >= n_past_blk) & ok, -1)
            g_s = gates[n_prompt:n_tok].reshape(DB, KV_HEADS, LANES)[:, :, :N_BRANCH * Q_PER_KV]
            g_s = jnp.transpose(g_s.reshape(DB, KV_HEADS, N_BRANCH, Q_PER_KV), (0, 1, 3, 2))
            g_s = jnp.pad(g_s, ((0, 0), (0, 0), (0, 8 - Q_PER_KV), (0, HEAD_DIM - N_BRANCH)))
            o_s = _attn_s2(qr_s, kv['pool_k'], kv['pool_v'], kv['ks_new'], kv['vs_new'], kv['win_k'], kv['win_v'],
                           kv['kw_new'], kv['vw_new'], oc_s, g_s, phys.reshape(-1).astype(I32),
                           idx.reshape(-1).astype(I32), ok.reshape(-1).astype(I32), has_new.reshape(-1).astype(I32),
                           qpos=qpos_s, past_len=past_len)
            o_s = o_s[:, :, :Q_PER_KV].reshape(n_sample, N_HEADS * HEAD_DIM).astype(BF16)
            o = jnp.concatenate([o, o_s, jnp.zeros((M - n_tok, N_HEADS * HEAD_DIM), BF16)], 0)
            x = residual_mm(o, nsa_w_out, bl, x, ROW_TILE, "nsa_out")

        if l % 2 == 0:
            h = _rmsnorm(x, norm_ffn[l], BF16, tm=ROW_TILE)
            hh = _ws_matmul(h, [(ff_w_gate, l // 2), (ff_w_up, l // 2)], _ep_swiglu, [(ff_w_gate.shape[2], BF16)],
                            tm=ROW_TILE, tn=COL_TILE, n_cols=ff_w_gate.shape[2], name="ffn_up")[0]
            x = residual_mm(hh, ff_w_down, l // 2, x, ROW_TILE_DOWN, "ffn_down")
        else:
            x = _moe(x, norm_ffn[l], moe_router[l // 2], moe_w_gate, moe_w_up, moe_w_down, l // 2, n_tok=n_tok)

        if l == N_A_LAYERS - 1:
            hkv = _rmsnorm(x, norm_kv, BF16, tm=ROW_TILE)
            r, rb = _ws_matmul(hkv, [(kv_w.reshape(1, D, 6 * KV_SEC), 0)], _ep_kv,
                               [(6 * KV_SEC, F32), (6 * KV_SEC, BF16)], tm=ROW_TILE, tn=COL_TILE_WIDE,
                               n_cols=6 * KV_SEC, extras=rope_extras, name="kv_proj")
            pages_p = r.reshape(M // PAGE_SIZE, PAGE_SIZE, 6 * KV_SEC)
            table_p = jnp.arange(n_prompt // PAGE_SIZE, dtype=I32).reshape(B, S // PAGE_SIZE)
            r_s = r[n_prompt:n_tok]
            sec_s = lambda i: r_s[:, i * KV_SEC:(i + 1) * KV_SEC]
            rows2d = lambda c: c.reshape(-1, HEAD_DIM)
            kv = dict(
                rb=rb,
                kc_p=_compress(pages_p, table_p, 0, cmp_pe_k, cmp_w1_k, cmp_w2_k),
                vc_p=_compress(pages_p, table_p, 1, cmp_pe_v, cmp_w1_v, cmp_w2_v),
                kc_s=_compress_cache(cache_cmp_k, page_table, cmp_pe_k, cmp_w1_k, cmp_w2_k),
                vc_s=_compress_cache(cache_cmp_v, page_table, cmp_pe_v, cmp_w1_v, cmp_w2_v),
                pool_k=rows2d(cache_sel_k), pool_v=rows2d(cache_sel_v),
                ks_new=_head_rows(sec_s(2)), vs_new=_head_rows(sec_s(3)),
                kw_new=_head_rows(sec_s(4)), vw_new=_head_rows(sec_s(5)),
                win_k=rows2d(state_win_k), win_v=rows2d(state_win_v),
            )
            r_p = r[:n_prompt]
            pg = lambda i: r_p[:, i * KV_SEC:(i + 1) * KV_SEC].reshape(B, S // PAGE_SIZE, PAGE_SIZE, KV_HEADS, HEAD_DIM)
            wkp = min(WINDOW, S)
            tail = lambda i: r_p[:, i * KV_SEC:(i + 1) * KV_SEC].reshape(B, S, KV_HEADS, HEAD_DIM)[:, S - wkp:]
            row_s = lambda i: sec_s(i).reshape(DB, DS, KV_HEADS, HEAD_DIM)
            state_p = (pg(0), pg(1), pg(2), pg(3), tail(4), tail(5))
            state_s = (row_s(0), row_s(1), row_s(2), row_s(3),
                       jnp.concatenate([state_win_k, row_s(4)], 1)[:, DS:],
                       jnp.concatenate([state_win_v, row_s(5)], 1)[:, DS:])

    y_prompt = _rmsnorm(x, norm_final, F32, tm=S // 2, n_rows=n_prompt).reshape(B, S, D)
    y_sample = _rmsnorm(x, norm_final, F32, tm=8, first_row=n_prompt, n_rows=8)[:n_sample].reshape(DB, DS, D)
    return (y_prompt, y_sample) + state_p + state_s + (jnp.stack(gm_v, 0),)
```
